```python
import jax, jax.numpy as jnp
from jax import lax
import numpy as np

D_MODEL = 4096
BATCH = 4
SEQ = 2048
DEPTH = 1

HEAD_DIM = 128
A_HEADS = 16
A_KV_HEADS = 4
WINDOW = 128
A_BLOCK = 128
ROT_DIM = HEAD_DIM // 4
ROPE_THETA = 500000.0
B_HEADS = 16
GRID_W = 64
NA_KH_MAX = 8
NA_KW = 16
MEM_LEN = 256
X_HEADS = 4
N_BRANCH = 2
N_EXPERTS = 16
EC_CAPACITY = 2
D_EXPERT = D_MODEL // 2
EPS = 1e-6
NEG = -1e30

QA_W = A_HEADS * HEAD_DIM
KVA_W = A_KV_HEADS * HEAD_DIM
QB_W = B_HEADS * HEAD_DIM
GATE_W = N_BRANCH * D_MODEL
IN_WIDTHS = [QA_W, KVA_W, KVA_W, QB_W, QB_W, QB_W, GATE_W]
IN_W = sum(IN_WIDTHS)
IN_SPLITS = [int(v) for v in np.cumsum(IN_WIDTHS)[:-1]]
X_W = X_HEADS * HEAD_DIM

kernel_name = 'hybrid_window_natten_ec_moe_encoder'


def rms_norm(x, g):
    xf = x.astype(jnp.float32)
    y = xf * lax.rsqrt(jnp.mean(xf * xf, axis=-1, keepdims=True) + EPS)
    return (y * g.astype(jnp.float32)).astype(x.dtype)


def partial_rotary(x, pos):
    half = ROT_DIM // 2
    inv = ROPE_THETA ** (-jnp.arange(half, dtype=jnp.float32) * 2.0 / ROT_DIM)
    ang = pos.astype(jnp.float32)[:, None] * inv[None, :]
    cos = jnp.cos(ang)[None, :, None, :]
    sin = jnp.sin(ang)[None, :, None, :]
    xr = x[..., :ROT_DIM].astype(jnp.float32)
    x1, x2 = xr[..., :half], xr[..., half:]
    rot = jnp.concatenate([x1 * cos - x2 * sin, x2 * cos + x1 * sin], axis=-1).astype(x.dtype)
    return jnp.concatenate([rot, x[..., ROT_DIM:]], axis=-1)


def window_gqa(q, k, v, sink):
    B, S, Hq, hd = q.shape
    Hkv = k.shape[2]
    G = Hq // Hkv
    nb = S // A_BLOCK
    pad = ((0, 0), (A_BLOCK, A_BLOCK), (0, 0), (0, 0))
    kp = jnp.pad(k, pad).reshape(B, nb + 2, A_BLOCK, Hkv, hd)
    vp = jnp.pad(v, pad).reshape(B, nb + 2, A_BLOCK, Hkv, hd)
    kb = jnp.concatenate([kp[:, :-2], kp[:, 1:-1], kp[:, 2:]], axis=2)
    vb = jnp.concatenate([vp[:, :-2], vp[:, 1:-1], vp[:, 2:]], axis=2)
    qb = q.reshape(B, nb, A_BLOCK, Hkv, G, hd)
    s = jnp.einsum('bnqkgd,bnckd->bnkgqc', qb, kb).astype(jnp.float32) * (hd ** -0.5)
    qi = jnp.arange(A_BLOCK)[:, None]
    kc = jnp.arange(3 * A_BLOCK)[None, :]
    in_band = jnp.abs(kc - A_BLOCK - qi) <= WINDOW
    key_abs = jnp.arange(nb)[:, None] * A_BLOCK - A_BLOCK + jnp.arange(3 * A_BLOCK)[None, :]
    key_ok = (key_abs >= 0) & (key_abs < S)
    mask = in_band[None] & key_ok[:, None, :]
    s = jnp.where(mask[None, :, None, None], s, NEG)
    sk = sink.astype(jnp.float32).reshape(Hkv, G)[None, None, :, :, None, None]
    sk = jnp.broadcast_to(sk, s.shape[:-1] + (1,))
    p = jax.nn.softmax(jnp.concatenate([s, sk], axis=-1), axis=-1)[..., :-1]
    o = jnp.einsum('bnkgqc,bnckd->bnqkgd', p.astype(v.dtype), vb)
    return o.reshape(B, S, Hq, hd)


def neighbourhood_attn(q, k, v, rpb):
    B, S, H, hd = q.shape
    rows = S // GRID_W
    kh = min(NA_KH_MAX, rows)
    kw = NA_KW
    r = jnp.arange(rows)
    c = jnp.arange(GRID_W)
    rs = jnp.clip(r - kh // 2, 0, rows - kh)
    cs = jnp.clip(c - kw // 2, 0, GRID_W - kw)
    krow = rs[:, None] + jnp.arange(kh)[None, :]
    tok = (krow[:, :, None] * GRID_W + c[None, None, :]).reshape(rows, kh * GRID_W)
    ks = jnp.take(k, tok, axis=1)
    vs = jnp.take(v, tok, axis=1)
    qr = q.reshape(B, rows, GRID_W, H, hd)
    s = jnp.einsum('brqhd,brkhd->bhrqk', qr, ks).astype(jnp.float32) * (hd ** -0.5)
    key_r = jnp.repeat(krow, GRID_W, axis=1)
    key_c = jnp.tile(c, kh)
    dr = key_r[:, None, :] - r[:, None, None]
    dc = key_c[None, None, :] - c[None, :, None]
    bias = rpb[:, dr + NA_KH_MAX - 1, jnp.clip(dc + kw - 1, 0, 2 * kw - 2)]
    col_ok = (key_c[None, :] >= cs[:, None]) & (key_c[None, :] < cs[:, None] + kw)
    s = jnp.where(col_ok[None, None, None], s + bias.astype(jnp.float32)[None], NEG)
    p = jax.nn.softmax(s, axis=-1)
    o = jnp.einsum('bhrqk,brkhd->brqhd', p.astype(v.dtype), vs)
    return o.reshape(B, S, H, hd)


def memory_cross_attn(h, m, wq, wk, wv, wo):
    B, S, _ = h.shape
    M = m.shape[1]
    q = (h @ wq).reshape(B, S, X_HEADS, HEAD_DIM)
    k = (m @ wk).reshape(B, M, X_HEADS, HEAD_DIM)
    v = (m @ wv).reshape(B, M, X_HEADS, HEAD_DIM)
    s = jnp.einsum('bshd,bmhd->bhsm', q, k).astype(jnp.float32) * (HEAD_DIM ** -0.5)
    p = jax.nn.softmax(s, axis=-1)
    o = jnp.einsum('bhsm,bmhd->bshd', p.astype(v.dtype), v).reshape(B, S, X_W)
    return o @ wo


def expert_choice_ffn(h, w_router, w_gate, w_up, w_down):
    B, S, D = h.shape
    cap = EC_CAPACITY * S // N_EXPERTS
    aff = jax.nn.softmax((h @ w_router).astype(jnp.float32), axis=-1)
    val, idx = lax.top_k(jnp.swapaxes(aff, 1, 2), cap)
    xg = jax.vmap(lambda hb, ib: hb[ib])(h, idx)
    a = jnp.einsum('becd,edf->becf', xg, w_gate)
    u = jnp.einsum('becd,edf->becf', xg, w_up)
    y = jnp.einsum('becf,efd->becd', jax.nn.silu(a) * u, w_down)
    y = y * val[..., None].astype(h.dtype)
    bidx = jnp.arange(B)[:, None, None]
    return jnp.zeros_like(h).at[bidx, idx].add(y)


def setup_inputs(seed: int = 0) -> dict:
    key = jax.random.key(seed)
    ks = jax.random.split(key, 24)
    f32 = jnp.float32
    L = DEPTH

    def nrm(k, shape, scale):
        return jax.random.normal(k, shape, f32) * scale

    def gain(k, shape):
        return 1.0 + 0.05 * jax.random.normal(k, shape, f32)

    return {
        'x': jax.random.normal(ks[0], (BATCH, SEQ, D_MODEL), f32),
        'mem': jax.random.normal(ks[1], (BATCH, MEM_LEN, D_MODEL), f32),
        'norm_mix': gain(ks[2], (L, D_MODEL)),
        'w_in': nrm(ks[3], (L, D_MODEL, IN_W), D_MODEL ** -0.5),
        'b_gate': nrm(ks[4], (L, GATE_W), 0.1),
        'sink': nrm(ks[5], (L, A_HEADS), 0.5),
        'rpb': nrm(ks[6], (L, B_HEADS, 2 * NA_KH_MAX - 1, 2 * NA_KW - 1), 0.2),
        'w_branch_a': nrm(ks[7], (L, QA_W, D_MODEL), QA_W ** -0.5),
        'w_branch_b': nrm(ks[8], (L, QB_W, D_MODEL), QB_W ** -0.5),
        'w_out': nrm(ks[9], (L, D_MODEL, D_MODEL), D_MODEL ** -0.5),
        'norm_cross': gain(ks[10], (L, D_MODEL)),
        'norm_mem': gain(ks[11], (L, D_MODEL)),
        'wq_x': nrm(ks[12], (L, D_MODEL, X_W), D_MODEL ** -0.5),
        'wk_x': nrm(ks[13], (L, D_MODEL, X_W), D_MODEL ** -0.5),
        'wv_x': nrm(ks[14], (L, D_MODEL, X_W), D_MODEL ** -0.5),
        'wo_x': nrm(ks[15], (L, X_W, D_MODEL), X_W ** -0.5),
        'norm_ffn': gain(ks[16], (L, D_MODEL)),
        'w_router': nrm(ks[17], (L, D_MODEL, N_EXPERTS), D_MODEL ** -0.5),
        'w_gate': nrm(ks[18], (L, N_EXPERTS, D_MODEL, D_EXPERT), D_MODEL ** -0.5),
        'w_up': nrm(ks[19], (L, N_EXPERTS, D_MODEL, D_EXPERT), D_MODEL ** -0.5),
        'w_down': nrm(ks[20], (L, N_EXPERTS, D_EXPERT, D_MODEL), D_EXPERT ** -0.5),
        'norm_final': gain(ks[21], (D_MODEL,)),
    }


def reference(x, mem, norm_mix, w_in, b_gate, sink, rpb, w_branch_a, w_branch_b, w_out,
              norm_cross, norm_mem, wq_x, wk_x, wv_x, wo_x, norm_ffn, w_router,
              w_gate, w_up, w_down, norm_final):
    B, S, D = x.shape
    pos = jnp.arange(S)
    for l in range(DEPTH):
        h = rms_norm(x, norm_mix[l])
        proj = h @ w_in[l]
        qa, ka, va, qb, kb, vb, gates = jnp.split(proj, IN_SPLITS, axis=-1)
        qa = partial_rotary(qa.reshape(B, S, A_HEADS, HEAD_DIM), pos)
        ka = partial_rotary(ka.reshape(B, S, A_KV_HEADS, HEAD_DIM), pos)
        va = va.reshape(B, S, A_KV_HEADS, HEAD_DIM)
        ya = window_gqa(qa, ka, va, sink[l]).reshape(B, S, QA_W) @ w_branch_a[l]
        yb = neighbourhood_attn(qb.reshape(B, S, B_HEADS, HEAD_DIM),
                                kb.reshape(B, S, B_HEADS, HEAD_DIM),
                                vb.reshape(B, S, B_HEADS, HEAD_DIM), rpb[l]).reshape(B, S, QB_W) @ w_branch_b[l]
        g = jax.nn.sigmoid((gates + b_gate[l]).astype(jnp.float32)).astype(x.dtype).reshape(B, S, N_BRANCH, D)
        x = x + (g[:, :, 0] * ya + g[:, :, 1] * yb) @ w_out[l]
        h = rms_norm(x, norm_cross[l])
        m = rms_norm(mem, norm_mem[l])
        x = x + memory_cross_attn(h, m, wq_x[l], wk_x[l], wv_x[l], wo_x[l])
        h = rms_norm(x, norm_ffn[l])
        x = x + expert_choice_ffn(h, w_router[l], w_gate[l], w_up[l], w_down[l])
    return rms_norm(x, norm_final)
```

```python
import functools

import jax
import jax.numpy as jnp
from jax import lax
from jax.experimental import pallas as pl
from jax.experimental.pallas import tpu as pltpu

F32 = jnp.float32
BF16 = jnp.bfloat16

HEAD_DIM = 128
A_HEADS = 16
A_KV_HEADS = 4
A_GROUP = A_HEADS // A_KV_HEADS
WINDOW = 128
A_BLOCK = 128
ROT_DIM = HEAD_DIM // 4
ROPE_THETA = 500000.0
B_HEADS = 16
GRID_W = 64
NA_KH_MAX = 8
NA_KW = 16
X_HEADS = 4
N_EXPERTS = 16
EC_CAPACITY = 2
EPS = 1e-6
NEG = -1e30
LANES = 128
MIB = 1024 * 1024

QA_W = A_HEADS * HEAD_DIM
KVA_W = A_KV_HEADS * HEAD_DIM
QB_W = B_HEADS * HEAD_DIM
X_W = X_HEADS * HEAD_DIM

_NT = (((1,), (1,)), ((), ()))
_TN = (((0,), (0,)), ((), ()))


def _params(semantics, vmem_mib):
    return pltpu.CompilerParams(dimension_semantics=semantics,
                                vmem_limit_bytes=vmem_mib * MIB)


def _cast_rows(src_ref, dst_ref, rows, chunk=256):
    def body(k, carry):
        r = pl.multiple_of(k * chunk, chunk)
        dst_ref[pl.ds(r, chunk), :] = src_ref[pl.ds(r, chunk), :].astype(dst_ref.dtype)
        return carry
    lax.fori_loop(0, rows // chunk, body, 0)


def _rmsnorm_rows(x, g):
    ms = jnp.mean(x * x, axis=-1, keepdims=True)
    return x * lax.rsqrt(ms + EPS) * g


def _rmsnorm_body(x_ref, g_ref, o_ref):
    o_ref[...] = _rmsnorm_rows(x_ref[...], g_ref[...]).astype(o_ref.dtype)


def rmsnorm(x2d, g, out_dtype, bm=256):
    m, d = x2d.shape
    return pl.pallas_call(
        _rmsnorm_body,
        grid=(m // bm,),
        in_specs=[pl.BlockSpec((bm, d), lambda i: (i, 0)),
                  pl.BlockSpec((1, d), lambda i: (0, 0))],
        out_specs=pl.BlockSpec((bm, d), lambda i: (i, 0)),
        out_shape=jax.ShapeDtypeStruct((m, d), out_dtype),
        compiler_params=_params(("arbitrary",), 40),
        name="rmsnorm",
    )(x2d, g.reshape(1, d))


def _ep_store(acc, o_ref):
    o_ref[...] = acc.astype(o_ref.dtype)


def _ep_residual(acc, o_ref, r_ref):
    o_ref[...] = (r_ref[...] + acc).astype(o_ref.dtype)


def _ep_sigmoid(acc, o_ref, b_ref):
    o_ref[...] = jax.nn.sigmoid(acc + b_ref[...]).astype(o_ref.dtype)


def _ep_rotary(acc, o_ref, c_ref, s1_ref, s2_ref):
    c, s1, s2 = c_ref[...], s1_ref[...], s2_ref[...]
    half = ROT_DIM // 2
    for h in range(acc.shape[1] // HEAD_DIM):
        a = acc[:, h * HEAD_DIM:(h + 1) * HEAD_DIM]
        r = a * c + pltpu.roll(a, HEAD_DIM - half, 1) * s1 + pltpu.roll(a, half, 1) * s2
        o_ref[:, h * HEAD_DIM:(h + 1) * HEAD_DIM] = r.astype(o_ref.dtype)


def _mm_body(*refs, n_extra, epilogue, k_rows):
    a_ref, w_ref = refs[0], refs[1]
    extra = refs[2:2 + n_extra]
    o_ref = refs[2 + n_extra]
    wb_ref = refs[3 + n_extra]

    @pl.when(pl.program_id(1) == 0)
    def _():
        _cast_rows(w_ref, wb_ref, k_rows)

    acc = jnp.dot(a_ref[...], wb_ref[...], preferred_element_type=F32)
    epilogue(acc, o_ref, *extra)


def matmul(a, w, *, col_off, n_cols, bm, bn, out_dtype, epilogue=_ep_store,
           extras=(), extra_specs=(), vmem_mib=56, name="matmul"):
    m, k = a.shape
    off = col_off // bn
    assert col_off % bn == 0 and n_cols % bn == 0 and m % bm == 0
    body = functools.partial(_mm_body, n_extra=len(extras), epilogue=epilogue, k_rows=k)
    return pl.pallas_call(
        body,
        grid=(n_cols // bn, m // bm),
        in_specs=[pl.BlockSpec((bm, k), lambda j, i: (i, 0)),
                  pl.BlockSpec((k, bn), lambda j, i: (0, j + off))] + list(extra_specs),
        out_specs=pl.BlockSpec((bm, bn), lambda j, i: (i, j)),
        out_shape=jax.ShapeDtypeStruct((m, n_cols), out_dtype),
        scratch_shapes=[pltpu.VMEM((k, bn), BF16)],
        compiler_params=_params(("arbitrary", "arbitrary"), vmem_mib),
        name=name,
    )(a, w, *extras)


def _softmax_parts(parts, extra_col=None):
    m = parts[0].max(axis=1, keepdims=True)
    for p in parts[1:]:
        m = jnp.maximum(m, p.max(axis=1, keepdims=True))
    if extra_col is not None:
        m = jnp.maximum(m, extra_col)
    es = [jnp.exp(p - m) for p in parts]
    den = es[0].sum(axis=1, keepdims=True)
    for e in es[1:]:
        den = den + e.sum(axis=1, keepdims=True)
    if extra_col is not None:
        den = den + jnp.exp(extra_col - m)
    inv = 1.0 / den
    return [e * inv for e in es]


def _win_body(sink_ref, q_ref, k_ref, v_ref, o_ref, *, seq):
    kv = pl.program_id(1)
    nb = seq // A_BLOCK
    scale = HEAD_DIM ** -0.5
    rows = A_GROUP * A_BLOCK
    qi = lax.broadcasted_iota(jnp.int32, (rows, A_BLOCK), 0) % A_BLOCK
    ci = lax.broadcasted_iota(jnp.int32, (rows, A_BLOCK), 1)
    sink_col = jnp.concatenate(
        [jnp.full((A_BLOCK, 1), sink_ref[kv * A_GROUP + g], F32) for g in range(A_GROUP)], axis=0)

    def body(n, carry):
        r0 = pl.multiple_of(n * A_BLOCK, A_BLOCK)
        rp = pl.multiple_of(jnp.maximum(n - 1, 0) * A_BLOCK, A_BLOCK)
        rn = pl.multiple_of(jnp.minimum(n + 1, nb - 1) * A_BLOCK, A_BLOCK)
        off_p = jnp.where(n > 0, 0, 2 * A_BLOCK)
        off_n = jnp.where(n < nb - 1, 0, 2 * A_BLOCK)
        q = jnp.concatenate(
            [q_ref[pl.ds(r0, A_BLOCK), g * HEAD_DIM:(g + 1) * HEAD_DIM] for g in range(A_GROUP)],
            axis=0)
        sp = lax.dot_general(q, k_ref[pl.ds(rp, A_BLOCK), :], _NT, preferred_element_type=F32) * scale
        sc = lax.dot_general(q, k_ref[pl.ds(r0, A_BLOCK), :], _NT, preferred_element_type=F32) * scale
        sn = lax.dot_general(q, k_ref[pl.ds(rn, A_BLOCK), :], _NT, preferred_element_type=F32) * scale
        sp = jnp.where(ci >= qi + off_p, sp, NEG)
        sn = jnp.where(ci <= qi - off_n, sn, NEG)
        pp, pc, pn = _softmax_parts([sp, sc, sn], sink_col)
        o = jnp.dot(pp.astype(BF16), v_ref[pl.ds(rp, A_BLOCK), :], preferred_element_type=F32)
        o = o + jnp.dot(pc.astype(BF16), v_ref[pl.ds(r0, A_BLOCK), :], preferred_element_type=F32)
        o = o + jnp.dot(pn.astype(BF16), v_ref[pl.ds(rn, A_BLOCK), :], preferred_element_type=F32)
        for g in range(A_GROUP):
            o_ref[pl.ds(r0, A_BLOCK), g * HEAD_DIM:(g + 1) * HEAD_DIM] = (
                o[g * A_BLOCK:(g + 1) * A_BLOCK].astype(o_ref.dtype))
        return carry

    lax.fori_loop(0, nb, body, 0)


def window_attention(qk, vqkv, sink, batch, seq):
    gw = A_GROUP * HEAD_DIM
    k_blk0 = QA_W // HEAD_DIM
    return pl.pallas_call(
        functools.partial(_win_body, seq=seq),
        grid=(batch, A_KV_HEADS),
        in_specs=[pl.BlockSpec(memory_space=pltpu.SMEM),
                  pl.BlockSpec((seq, gw), lambda b, h: (b, h)),
                  pl.BlockSpec((seq, HEAD_DIM), lambda b, h: (b, k_blk0 + h)),
                  pl.BlockSpec((seq, HEAD_DIM), lambda b, h: (b, h))],
        out_specs=pl.BlockSpec((seq, gw), lambda b, h: (b, h)),
        out_shape=jax.ShapeDtypeStruct((batch * seq, QA_W), BF16),
        compiler_params=_params(("arbitrary", "arbitrary"), 32),
        name="window_attention",
    )(sink, qk, qk, vqkv)


NBR_HG = 4


def _nbr_body(q_ref, k_ref, v_ref, bias_ref, o_ref, *, seq):
    rows = seq // GRID_W
    kh = min(NA_KH_MAX, rows)
    strip = kh * GRID_W
    scale = HEAD_DIM ** -0.5

    def body(r, carry):
        rs = jnp.clip(r - kh // 2, 0, rows - kh)
        var = r - rs
        q0 = pl.multiple_of(r * GRID_W, GRID_W)
        k0 = pl.multiple_of(rs * GRID_W, GRID_W)
        for h in range(NBR_HG):
            cols = slice(h * HEAD_DIM, (h + 1) * HEAD_DIM)
            q = q_ref[pl.ds(q0, GRID_W), cols]
            k = k_ref[pl.ds(k0, strip), cols]
            v = v_ref[pl.ds(k0, strip), cols]
            s = lax.dot_general(q, k, _NT, preferred_element_type=F32) * scale + bias_ref[h, var]
            (p,) = _softmax_parts([s])
            o = jnp.dot(p.astype(BF16), v, preferred_element_type=F32)
            o_ref[pl.ds(q0, GRID_W), cols] = o.astype(o_ref.dtype)
        return carry

    lax.fori_loop(0, rows, body, 0)


def neighbourhood_attention(vqkv, bias_tbl, batch, seq):
    gw = NBR_HG * HEAD_DIM
    q0, k0, v0 = KVA_W // gw, (KVA_W + QB_W) // gw, (KVA_W + 2 * QB_W) // gw
    kh = bias_tbl.shape[1]
    return pl.pallas_call(
        functools.partial(_nbr_body, seq=seq),
        grid=(B_HEADS // NBR_HG, batch),
        in_specs=[pl.BlockSpec((seq, gw), lambda g, b: (b, q0 + g)),
                  pl.BlockSpec((seq, gw), lambda g, b: (b, k0 + g)),
                  pl.BlockSpec((seq, gw), lambda g, b: (b, v0 + g)),
                  pl.BlockSpec((NBR_HG, kh, GRID_W, kh * GRID_W), lambda g, b: (g, 0, 0, 0))],
        out_specs=pl.BlockSpec((seq, gw), lambda g, b: (b, g)),
        out_shape=jax.ShapeDtypeStruct((batch * seq, QB_W), BF16),
        compiler_params=_params(("arbitrary", "arbitrary"), 40),
        name="neighbourhood_attention",
    )(vqkv, vqkv, vqkv, bias_tbl)


def _nbr_bias_table(rpb, seq):
    rows = seq // GRID_W
    kh = min(NA_KH_MAX, rows)
    kw = NA_KW
    c = jnp.arange(GRID_W)
    j = jnp.arange(kh)
    var = jnp.arange(kh)
    dr_idx = j[None, :] - var[:, None] + NA_KH_MAX - 1
    dc_idx = jnp.clip(c[None, :] - c[:, None] + kw - 1, 0, 2 * kw - 2)
    tbl = rpb[:, dr_idx[:, :, None, None], dc_idx[None, None, :, :]]
    cs = jnp.clip(c - kw // 2, 0, GRID_W - kw)
    col_ok = (c[None, :] >= cs[:, None]) & (c[None, :] < cs[:, None] + kw)
    tbl = jnp.where(col_ok[None, None, None], tbl.astype(F32), NEG)
    return tbl.transpose(0, 1, 3, 2, 4).reshape(rpb.shape[0], kh, GRID_W, kh * GRID_W)


def _merge_body(oa_ref, ob_ref, wa_ref, wb_ref, g0_ref, g1_ref, o_ref, wa_s, wb_s, *, k_rows):
    @pl.when(pl.program_id(1) == 0)
    def _():
        _cast_rows(wa_ref, wa_s, k_rows)
        _cast_rows(wb_ref, wb_s, k_rows)

    ya = jnp.dot(oa_ref[...], wa_s[...], preferred_element_type=F32)
    yb = jnp.dot(ob_ref[...], wb_s[...], preferred_element_type=F32)
    o_ref[...] = (g0_ref[...].astype(F32) * ya + g1_ref[...].astype(F32) * yb).astype(o_ref.dtype)


def branch_merge(oa, ob, wa, wb, gates, bm=1024, bn=512):
    m, k = oa.shape
    n = wa.shape[1]
    g1_off = n // bn
    return pl.pallas_call(
        functools.partial(_merge_body, k_rows=k),
        grid=(n // bn, m // bm),
        in_specs=[pl.BlockSpec((bm, k), lambda j, i: (i, 0)),
                  pl.BlockSpec((bm, k), lambda j, i: (i, 0)),
                  pl.BlockSpec((k, bn), lambda j, i: (0, j)),
                  pl.BlockSpec((k, bn), lambda j, i: (0, j)),
                  pl.BlockSpec((bm, bn), lambda j, i: (i, j)),
                  pl.BlockSpec((bm, bn), lambda j, i: (i, j + g1_off))],
        out_specs=pl.BlockSpec((bm, bn), lambda j, i: (i, j)),
        out_shape=jax.ShapeDtypeStruct((m, n), BF16),
        scratch_shapes=[pltpu.VMEM((k, bn), BF16), pltpu.VMEM((k, bn), BF16)],
        compiler_params=_params(("arbitrary", "arbitrary"), 56),
        name="branch_merge",
    )(oa, ob, wa, wb, gates, gates)


def _xattn_body(q_ref, k_ref, v_ref, o_ref):
    scale = HEAD_DIM ** -0.5
    for h in range(X_HEADS):
        cols = slice(h * HEAD_DIM, (h + 1) * HEAD_DIM)
        s = lax.dot_general(q_ref[:, cols], k_ref[:, cols], _NT, preferred_element_type=F32) * scale
        (p,) = _softmax_parts([s])
        o = jnp.dot(p.astype(BF16), v_ref[:, cols], preferred_element_type=F32)
        o_ref[:, cols] = o.astype(o_ref.dtype)


def cross_attention(q, k, v, batch, seq, mem_len, tq=512):
    nt = seq // tq
    return pl.pallas_call(
        _xattn_body,
        grid=(batch, nt),
        in_specs=[pl.BlockSpec((tq, X_W), lambda b, i: (b * nt + i, 0)),
                  pl.BlockSpec((mem_len, X_W), lambda b, i: (b, 0)),
                  pl.BlockSpec((mem_len, X_W), lambda b, i: (b, 0))],
        out_specs=pl.BlockSpec((tq, X_W), lambda b, i: (b * nt + i, 0)),
        out_shape=jax.ShapeDtypeStruct((batch * seq, X_W), BF16),
        compiler_params=_params(("arbitrary", "arbitrary"), 32),
        name="cross_attention",
    )(q, k, v)


def _router_body(x_ref, g_ref, wr_ref, h_ref, aff_ref):
    hn = _rmsnorm_rows(x_ref[...], g_ref[...])
    h_ref[...] = hn.astype(h_ref.dtype)
    logits = jnp.dot(hn, wr_ref[...], precision=lax.Precision.HIGHEST,
                     preferred_element_type=F32)
    lane = lax.broadcasted_iota(jnp.int32, logits.shape, 1)
    logits = jnp.where(lane < N_EXPERTS, logits, NEG)
    (aff,) = _softmax_parts([logits])
    aff_ref[0] = aff.T[:N_EXPERTS, :]


def router_norm(x2d, g, w_router, batch, seq, bm=256):
    m, d = x2d.shape
    nt = seq // bm
    wr = jnp.pad(w_router, ((0, 0), (0, LANES - N_EXPERTS)))
    return pl.pallas_call(
        _router_body,
        grid=(m // bm,),
        in_specs=[pl.BlockSpec((bm, d), lambda i: (i, 0)),
                  pl.BlockSpec((1, d), lambda i: (0, 0)),
                  pl.BlockSpec((d, LANES), lambda i: (0, 0))],
        out_specs=[pl.BlockSpec((bm, d), lambda i: (i, 0)),
                   pl.BlockSpec((1, N_EXPERTS, bm), lambda i: (i // nt, 0, i % nt))],
        out_shape=[jax.ShapeDtypeStruct((m, d), BF16),
                   jax.ShapeDtypeStruct((batch, N_EXPERTS, seq), F32)],
        compiler_params=_params(("arbitrary",), 40),
        name="router_norm",
    )(x2d, g.reshape(1, d), wr)


CUM_CHUNK = 256


def _excl_cumsum_lanes(x01, tri):
    n = x01.shape[1]
    carry = jnp.zeros((x01.shape[0], 1), F32)
    out = []
    for c in range(n // CUM_CHUNK):
        xc = x01[:, c * CUM_CHUNK:(c + 1) * CUM_CHUNK]
        out.append(jnp.dot(xc.astype(BF16), tri, preferred_element_type=F32) + carry)
        carry = carry + xc.sum(axis=1, keepdims=True)
    return jnp.concatenate(out, axis=1)


def _topk_body(aff_ref, slot_ref, *, cap):
    a = aff_ref[0]
    bits = pltpu.bitcast(a, jnp.int32)
    capf = jnp.float32(cap)

    def search(i, t):
        cand = t | jnp.left_shift(jnp.int32(1), 30 - i)
        cnt = jnp.where(bits >= cand, 1.0, 0.0).sum(axis=1, keepdims=True)
        return jnp.where(cnt >= capf, cand, t)

    t = lax.fori_loop(0, 31, search, jnp.zeros((a.shape[0], 1), jnp.int32))
    ri = lax.broadcasted_iota(jnp.int32, (CUM_CHUNK, CUM_CHUNK), 0)
    cj = lax.broadcasted_iota(jnp.int32, (CUM_CHUNK, CUM_CHUNK), 1)
    tri = jnp.where(ri < cj, 1.0, 0.0).astype(BF16)
    gt = jnp.where(bits > t, 1.0, 0.0)
    eq = jnp.where(bits == t, 1.0, 0.0)
    need = capf - gt.sum(axis=1, keepdims=True)
    sel = gt + jnp.where(_excl_cumsum_lanes(eq, tri) < need, eq, 0.0)
    pos = _excl_cumsum_lanes(sel, tri)
    slot_ref[0] = jnp.where(sel > 0.5, pos, -1.0).astype(jnp.int32)


def expert_topk(aff_t, cap):
    b, e, s = aff_t.shape
    return pl.pallas_call(
        functools.partial(_topk_body, cap=cap),
        grid=(b,),
        in_specs=[pl.BlockSpec((1, e, s), lambda i: (i, 0, 0))],
        out_specs=pl.BlockSpec((1, e, s), lambda i: (i, 0, 0)),
        out_shape=jax.ShapeDtypeStruct((b, e, s), jnp.int32),
        compiler_params=_params(("arbitrary",), 32),
        name="expert_topk",
    )(aff_t)


def _gather_body(slot_ref, aff_ref, h_ref, xg_ref, val_ref, *, cap):
    e = pl.program_id(1)
    srow = slot_ref[0, pl.ds(e, 1), :]
    arow = aff_ref[0, pl.ds(e, 1), :]
    ci = lax.broadcasted_iota(jnp.int32, (cap, srow.shape[1]), 0)
    hit = srow == ci
    onehot = jnp.where(hit, 1.0, 0.0).astype(BF16)
    xg_ref[0] = jnp.dot(onehot, h_ref[0], preferred_element_type=F32).astype(xg_ref.dtype)
    val_ref[0] = jnp.where(hit, arow, 0.0).sum(axis=1, keepdims=True)


def expert_gather(slot, aff_t, h3, cap):
    b, e, s = slot.shape
    d = h3.shape[-1]
    return pl.pallas_call(
        functools.partial(_gather_body, cap=cap),
        grid=(b, e),
        in_specs=[pl.BlockSpec((1, e, s), lambda bi, ei: (bi, 0, 0)),
                  pl.BlockSpec((1, e, s), lambda bi, ei: (bi, 0, 0)),
                  pl.BlockSpec((1, s, d), lambda bi, ei: (bi, 0, 0))],
        out_specs=[pl.BlockSpec((1, cap, d), lambda bi, ei: (ei, bi, 0)),
                   pl.BlockSpec((1, cap, 1), lambda bi, ei: (ei, bi, 0))],
        out_shape=[jax.ShapeDtypeStruct((e, b * cap, d), BF16),
                   jax.ShapeDtypeStruct((e, b * cap, 1), F32)],
        compiler_params=_params(("arbitrary", "arbitrary"), 56),
        name="expert_gather",
    )(slot, aff_t, h3)


def _expert_up_body(x_ref, wg_ref, wu_ref, o_ref):
    x = x_ref[0]
    a = jnp.dot(x, wg_ref[0].astype(BF16), preferred_element_type=F32)
    u = jnp.dot(x, wu_ref[0].astype(BF16), preferred_element_type=F32)
    o_ref[0] = (jax.nn.silu(a) * u).astype(o_ref.dtype)


def expert_up(xg, w_gate, w_up, tf=256):
    e, rows, d = xg.shape
    f = w_gate.shape[-1]
    return pl.pallas_call(
        _expert_up_body,
        grid=(e, f // tf),
        in_specs=[pl.BlockSpec((1, rows, d), lambda ei, fi: (ei, 0, 0)),
                  pl.BlockSpec((1, d, tf), lambda ei, fi: (ei, 0, fi)),
                  pl.BlockSpec((1, d, tf), lambda ei, fi: (ei, 0, fi))],
        out_specs=pl.BlockSpec((1, rows, tf), lambda ei, fi: (ei, 0, fi)),
        out_shape=jax.ShapeDtypeStruct((e, rows, f), BF16),
        compiler_params=_params(("arbitrary", "arbitrary"), 56),
        name="expert_up",
    )(xg, w_gate, w_up)


def _expert_down_body(h_ref, wd_ref, val_ref, o_ref):
    y = jnp.dot(h_ref[0], wd_ref[0].astype(BF16), preferred_element_type=F32)
    o_ref[0] = (y * val_ref[0]).astype(o_ref.dtype)


def expert_down(hmid, w_down, valc, tn=512):
    e, rows, f = hmid.shape
    d = w_down.shape[-1]
    return pl.pallas_call(
        _expert_down_body,
        grid=(e, d // tn),
        in_specs=[pl.BlockSpec((1, rows, f), lambda ei, ni: (ei, 0, 0)),
                  pl.BlockSpec((1, f, tn), lambda ei, ni: (ei, 0, ni)),
                  pl.BlockSpec((1, rows, 1), lambda ei, ni: (ei, 0, 0))],
        out_specs=pl.BlockSpec((1, rows, tn), lambda ei, ni: (ei, 0, ni)),
        out_shape=jax.ShapeDtypeStruct((e, rows, d), BF16),
        compiler_params=_params(("arbitrary", "arbitrary"), 48),
        name="expert_down",
    )(hmid, w_down, valc)


def _combine_body(slot_ref, y_ref, x_ref, g_ref, o_ref, acc_ref, *, cap):
    e = pl.program_id(2)

    @pl.when(e == 0)
    def _():
        acc_ref[...] = x_ref[...]

    srow = slot_ref[0, pl.ds(e, 1), :]
    ci = lax.broadcasted_iota(jnp.int32, (cap, srow.shape[1]), 0)
    onehot = jnp.where(srow == ci, 1.0, 0.0).astype(BF16)
    acc_ref[...] += lax.dot_general(onehot, y_ref[0], _TN, preferred_element_type=F32)

    @pl.when(e == pl.num_programs(2) - 1)
    def _():
        o_ref[...] = _rmsnorm_rows(acc_ref[...], g_ref[...]).astype(o_ref.dtype)


def expert_combine(slot, y, x2d, g, cap, ts=512):
    b, e, s = slot.shape
    d = x2d.shape[-1]
    nt = s // ts
    return pl.pallas_call(
        functools.partial(_combine_body, cap=cap),
        grid=(b, nt, e),
        in_specs=[pl.BlockSpec((1, e, ts), lambda bi, si, ei: (bi, 0, si)),
                  pl.BlockSpec((1, cap, d), lambda bi, si, ei: (ei, bi, 0)),
                  pl.BlockSpec((ts, d), lambda bi, si, ei: (bi * nt + si, 0)),
                  pl.BlockSpec((1, d), lambda bi, si, ei: (0, 0))],
        out_specs=pl.BlockSpec((ts, d), lambda bi, si, ei: (bi * nt + si, 0)),
        out_shape=jax.ShapeDtypeStruct(x2d.shape, x2d.dtype),
        scratch_shapes=[pltpu.VMEM((ts, d), F32)],
        compiler_params=_params(("arbitrary", "arbitrary", "arbitrary"), 56),
        name="expert_combine",
    )(slot, y, x2d, g.reshape(1, d))


def _rotary_tables(seq):
    half = ROT_DIM // 2
    inv = ROPE_THETA ** (-jnp.arange(half, dtype=F32) * 2.0 / ROT_DIM)
    ang = jnp.arange(seq).astype(F32)[:, None] * inv[None, :]
    cos, sin = jnp.cos(ang), jnp.sin(ang)
    ones = jnp.ones((seq, HEAD_DIM - ROT_DIM), F32)
    zeros = jnp.zeros((seq, HEAD_DIM - ROT_DIM), F32)
    zh = jnp.zeros((seq, half), F32)
    c = jnp.concatenate([cos, cos, ones], axis=1)
    s1 = jnp.concatenate([-sin, zh, zeros], axis=1)
    s2 = jnp.concatenate([zh, sin, zeros], axis=1)
    return c, s1, s2


def kernel(x, mem, norm_mix, w_in, b_gate, sink, rpb, w_branch_a, w_branch_b, w_out,
           norm_cross, norm_mem, wq_x, wk_x, wv_x, wo_x, norm_ffn, w_router,
           w_gate, w_up, w_down, norm_final):
    batch, seq, d = x.shape
    mem_len = mem.shape[1]
    m = batch * seq
    assert norm_mix.shape[0] == 1, "final RMSNorm is fused into the single layer's last kernel"
    cap = EC_CAPACITY * seq // N_EXPERTS
    bm, bn = 1024, 512
    sb = seq // bm
    x0 = x.reshape(m, d)

    h = rmsnorm(x0, norm_mix[0], BF16)
    rot = _rotary_tables(seq)
    rot_specs = [pl.BlockSpec((bm, HEAD_DIM), lambda j, i: (i % sb, 0))] * 3
    qk = matmul(h, w_in[0], col_off=0, n_cols=QA_W + KVA_W, bm=bm, bn=bn, out_dtype=BF16,
                epilogue=_ep_rotary, extras=rot, extra_specs=rot_specs, name="in_proj_rotary")
    vqkv = matmul(h, w_in[0], col_off=QA_W + KVA_W, n_cols=KVA_W + 3 * QB_W, bm=bm, bn=bn,
                  out_dtype=BF16, name="in_proj_plain")
    g_off = QA_W + 2 * KVA_W + 3 * QB_W
    gates = matmul(h, w_in[0], col_off=g_off, n_cols=2 * d, bm=bm, bn=bn, out_dtype=BF16,
                   epilogue=_ep_sigmoid, extras=(b_gate[0].reshape(1, 2 * d),),
                   extra_specs=[pl.BlockSpec((1, bn), lambda j, i: (0, j))], name="in_proj_gates")
    oa = window_attention(qk, vqkv, sink[0], batch, seq)
    ob = neighbourhood_attention(vqkv, _nbr_bias_table(rpb[0], seq), batch, seq)
    merged = branch_merge(oa, ob, w_branch_a[0], w_branch_b[0], gates)
    res_spec = [pl.BlockSpec((bm, bn), lambda j, i: (i, j))]
    x1 = matmul(merged, w_out[0], col_off=0, n_cols=d, bm=bm, bn=bn, out_dtype=F32,
                epilogue=_ep_residual, extras=(x0,), extra_specs=res_spec, name="out_proj")

    h2 = rmsnorm(x1, norm_cross[0], BF16)
    mn = rmsnorm(mem.reshape(batch * mem_len, d), norm_mem[0], BF16)
    qx = matmul(h2, wq_x[0], col_off=0, n_cols=X_W, bm=bm, bn=bn, out_dtype=BF16, name="xattn_q")
    kx = matmul(mn, wk_x[0], col_off=0, n_cols=X_W, bm=bm, bn=bn, out_dtype=BF16, name="xattn_k")
    vx = matmul(mn, wv_x[0], col_off=0, n_cols=X_W, bm=bm, bn=bn, out_dtype=BF16, name="xattn_v")
    ox = cross_attention(qx, kx, vx, batch, seq, mem_len)
    x2 = matmul(ox, wo_x[0], col_off=0, n_cols=d, bm=bm, bn=bn, out_dtype=F32,
                epilogue=_ep_residual, extras=(x1,), extra_specs=res_spec, name="xattn_out")

    h3, aff_t = router_norm(x2, norm_ffn[0], w_router[0], batch, seq)
    slot = expert_topk(aff_t, cap)
    xg, valc = expert_gather(slot, aff_t, h3.reshape(batch, seq, d), cap)
    hmid = expert_up(xg, w_gate[0], w_up[0])
    y = expert_down(hmid, w_down[0], valc)
    out = expert_combine(slot, y, x2, norm_final, cap)
    return out.reshape(batch, seq, d)
```

```python
import functools

import jax
import jax.numpy as jnp
from jax import lax
from jax.experimental import pallas as pl
from jax.experimental.pallas import tpu as pltpu

F32 = jnp.float32
BF16 = jnp.bfloat16

HEAD_DIM = 128
A_HEADS = 16
A_KV_HEADS = 4
A_GROUP = A_HEADS // A_KV_HEADS
WINDOW = 128
A_BLOCK = 128
ROT_DIM = HEAD_DIM // 4
ROPE_THETA = 500000.0
B_HEADS = 16
GRID_W = 64
NA_KH_MAX = 8
NA_KW = 16
X_HEADS = 4
N_EXPERTS = 16
EC_CAPACITY = 2
EPS = 1e-6
NEG = -1e30
LANES = 128
MIB = 1024 * 1024

QA_W = A_HEADS * HEAD_DIM
KVA_W = A_KV_HEADS * HEAD_DIM
QB_W = B_HEADS * HEAD_DIM
X_W = X_HEADS * HEAD_DIM

_NT = (((1,), (1,)), ((), ()))
_TN = (((0,), (0,)), ((), ()))


def _params(semantics, vmem_mib):
    return pltpu.CompilerParams(dimension_semantics=semantics,
                                vmem_limit_bytes=vmem_mib * MIB)


def _cast_rows(src_ref, dst_ref, rows, chunk=256):
    def body(k, carry):
        r = pl.multiple_of(k * chunk, chunk)
        dst_ref[pl.ds(r, chunk), :] = src_ref[pl.ds(r, chunk), :].astype(dst_ref.dtype)
        return carry
    lax.fori_loop(0, rows // chunk, body, 0)


def _rmsnorm_rows(x, g):
    ms = jnp.mean(x * x, axis=-1, keepdims=True)
    return x * lax.rsqrt(ms + EPS) * g


def _rmsnorm_body(x_ref, g_ref, o_ref):
    o_ref[...] = _rmsnorm_rows(x_ref[...], g_ref[...]).astype(o_ref.dtype)


def rmsnorm(x2d, g, out_dtype, bm=256):
    m, d = x2d.shape
    return pl.pallas_call(
        _rmsnorm_body,
        grid=(m // bm,),
        in_specs=[pl.BlockSpec((bm, d), lambda i: (i, 0)),
                  pl.BlockSpec((1, d), lambda i: (0, 0))],
        out_specs=pl.BlockSpec((bm, d), lambda i: (i, 0)),
        out_shape=jax.ShapeDtypeStruct((m, d), out_dtype),
        compiler_params=_params(("arbitrary",), 40),
        name="rmsnorm",
    )(x2d, g.reshape(1, d))


def _ep_store(acc, o_ref):
    o_ref[...] = acc.astype(o_ref.dtype)


def _ep_residual(acc, o_ref, r_ref):
    o_ref[...] = (r_ref[...] + acc).astype(o_ref.dtype)


def _ep_sigmoid(acc, o_ref, b_ref):
    o_ref[...] = jax.nn.sigmoid(acc + b_ref[...]).astype(o_ref.dtype)


def _ep_rotary(acc, o_ref, c_ref, s1_ref, s2_ref):
    c, s1, s2 = c_ref[...], s1_ref[...], s2_ref[...]
    half = ROT_DIM // 2
    for h in range(acc.shape[1] // HEAD_DIM):
        a = acc[:, h * HEAD_DIM:(h + 1) * HEAD_DIM]
        r = a * c + pltpu.roll(a, HEAD_DIM - half, 1) * s1 + pltpu.roll(a, half, 1) * s2
        o_ref[:, h * HEAD_DIM:(h + 1) * HEAD_DIM] = r.astype(o_ref.dtype)


def _mm_body(*refs, n_extra, epilogue, k_rows):
    a_ref, w_ref = refs[0], refs[1]
    extra = refs[2:2 + n_extra]
    o_ref = refs[2 + n_extra]
    wb_ref = refs[3 + n_extra]

    @pl.when(pl.program_id(1) == 0)
    def _():
        _cast_rows(w_ref, wb_ref, k_rows)

    acc = jnp.dot(a_ref[...], wb_ref[...], preferred_element_type=F32)
    epilogue(acc, o_ref, *extra)


def matmul(a, w, *, col_off, n_cols, bm, bn, out_dtype, epilogue=_ep_store,
           extras=(), extra_specs=(), vmem_mib=56, name="matmul"):
    m, k = a.shape
    off = col_off // bn
    assert col_off % bn == 0 and n_cols % bn == 0 and m % bm == 0
    body = functools.partial(_mm_body, n_extra=len(extras), epilogue=epilogue, k_rows=k)
    return pl.pallas_call(
        body,
        grid=(n_cols // bn, m // bm),
        in_specs=[pl.BlockSpec((bm, k), lambda j, i: (i, 0)),
                  pl.BlockSpec((k, bn), lambda j, i: (0, j + off))] + list(extra_specs),
        out_specs=pl.BlockSpec((bm, bn), lambda j, i: (i, j)),
        out_shape=jax.ShapeDtypeStruct((m, n_cols), out_dtype),
        scratch_shapes=[pltpu.VMEM((k, bn), BF16)],
        compiler_params=_params(("arbitrary", "arbitrary"), vmem_mib),
        name=name,
    )(a, w, *extras)


def _softmax_parts(parts, extra_col=None):
    m = parts[0].max(axis=1, keepdims=True)
    for p in parts[1:]:
        m = jnp.maximum(m, p.max(axis=1, keepdims=True))
    if extra_col is not None:
        m = jnp.maximum(m, extra_col)
    es = [jnp.exp(p - m) for p in parts]
    den = es[0].sum(axis=1, keepdims=True)
    for e in es[1:]:
        den = den + e.sum(axis=1, keepdims=True)
    if extra_col is not None:
        den = den + jnp.exp(extra_col - m)
    inv = 1.0 / den
    return [e * inv for e in es]


WIN_UNROLL = 2


def _win_body(sink_ref, q_ref, k_ref, v_ref, o_ref, *, seq):
    kv = pl.program_id(1)
    nb = seq // A_BLOCK
    scale = HEAD_DIM ** -0.5
    rows = A_GROUP * A_BLOCK
    qi = lax.broadcasted_iota(jnp.int32, (rows, A_BLOCK), 0) % A_BLOCK
    ci = lax.broadcasted_iota(jnp.int32, (rows, A_BLOCK), 1)
    sink_b = jnp.concatenate(
        [jnp.full((A_BLOCK, HEAD_DIM), sink_ref[kv * A_GROUP + g], F32) for g in range(A_GROUP)], axis=0)

    def scores(n):
        r0 = pl.multiple_of(n * A_BLOCK, A_BLOCK)
        rp = pl.multiple_of(jnp.maximum(n - 1, 0) * A_BLOCK, A_BLOCK)
        rn = pl.multiple_of(jnp.minimum(n + 1, nb - 1) * A_BLOCK, A_BLOCK)
        off_p = jnp.where(n > 0, 0, 2 * A_BLOCK)
        off_n = jnp.where(n < nb - 1, 0, 2 * A_BLOCK)
        q = jnp.concatenate(
            [q_ref[pl.ds(r0, A_BLOCK), g * HEAD_DIM:(g + 1) * HEAD_DIM] for g in range(A_GROUP)],
            axis=0)
        sp = lax.dot_general(q, k_ref[pl.ds(rp, A_BLOCK), :], _NT, preferred_element_type=F32) * scale
        sc = lax.dot_general(q, k_ref[pl.ds(r0, A_BLOCK), :], _NT, preferred_element_type=F32) * scale
        sn = lax.dot_general(q, k_ref[pl.ds(rn, A_BLOCK), :], _NT, preferred_element_type=F32) * scale
        sp = jnp.where(ci >= qi + off_p, sp, NEG)
        sn = jnp.where(ci <= qi - off_n, sn, NEG)
        return (rp, r0, rn), (sp, sc, sn)

    def exps(parts):
        m = jnp.maximum(jnp.maximum(parts[0], parts[1]), parts[2]).max(axis=1, keepdims=True)
        m = jnp.maximum(jnp.broadcast_to(m, sink_b.shape), sink_b)
        es = [jnp.exp(p - m) for p in parts]
        den = (es[0] + es[1] + es[2]).sum(axis=1, keepdims=True)
        den = jnp.broadcast_to(den, sink_b.shape) + jnp.exp(sink_b - m)
        return [e.astype(BF16) for e in es], 1.0 / den

    def body(it, carry):
        blocks = [scores(it * WIN_UNROLL + u) for u in range(WIN_UNROLL)]
        probs = [exps(parts) for _, parts in blocks]
        for (rows_kv, _), (es, inv) in zip(blocks, probs):
            o = jnp.dot(es[0], v_ref[pl.ds(rows_kv[0], A_BLOCK), :], preferred_element_type=F32)
            o = o + jnp.dot(es[1], v_ref[pl.ds(rows_kv[1], A_BLOCK), :], preferred_element_type=F32)
            o = o + jnp.dot(es[2], v_ref[pl.ds(rows_kv[2], A_BLOCK), :], preferred_element_type=F32)
            o = o * inv
            for g in range(A_GROUP):
                o_ref[pl.ds(rows_kv[1], A_BLOCK), g * HEAD_DIM:(g + 1) * HEAD_DIM] = (
                    o[g * A_BLOCK:(g + 1) * A_BLOCK].astype(o_ref.dtype))
        return carry

    lax.fori_loop(0, nb // WIN_UNROLL, body, 0)


def window_attention(qk, vqkv, sink, batch, seq):
    gw = A_GROUP * HEAD_DIM
    k_blk0 = QA_W // HEAD_DIM
    return pl.pallas_call(
        functools.partial(_win_body, seq=seq),
        grid=(batch, A_KV_HEADS),
        in_specs=[pl.BlockSpec(memory_space=pltpu.SMEM),
                  pl.BlockSpec((seq, gw), lambda b, h: (b, h)),
                  pl.BlockSpec((seq, HEAD_DIM), lambda b, h: (b, k_blk0 + h)),
                  pl.BlockSpec((seq, HEAD_DIM), lambda b, h: (b, h))],
        out_specs=pl.BlockSpec((seq, gw), lambda b, h: (b, h)),
        out_shape=jax.ShapeDtypeStruct((batch * seq, QA_W), BF16),
        compiler_params=_params(("arbitrary", "arbitrary"), 32),
        name="window_attention",
    )(sink, qk, qk, vqkv)


NBR_HG = 4
NBR_ROWS = 2


def _nbr_body(q_ref, k_ref, v_ref, bias_ref, o_ref, *, seq):
    rows = seq // GRID_W
    kh = min(NA_KH_MAX, rows)
    strip = kh * GRID_W
    scale = HEAD_DIM ** -0.5

    def body(it, carry):
        units = []
        for rr in range(NBR_ROWS):
            r = it * NBR_ROWS + rr
            rs = jnp.clip(r - kh // 2, 0, rows - kh)
            q0 = pl.multiple_of(r * GRID_W, GRID_W)
            k0 = pl.multiple_of(rs * GRID_W, GRID_W)
            for h in range(NBR_HG):
                units.append((q0, k0, r - rs, h, slice(h * HEAD_DIM, (h + 1) * HEAD_DIM)))
        ss = [lax.dot_general(q_ref[pl.ds(q0, GRID_W), cols], k_ref[pl.ds(k0, strip), cols], _NT,
                              preferred_element_type=F32) * scale + bias_ref[h, var]
              for q0, k0, var, h, cols in units]
        ps = []
        for s in ss:
            e = jnp.exp(s - s.max(axis=1, keepdims=True))
            ps.append((e.astype(BF16), 1.0 / e.sum(axis=1, keepdims=True)))
        for (q0, k0, var, h, cols), (e, inv) in zip(units, ps):
            o = jnp.dot(e, v_ref[pl.ds(k0, strip), cols], preferred_element_type=F32) * inv
            o_ref[pl.ds(q0, GRID_W), cols] = o.astype(o_ref.dtype)
        return carry

    lax.fori_loop(0, rows // NBR_ROWS, body, 0)


def neighbourhood_attention(vqkv, bias_tbl, batch, seq):
    gw = NBR_HG * HEAD_DIM
    q0, k0, v0 = KVA_W // gw, (KVA_W + QB_W) // gw, (KVA_W + 2 * QB_W) // gw
    kh = bias_tbl.shape[1]
    return pl.pallas_call(
        functools.partial(_nbr_body, seq=seq),
        grid=(B_HEADS // NBR_HG, batch),
        in_specs=[pl.BlockSpec((seq, gw), lambda g, b: (b, q0 + g)),
                  pl.BlockSpec((seq, gw), lambda g, b: (b, k0 + g)),
                  pl.BlockSpec((seq, gw), lambda g, b: (b, v0 + g)),
                  pl.BlockSpec((NBR_HG, kh, GRID_W, kh * GRID_W), lambda g, b: (g, 0, 0, 0))],
        out_specs=pl.BlockSpec((seq, gw), lambda g, b: (b, g)),
        out_shape=jax.ShapeDtypeStruct((batch * seq, QB_W), BF16),
        compiler_params=_params(("arbitrary", "arbitrary"), 40),
        name="neighbourhood_attention",
    )(vqkv, vqkv, vqkv, bias_tbl)


def _bias_table_body(rpb_ref, o_ref, *, kh):
    h = pl.program_id(0)
    n_dr, n_dc = 2 * NA_KH_MAX - 1, 2 * NA_KW - 1
    c = lax.broadcasted_iota(jnp.int32, (GRID_W, LANES), 0)
    lane = lax.broadcasted_iota(jnp.int32, (GRID_W, LANES), 1)
    kc = lane % GRID_W
    diff = jnp.clip(kc - c + NA_KW - 1, 0, n_dc - 1)
    cs = jnp.clip(c - NA_KW // 2, 0, GRID_W - NA_KW)
    col_ok = (kc >= cs) & (kc < cs + NA_KW)
    slabs = []
    for dr in range(n_dr):
        acc = jnp.zeros((GRID_W, LANES), F32)
        for d in range(n_dc):
            acc = jnp.where(diff == d, rpb_ref[(h * n_dr + dr) * n_dc + d], acc)
        slabs.append(jnp.where(col_ok, acc, NEG))
    left = lane < GRID_W
    for var in range(kh):
        for jp in range(kh * GRID_W // LANES):
            dr0 = 2 * jp - var + NA_KH_MAX - 1
            o_ref[0, var, :, jp * LANES:(jp + 1) * LANES] = jnp.where(left, slabs[dr0], slabs[dr0 + 1])


def _nbr_bias_table(rpb, seq):
    rows = seq // GRID_W
    kh = min(NA_KH_MAX, rows)
    heads = rpb.shape[0]
    assert kh == NA_KH_MAX and 2 * GRID_W == LANES
    return pl.pallas_call(
        functools.partial(_bias_table_body, kh=kh),
        grid=(heads,),
        in_specs=[pl.BlockSpec(memory_space=pltpu.SMEM)],
        out_specs=pl.BlockSpec((1, kh, GRID_W, kh * GRID_W), lambda h: (h, 0, 0, 0)),
        out_shape=jax.ShapeDtypeStruct((heads, kh, GRID_W, kh * GRID_W), F32),
        compiler_params=_params(("arbitrary",), 16),
        name="nbr_bias_table",
    )(rpb.astype(F32).reshape(-1))


def _merge_body(oa_ref, ob_ref, wa_ref, wb_ref, g0_ref, g1_ref, o_ref, wa_s, wb_s, *, k_rows):
    @pl.when(pl.program_id(1) == 0)
    def _():
        _cast_rows(wa_ref, wa_s, k_rows)
        _cast_rows(wb_ref, wb_s, k_rows)

    ya = jnp.dot(oa_ref[...], wa_s[...], preferred_element_type=F32)
    yb = jnp.dot(ob_ref[...], wb_s[...], preferred_element_type=F32)
    o_ref[...] = (g0_ref[...].astype(F32) * ya + g1_ref[...].astype(F32) * yb).astype(o_ref.dtype)


def branch_merge(oa, ob, wa, wb, gates, bm=1024, bn=512):
    m, k = oa.shape
    n = wa.shape[1]
    g1_off = n // bn
    return pl.pallas_call(
        functools.partial(_merge_body, k_rows=k),
        grid=(n // bn, m // bm),
        in_specs=[pl.BlockSpec((bm, k), lambda j, i: (i, 0)),
                  pl.BlockSpec((bm, k), lambda j, i: (i, 0)),
                  pl.BlockSpec((k, bn), lambda j, i: (0, j)),
                  pl.BlockSpec((k, bn), lambda j, i: (0, j)),
                  pl.BlockSpec((bm, bn), lambda j, i: (i, j)),
                  pl.BlockSpec((bm, bn), lambda j, i: (i, j + g1_off))],
        out_specs=pl.BlockSpec((bm, bn), lambda j, i: (i, j)),
        out_shape=jax.ShapeDtypeStruct((m, n), BF16),
        scratch_shapes=[pltpu.VMEM((k, bn), BF16), pltpu.VMEM((k, bn), BF16)],
        compiler_params=_params(("arbitrary", "arbitrary"), 56),
        name="branch_merge",
    )(oa, ob, wa, wb, gates, gates)


def _xattn_body(q_ref, k_ref, v_ref, o_ref):
    scale = HEAD_DIM ** -0.5
    for h in range(X_HEADS):
        cols = slice(h * HEAD_DIM, (h + 1) * HEAD_DIM)
        s = lax.dot_general(q_ref[:, cols], k_ref[:, cols], _NT, preferred_element_type=F32) * scale
        (p,) = _softmax_parts([s])
        o = jnp.dot(p.astype(BF16), v_ref[:, cols], preferred_element_type=F32)
        o_ref[:, cols] = o.astype(o_ref.dtype)


def cross_attention(q, k, v, batch, seq, mem_len, tq=512):
    nt = seq // tq
    return pl.pallas_call(
        _xattn_body,
        grid=(batch, nt),
        in_specs=[pl.BlockSpec((tq, X_W), lambda b, i: (b * nt + i, 0)),
                  pl.BlockSpec((mem_len, X_W), lambda b, i: (b, 0)),
                  pl.BlockSpec((mem_len, X_W), lambda b, i: (b, 0))],
        out_specs=pl.BlockSpec((tq, X_W), lambda b, i: (b * nt + i, 0)),
        out_shape=jax.ShapeDtypeStruct((batch * seq, X_W), BF16),
        compiler_params=_params(("arbitrary", "arbitrary"), 32),
        name="cross_attention",
    )(q, k, v)


def _router_body(x_ref, g_ref, wr_ref, h_ref, aff_ref, atm_ref):
    hn = _rmsnorm_rows(x_ref[...], g_ref[...])
    h_ref[...] = hn.astype(h_ref.dtype)
    logits = jnp.dot(hn, wr_ref[...], precision=lax.Precision.HIGHEST,
                     preferred_element_type=F32)
    lane = lax.broadcasted_iota(jnp.int32, logits.shape, 1)
    logits = jnp.where(lane < N_EXPERTS, logits, NEG)
    (aff,) = _softmax_parts([logits])
    atm_ref[...] = aff
    aff_ref[0] = aff.T[:N_EXPERTS, :]


def router_norm(x2d, g, w_router, batch, seq, bm=256):
    m, d = x2d.shape
    nt = seq // bm
    wr = jnp.pad(w_router, ((0, 0), (0, LANES - N_EXPERTS)))
    return pl.pallas_call(
        _router_body,
        grid=(m // bm,),
        in_specs=[pl.BlockSpec((bm, d), lambda i: (i, 0)),
                  pl.BlockSpec((1, d), lambda i: (0, 0)),
                  pl.BlockSpec((d, LANES), lambda i: (0, 0))],
        out_specs=[pl.BlockSpec((bm, d), lambda i: (i, 0)),
                   pl.BlockSpec((1, N_EXPERTS, bm), lambda i: (i // nt, 0, i % nt)),
                   pl.BlockSpec((bm, LANES), lambda i: (i, 0))],
        out_shape=[jax.ShapeDtypeStruct((m, d), BF16),
                   jax.ShapeDtypeStruct((batch, N_EXPERTS, seq), F32),
                   jax.ShapeDtypeStruct((m, LANES), F32)],
        compiler_params=_params(("arbitrary",), 40),
        name="router_norm",
    )(x2d, g.reshape(1, d), wr)


CUM_CHUNK = 256


def _excl_cumsum_lanes(x01, tri):
    n = x01.shape[1]
    carry = jnp.zeros((x01.shape[0], 1), F32)
    out = []
    for c in range(n // CUM_CHUNK):
        xc = x01[:, c * CUM_CHUNK:(c + 1) * CUM_CHUNK]
        out.append(jnp.dot(xc.astype(BF16), tri, preferred_element_type=F32) + carry)
        carry = carry + xc.sum(axis=1, keepdims=True)
    return jnp.concatenate(out, axis=1)


RANK_CHUNK = 256
SUBLANES = 8


def _topk_body(aff_ref, atm_ref, slot_ref, cgt_ref, *, cap):
    n_exp, s = aff_ref.shape[1], aff_ref.shape[2]
    capf = jnp.float32(cap)
    for e in range(n_exp):
        a_row = aff_ref[0, e:e + 1, :]

        def chunk(c, acc, a_row=a_row, e=e):
            r = pl.multiple_of(c * RANK_CHUNK, RANK_CHUNK)
            col = atm_ref[pl.ds(r, RANK_CHUNK), e:e + 1]
            above = jnp.where(col > a_row, 1.0, 0.0)
            return acc + above.reshape(RANK_CHUNK // SUBLANES, SUBLANES, s).sum(axis=0)

        acc = lax.fori_loop(0, s // RANK_CHUNK, chunk, jnp.zeros((SUBLANES, s), F32))
        cgt_ref[e:e + 1, :] = acc.sum(axis=0, keepdims=True)
    cgt = cgt_ref[...]
    ri = lax.broadcasted_iota(jnp.int32, (CUM_CHUNK, CUM_CHUNK), 0)
    cj = lax.broadcasted_iota(jnp.int32, (CUM_CHUNK, CUM_CHUNK), 1)
    tri = jnp.where(ri < cj, 1.0, 0.0).astype(BF16)
    cand = jnp.where(cgt < capf, 1.0, 0.0)
    extra = cand.sum(axis=1, keepdims=True) - capf
    g_last = jnp.where(cand > 0.5, cgt, -1.0).max(axis=1, keepdims=True)
    tie = jnp.where(cgt == g_last, cand, 0.0)
    keep = tie.sum(axis=1, keepdims=True) - extra
    sel = cand - jnp.where(_excl_cumsum_lanes(tie, tri) >= keep, tie, 0.0)
    pos = _excl_cumsum_lanes(sel, tri)
    slot_ref[0] = jnp.where(sel > 0.5, pos, -1.0).astype(jnp.int32)


def expert_topk(aff_t, aff_tm, cap):
    b, e, s = aff_t.shape
    return pl.pallas_call(
        functools.partial(_topk_body, cap=cap),
        grid=(b,),
        in_specs=[pl.BlockSpec((1, e, s), lambda i: (i, 0, 0)),
                  pl.BlockSpec((s, LANES), lambda i: (i, 0))],
        out_specs=pl.BlockSpec((1, e, s), lambda i: (i, 0, 0)),
        out_shape=jax.ShapeDtypeStruct((b, e, s), jnp.int32),
        scratch_shapes=[pltpu.VMEM((e, s), F32)],
        compiler_params=_params(("arbitrary",), 32),
        name="expert_topk",
    )(aff_t, aff_tm)


def _gather_body(slot_ref, aff_ref, h_ref, xg_ref, val_ref, *, cap):
    e = pl.program_id(1)
    srow = slot_ref[0, pl.ds(e, 1), :]
    arow = aff_ref[0, pl.ds(e, 1), :]
    ci = lax.broadcasted_iota(jnp.int32, (cap, srow.shape[1]), 0)
    hit = srow == ci
    onehot = jnp.where(hit, 1.0, 0.0).astype(BF16)
    xg_ref[0] = jnp.dot(onehot, h_ref[0], preferred_element_type=F32).astype(xg_ref.dtype)
    val_ref[0] = jnp.where(hit, arow, 0.0).sum(axis=1, keepdims=True)


def expert_gather(slot, aff_t, h3, cap):
    b, e, s = slot.shape
    d = h3.shape[-1]
    return pl.pallas_call(
        functools.partial(_gather_body, cap=cap),
        grid=(b, e),
        in_specs=[pl.BlockSpec((1, e, s), lambda bi, ei: (bi, 0, 0)),
                  pl.BlockSpec((1, e, s), lambda bi, ei: (bi, 0, 0)),
                  pl.BlockSpec((1, s, d), lambda bi, ei: (bi, 0, 0))],
        out_specs=[pl.BlockSpec((1, cap, d), lambda bi, ei: (ei, bi, 0)),
                   pl.BlockSpec((1, cap, 1), lambda bi, ei: (ei, bi, 0))],
        out_shape=[jax.ShapeDtypeStruct((e, b * cap, d), BF16),
                   jax.ShapeDtypeStruct((e, b * cap, 1), F32)],
        compiler_params=_params(("arbitrary", "arbitrary"), 56),
        name="expert_gather",
    )(slot, aff_t, h3)


def _expert_up_body(x_ref, wg_ref, wu_ref, o_ref):
    x = x_ref[0]
    a = jnp.dot(x, wg_ref[0].astype(BF16), preferred_element_type=F32)
    u = jnp.dot(x, wu_ref[0].astype(BF16), preferred_element_type=F32)
    o_ref[0] = (jax.nn.silu(a) * u).astype(o_ref.dtype)


def expert_up(xg, w_gate, w_up, tf=256):
    e, rows, d = xg.shape
    f = w_gate.shape[-1]
    return pl.pallas_call(
        _expert_up_body,
        grid=(e, f // tf),
        in_specs=[pl.BlockSpec((1, rows, d), lambda ei, fi: (ei, 0, 0)),
                  pl.BlockSpec((1, d, tf), lambda ei, fi: (ei, 0, fi)),
                  pl.BlockSpec((1, d, tf), lambda ei, fi: (ei, 0, fi))],
        out_specs=pl.BlockSpec((1, rows, tf), lambda ei, fi: (ei, 0, fi)),
        out_shape=jax.ShapeDtypeStruct((e, rows, f), BF16),
        compiler_params=_params(("arbitrary", "arbitrary"), 56),
        name="expert_up",
    )(xg, w_gate, w_up)


def _expert_down_body(h_ref, wd_ref, val_ref, o_ref):
    y = jnp.dot(h_ref[0], wd_ref[0].astype(BF16), preferred_element_type=F32)
    o_ref[0] = (y * val_ref[0]).astype(o_ref.dtype)


def expert_down(hmid, w_down, valc, tn=512):
    e, rows, f = hmid.shape
    d = w_down.shape[-1]
    return pl.pallas_call(
        _expert_down_body,
        grid=(e, d // tn),
        in_specs=[pl.BlockSpec((1, rows, f), lambda ei, ni: (ei, 0, 0)),
                  pl.BlockSpec((1, f, tn), lambda ei, ni: (ei, 0, ni)),
                  pl.BlockSpec((1, rows, 1), lambda ei, ni: (ei, 0, 0))],
        out_specs=pl.BlockSpec((1, rows, tn), lambda ei, ni: (ei, 0, ni)),
        out_shape=jax.ShapeDtypeStruct((e, rows, d), BF16),
        compiler_params=_params(("arbitrary", "arbitrary"), 48),
        name="expert_down",
    )(hmid, w_down, valc)


def _combine_body(slot_ref, y_ref, x_ref, g_ref, o_ref, acc_ref, *, cap):
    e = pl.program_id(2)

    @pl.when(e == 0)
    def _():
        acc_ref[...] = x_ref[...]

    srow = slot_ref[0, pl.ds(e, 1), :]
    ci = lax.broadcasted_iota(jnp.int32, (cap, srow.shape[1]), 0)
    onehot = jnp.where(srow == ci, 1.0, 0.0).astype(BF16)
    acc_ref[...] += lax.dot_general(onehot, y_ref[0], _TN, preferred_element_type=F32)

    @pl.when(e == pl.num_programs(2) - 1)
    def _():
        o_ref[...] = _rmsnorm_rows(acc_ref[...], g_ref[...]).astype(o_ref.dtype)


def expert_combine(slot, y, x2d, g, cap, ts=512):
    b, e, s = slot.shape
    d = x2d.shape[-1]
    nt = s // ts
    return pl.pallas_call(
        functools.partial(_combine_body, cap=cap),
        grid=(b, nt, e),
        in_specs=[pl.BlockSpec((1, e, ts), lambda bi, si, ei: (bi, 0, si)),
                  pl.BlockSpec((1, cap, d), lambda bi, si, ei: (ei, bi, 0)),
                  pl.BlockSpec((ts, d), lambda bi, si, ei: (bi * nt + si, 0)),
                  pl.BlockSpec((1, d), lambda bi, si, ei: (0, 0))],
        out_specs=pl.BlockSpec((ts, d), lambda bi, si, ei: (bi * nt + si, 0)),
        out_shape=jax.ShapeDtypeStruct(x2d.shape, x2d.dtype),
        scratch_shapes=[pltpu.VMEM((ts, d), F32)],
        compiler_params=_params(("arbitrary", "arbitrary", "arbitrary"), 56),
        name="expert_combine",
    )(slot, y, x2d, g.reshape(1, d))


def _rotary_tables(seq):
    half = ROT_DIM // 2
    inv = ROPE_THETA ** (-jnp.arange(half, dtype=F32) * 2.0 / ROT_DIM)
    ang = jnp.arange(seq).astype(F32)[:, None] * inv[None, :]
    cos, sin = jnp.cos(ang), jnp.sin(ang)
    ones = jnp.ones((seq, HEAD_DIM - ROT_DIM), F32)
    zeros = jnp.zeros((seq, HEAD_DIM - ROT_DIM), F32)
    zh = jnp.zeros((seq, half), F32)
    c = jnp.concatenate([cos, cos, ones], axis=1)
    s1 = jnp.concatenate([-sin, zh, zeros], axis=1)
    s2 = jnp.concatenate([zh, sin, zeros], axis=1)
    return c, s1, s2


def kernel(x, mem, norm_mix, w_in, b_gate, sink, rpb, w_branch_a, w_branch_b, w_out,
           norm_cross, norm_mem, wq_x, wk_x, wv_x, wo_x, norm_ffn, w_router,
           w_gate, w_up, w_down, norm_final):
    batch, seq, d = x.shape
    mem_len = mem.shape[1]
    m = batch * seq
    assert norm_mix.shape[0] == 1, "final RMSNorm is fused into the single layer's last kernel"
    cap = EC_CAPACITY * seq // N_EXPERTS
    bm, bn = 1024, 512
    sb = seq // bm
    x0 = x.reshape(m, d)

    h = rmsnorm(x0, norm_mix[0], BF16)
    rot = _rotary_tables(seq)
    rot_specs = [pl.BlockSpec((bm, HEAD_DIM), lambda j, i: (i % sb, 0))] * 3
    qk = matmul(h, w_in[0], col_off=0, n_cols=QA_W + KVA_W, bm=bm, bn=bn, out_dtype=BF16,
                epilogue=_ep_rotary, extras=rot, extra_specs=rot_specs, name="in_proj_rotary")
    vqkv = matmul(h, w_in[0], col_off=QA_W + KVA_W, n_cols=KVA_W + 3 * QB_W, bm=bm, bn=bn,
                  out_dtype=BF16, name="in_proj_plain")
    g_off = QA_W + 2 * KVA_W + 3 * QB_W
    gates = matmul(h, w_in[0], col_off=g_off, n_cols=2 * d, bm=bm, bn=bn, out_dtype=BF16,
                   epilogue=_ep_sigmoid, extras=(b_gate[0].reshape(1, 2 * d),),
                   extra_specs=[pl.BlockSpec((1, bn), lambda j, i: (0, j))], name="in_proj_gates")
    oa = window_attention(qk, vqkv, sink[0], batch, seq)
    ob = neighbourhood_attention(vqkv, _nbr_bias_table(rpb[0], seq), batch, seq)
    merged = branch_merge(oa, ob, w_branch_a[0], w_branch_b[0], gates)
    res_spec = [pl.BlockSpec((bm, bn), lambda j, i: (i, j))]
    x1 = matmul(merged, w_out[0], col_off=0, n_cols=d, bm=bm, bn=bn, out_dtype=F32,
                epilogue=_ep_residual, extras=(x0,), extra_specs=res_spec, name="out_proj")

    h2 = rmsnorm(x1, norm_cross[0], BF16)
    mn = rmsnorm(mem.reshape(batch * mem_len, d), norm_mem[0], BF16)
    qx = matmul(h2, wq_x[0], col_off=0, n_cols=X_W, bm=bm, bn=bn, out_dtype=BF16, name="xattn_q")
    kx = matmul(mn, wk_x[0], col_off=0, n_cols=X_W, bm=bm, bn=bn, out_dtype=BF16, name="xattn_k")
    vx = matmul(mn, wv_x[0], col_off=0, n_cols=X_W, bm=bm, bn=bn, out_dtype=BF16, name="xattn_v")
    ox = cross_attention(qx, kx, vx, batch, seq, mem_len)
    x2 = matmul(ox, wo_x[0], col_off=0, n_cols=d, bm=bm, bn=bn, out_dtype=F32,
                epilogue=_ep_residual, extras=(x1,), extra_specs=res_spec, name="xattn_out")

    h3, aff_t, aff_tm = router_norm(x2, norm_ffn[0], w_router[0], batch, seq)
    slot = expert_topk(aff_t, aff_tm, cap)
    xg, valc = expert_gather(slot, aff_t, h3.reshape(batch, seq, d), cap)
    hmid = expert_up(xg, w_gate[0], w_up[0])
    y = expert_down(hmid, w_down[0], valc)
    out = expert_combine(slot, y, x2, norm_final, cap)
    return out.reshape(batch, seq, d)
```

```python
import functools

import jax
import jax.numpy as jnp
from jax import lax
from jax.experimental import pallas as pl
from jax.experimental.pallas import tpu as pltpu

F32 = jnp.float32
BF16 = jnp.bfloat16

HEAD_DIM = 128
A_HEADS = 16
A_KV_HEADS = 4
A_GROUP = A_HEADS // A_KV_HEADS
WINDOW = 128
A_BLOCK = 128
ROT_DIM = HEAD_DIM // 4
ROPE_THETA = 500000.0
B_HEADS = 16
GRID_W = 64
NA_KH_MAX = 8
NA_KW = 16
X_HEADS = 4
N_EXPERTS = 16
EC_CAPACITY = 2
EPS = 1e-6
NEG = -1e30
LANES = 128
MIB = 1024 * 1024

QA_W = A_HEADS * HEAD_DIM
KVA_W = A_KV_HEADS * HEAD_DIM
QB_W = B_HEADS * HEAD_DIM
X_W = X_HEADS * HEAD_DIM

_NT = (((1,), (1,)), ((), ()))
_TN = (((0,), (0,)), ((), ()))


def _params(semantics, vmem_mib):
    return pltpu.CompilerParams(dimension_semantics=semantics,
                                vmem_limit_bytes=vmem_mib * MIB)


def _cast_rows(src_ref, dst_ref, rows, chunk=256):
    def body(k, carry):
        r = pl.multiple_of(k * chunk, chunk)
        dst_ref[pl.ds(r, chunk), :] = src_ref[pl.ds(r, chunk), :].astype(dst_ref.dtype)
        return carry
    lax.fori_loop(0, rows // chunk, body, 0)


def _rmsnorm_rows(x, g):
    ms = jnp.mean(x * x, axis=-1, keepdims=True)
    return x * lax.rsqrt(ms + EPS) * g


def _rmsnorm_body(x_ref, g_ref, o_ref):
    o_ref[...] = _rmsnorm_rows(x_ref[...], g_ref[...]).astype(o_ref.dtype)


def rmsnorm(x2d, g, out_dtype, bm=256):
    m, d = x2d.shape
    return pl.pallas_call(
        _rmsnorm_body,
        grid=(m // bm,),
        in_specs=[pl.BlockSpec((bm, d), lambda i: (i, 0)),
                  pl.BlockSpec((1, d), lambda i: (0, 0))],
        out_specs=pl.BlockSpec((bm, d), lambda i: (i, 0)),
        out_shape=jax.ShapeDtypeStruct((m, d), out_dtype),
        compiler_params=_params(("arbitrary",), 40),
        name="rmsnorm",
    )(x2d, g.reshape(1, d))


def _ep_store(acc, o_ref):
    o_ref[...] = acc.astype(o_ref.dtype)


def _ep_residual(acc, o_ref, r_ref):
    o_ref[...] = (r_ref[...] + acc).astype(o_ref.dtype)


def _ep_sigmoid(acc, o_ref, b_ref):
    o_ref[...] = jax.nn.sigmoid(acc + b_ref[...]).astype(o_ref.dtype)


def _ep_rotary(acc, o_ref, c_ref, s1_ref, s2_ref):
    c, s1, s2 = c_ref[...], s1_ref[...], s2_ref[...]
    half = ROT_DIM // 2
    for h in range(acc.shape[1] // HEAD_DIM):
        a = acc[:, h * HEAD_DIM:(h + 1) * HEAD_DIM]
        r = a * c + pltpu.roll(a, HEAD_DIM - half, 1) * s1 + pltpu.roll(a, half, 1) * s2
        o_ref[:, h * HEAD_DIM:(h + 1) * HEAD_DIM] = r.astype(o_ref.dtype)


def _mm_body(*refs, n_extra, epilogue, k_rows):
    a_ref, w_ref = refs[0], refs[1]
    extra = refs[2:2 + n_extra]
    o_ref = refs[2 + n_extra]
    wb_ref = refs[3 + n_extra]

    @pl.when(pl.program_id(1) == 0)
    def _():
        _cast_rows(w_ref, wb_ref, k_rows)

    acc = jnp.dot(a_ref[...], wb_ref[...], preferred_element_type=F32)
    epilogue(acc, o_ref, *extra)


def matmul(a, w, *, col_off, n_cols, bm, bn, out_dtype, epilogue=_ep_store,
           extras=(), extra_specs=(), vmem_mib=56, name="matmul"):
    m, k = a.shape
    off = col_off // bn
    assert col_off % bn == 0 and n_cols % bn == 0 and m % bm == 0
    body = functools.partial(_mm_body, n_extra=len(extras), epilogue=epilogue, k_rows=k)
    return pl.pallas_call(
        body,
        grid=(n_cols // bn, m // bm),
        in_specs=[pl.BlockSpec((bm, k), lambda j, i: (i, 0)),
                  pl.BlockSpec((k, bn), lambda j, i: (0, j + off))] + list(extra_specs),
        out_specs=pl.BlockSpec((bm, bn), lambda j, i: (i, j)),
        out_shape=jax.ShapeDtypeStruct((m, n_cols), out_dtype),
        scratch_shapes=[pltpu.VMEM((k, bn), BF16)],
        compiler_params=_params(("arbitrary", "arbitrary"), vmem_mib),
        name=name,
    )(a, w, *extras)


def _softmax_parts(parts, extra_col=None):
    m = parts[0].max(axis=1, keepdims=True)
    for p in parts[1:]:
        m = jnp.maximum(m, p.max(axis=1, keepdims=True))
    if extra_col is not None:
        m = jnp.maximum(m, extra_col)
    es = [jnp.exp(p - m) for p in parts]
    den = es[0].sum(axis=1, keepdims=True)
    for e in es[1:]:
        den = den + e.sum(axis=1, keepdims=True)
    if extra_col is not None:
        den = den + jnp.exp(extra_col - m)
    inv = 1.0 / den
    return [e * inv for e in es]


WIN_UNROLL = 2


def _win_body(sink_ref, q_ref, k_ref, v_ref, o_ref, *, seq):
    kv = pl.program_id(1)
    nb = seq // A_BLOCK
    scale = HEAD_DIM ** -0.5
    rows = A_GROUP * A_BLOCK
    qi = lax.broadcasted_iota(jnp.int32, (rows, A_BLOCK), 0) % A_BLOCK
    ci = lax.broadcasted_iota(jnp.int32, (rows, A_BLOCK), 1)
    sink_b = jnp.concatenate(
        [jnp.full((A_BLOCK, HEAD_DIM), sink_ref[kv * A_GROUP + g], F32) for g in range(A_GROUP)], axis=0)

    def scores(n):
        r0 = pl.multiple_of(n * A_BLOCK, A_BLOCK)
        rp = pl.multiple_of(jnp.maximum(n - 1, 0) * A_BLOCK, A_BLOCK)
        rn = pl.multiple_of(jnp.minimum(n + 1, nb - 1) * A_BLOCK, A_BLOCK)
        off_p = jnp.where(n > 0, 0, 2 * A_BLOCK)
        off_n = jnp.where(n < nb - 1, 0, 2 * A_BLOCK)
        q = jnp.concatenate(
            [q_ref[pl.ds(r0, A_BLOCK), g * HEAD_DIM:(g + 1) * HEAD_DIM] for g in range(A_GROUP)],
            axis=0)
        sp = lax.dot_general(q, k_ref[pl.ds(rp, A_BLOCK), :], _NT, preferred_element_type=F32) * scale
        sc = lax.dot_general(q, k_ref[pl.ds(r0, A_BLOCK), :], _NT, preferred_element_type=F32) * scale
        sn = lax.dot_general(q, k_ref[pl.ds(rn, A_BLOCK), :], _NT, preferred_element_type=F32) * scale
        sp = jnp.where(ci >= qi + off_p, sp, NEG)
        sn = jnp.where(ci <= qi - off_n, sn, NEG)
        return (rp, r0, rn), (sp, sc, sn)

    def exps(parts):
        m = jnp.maximum(jnp.maximum(parts[0], parts[1]), parts[2]).max(axis=1, keepdims=True)
        m = jnp.maximum(jnp.broadcast_to(m, sink_b.shape), sink_b)
        es = [jnp.exp(p - m) for p in parts]
        den = (es[0] + es[1] + es[2]).sum(axis=1, keepdims=True)
        den = jnp.broadcast_to(den, sink_b.shape) + jnp.exp(sink_b - m)
        return [e.astype(BF16) for e in es], 1.0 / den

    def body(it, carry):
        blocks = [scores(it * WIN_UNROLL + u) for u in range(WIN_UNROLL)]
        probs = [exps(parts) for _, parts in blocks]
        for (rows_kv, _), (es, inv) in zip(blocks, probs):
            o = jnp.dot(es[0], v_ref[pl.ds(rows_kv[0], A_BLOCK), :], preferred_element_type=F32)
            o = o + jnp.dot(es[1], v_ref[pl.ds(rows_kv[1], A_BLOCK), :], preferred_element_type=F32)
            o = o + jnp.dot(es[2], v_ref[pl.ds(rows_kv[2], A_BLOCK), :], preferred_element_type=F32)
            o = o * inv
            for g in range(A_GROUP):
                o_ref[pl.ds(rows_kv[1], A_BLOCK), g * HEAD_DIM:(g + 1) * HEAD_DIM] = (
                    o[g * A_BLOCK:(g + 1) * A_BLOCK].astype(o_ref.dtype))
        return carry

    lax.fori_loop(0, nb // WIN_UNROLL, body, 0)


def window_attention(qk, vqkv, sink, batch, seq):
    gw = A_GROUP * HEAD_DIM
    k_blk0 = QA_W // HEAD_DIM
    return pl.pallas_call(
        functools.partial(_win_body, seq=seq),
        grid=(batch, A_KV_HEADS),
        in_specs=[pl.BlockSpec(memory_space=pltpu.SMEM),
                  pl.BlockSpec((seq, gw), lambda b, h: (b, h)),
                  pl.BlockSpec((seq, HEAD_DIM), lambda b, h: (b, k_blk0 + h)),
                  pl.BlockSpec((seq, HEAD_DIM), lambda b, h: (b, h))],
        out_specs=pl.BlockSpec((seq, gw), lambda b, h: (b, h)),
        out_shape=jax.ShapeDtypeStruct((batch * seq, QA_W), BF16),
        compiler_params=_params(("arbitrary", "arbitrary"), 32),
        name="window_attention",
    )(sink, qk, qk, vqkv)


NBR_HG = 4
NBR_ROWS = 2


def _nbr_body(q_ref, k_ref, v_ref, bias_ref, o_ref, *, seq):
    rows = seq // GRID_W
    kh = min(NA_KH_MAX, rows)
    strip = kh * GRID_W
    scale = HEAD_DIM ** -0.5

    def body(it, carry):
        units = []
        for rr in range(NBR_ROWS):
            r = it * NBR_ROWS + rr
            rs = jnp.clip(r - kh // 2, 0, rows - kh)
            q0 = pl.multiple_of(r * GRID_W, GRID_W)
            k0 = pl.multiple_of(rs * GRID_W, GRID_W)
            for h in range(NBR_HG):
                units.append((q0, k0, r - rs, h, slice(h * HEAD_DIM, (h + 1) * HEAD_DIM)))
        ss = [lax.dot_general(q_ref[pl.ds(q0, GRID_W), cols], k_ref[pl.ds(k0, strip), cols], _NT,
                              preferred_element_type=F32) * scale + bias_ref[h, var]
              for q0, k0, var, h, cols in units]
        ps = []
        for s in ss:
            e = jnp.exp(s - s.max(axis=1, keepdims=True))
            ps.append((e.astype(BF16), 1.0 / e.sum(axis=1, keepdims=True)))
        for (q0, k0, var, h, cols), (e, inv) in zip(units, ps):
            o = jnp.dot(e, v_ref[pl.ds(k0, strip), cols], preferred_element_type=F32) * inv
            o_ref[pl.ds(q0, GRID_W), cols] = o.astype(o_ref.dtype)
        return carry

    lax.fori_loop(0, rows // NBR_ROWS, body, 0)


def neighbourhood_attention(vqkv, bias_tbl, batch, seq):
    gw = NBR_HG * HEAD_DIM
    q0, k0, v0 = KVA_W // gw, (KVA_W + QB_W) // gw, (KVA_W + 2 * QB_W) // gw
    kh = bias_tbl.shape[1]
    return pl.pallas_call(
        functools.partial(_nbr_body, seq=seq),
        grid=(B_HEADS // NBR_HG, batch),
        in_specs=[pl.BlockSpec((seq, gw), lambda g, b: (b, q0 + g)),
                  pl.BlockSpec((seq, gw), lambda g, b: (b, k0 + g)),
                  pl.BlockSpec((seq, gw), lambda g, b: (b, v0 + g)),
                  pl.BlockSpec((NBR_HG, kh, GRID_W, kh * GRID_W), lambda g, b: (g, 0, 0, 0))],
        out_specs=pl.BlockSpec((seq, gw), lambda g, b: (b, g)),
        out_shape=jax.ShapeDtypeStruct((batch * seq, QB_W), BF16),
        compiler_params=_params(("arbitrary", "arbitrary"), 40),
        name="neighbourhood_attention",
    )(vqkv, vqkv, vqkv, bias_tbl)


def _bias_table_body(rpb_ref, o_ref, *, kh):
    h = pl.program_id(0)
    n_dr, n_dc = 2 * NA_KH_MAX - 1, 2 * NA_KW - 1
    c = lax.broadcasted_iota(jnp.int32, (GRID_W, LANES), 0)
    lane = lax.broadcasted_iota(jnp.int32, (GRID_W, LANES), 1)
    kc = lane % GRID_W
    diff = jnp.clip(kc - c + NA_KW - 1, 0, n_dc - 1)
    cs = jnp.clip(c - NA_KW // 2, 0, GRID_W - NA_KW)
    col_ok = (kc >= cs) & (kc < cs + NA_KW)
    slabs = []
    for dr in range(n_dr):
        acc = jnp.zeros((GRID_W, LANES), F32)
        for d in range(n_dc):
            acc = jnp.where(diff == d, rpb_ref[(h * n_dr + dr) * n_dc + d], acc)
        slabs.append(jnp.where(col_ok, acc, NEG))
    left = lane < GRID_W
    for var in range(kh):
        for jp in range(kh * GRID_W // LANES):
            dr0 = 2 * jp - var + NA_KH_MAX - 1
            o_ref[0, var, :, jp * LANES:(jp + 1) * LANES] = jnp.where(left, slabs[dr0], slabs[dr0 + 1])


def _nbr_bias_table(rpb, seq):
    rows = seq // GRID_W
    kh = min(NA_KH_MAX, rows)
    heads = rpb.shape[0]
    assert kh == NA_KH_MAX and 2 * GRID_W == LANES
    return pl.pallas_call(
        functools.partial(_bias_table_body, kh=kh),
        grid=(heads,),
        in_specs=[pl.BlockSpec(memory_space=pltpu.SMEM)],
        out_specs=pl.BlockSpec((1, kh, GRID_W, kh * GRID_W), lambda h: (h, 0, 0, 0)),
        out_shape=jax.ShapeDtypeStruct((heads, kh, GRID_W, kh * GRID_W), F32),
        compiler_params=_params(("arbitrary",), 16),
        name="nbr_bias_table",
    )(rpb.astype(F32).reshape(-1))


def _merge_body(oa_ref, ob_ref, wa_ref, wb_ref, g0_ref, g1_ref, o_ref, wa_s, wb_s, *, k_rows):
    @pl.when(pl.program_id(1) == 0)
    def _():
        _cast_rows(wa_ref, wa_s, k_rows)
        _cast_rows(wb_ref, wb_s, k_rows)

    ya = jnp.dot(oa_ref[...], wa_s[...], preferred_element_type=F32)
    yb = jnp.dot(ob_ref[...], wb_s[...], preferred_element_type=F32)
    o_ref[...] = (g0_ref[...].astype(F32) * ya + g1_ref[...].astype(F32) * yb).astype(o_ref.dtype)


def branch_merge(oa, ob, wa, wb, gates, bm=1024, bn=512):
    m, k = oa.shape
    n = wa.shape[1]
    g1_off = n // bn
    return pl.pallas_call(
        functools.partial(_merge_body, k_rows=k),
        grid=(n // bn, m // bm),
        in_specs=[pl.BlockSpec((bm, k), lambda j, i: (i, 0)),
                  pl.BlockSpec((bm, k), lambda j, i: (i, 0)),
                  pl.BlockSpec((k, bn), lambda j, i: (0, j)),
                  pl.BlockSpec((k, bn), lambda j, i: (0, j)),
                  pl.BlockSpec((bm, bn), lambda j, i: (i, j)),
                  pl.BlockSpec((bm, bn), lambda j, i: (i, j + g1_off))],
        out_specs=pl.BlockSpec((bm, bn), lambda j, i: (i, j)),
        out_shape=jax.ShapeDtypeStruct((m, n), BF16),
        scratch_shapes=[pltpu.VMEM((k, bn), BF16), pltpu.VMEM((k, bn), BF16)],
        compiler_params=_params(("arbitrary", "arbitrary"), 56),
        name="branch_merge",
    )(oa, ob, wa, wb, gates, gates)


def _cast_body(x_ref, o_ref):
    o_ref[...] = x_ref[...].astype(o_ref.dtype)


def cast_bf16(w):
    r, c = w.shape
    return pl.pallas_call(
        _cast_body,
        grid=(1,),
        in_specs=[pl.BlockSpec((r, c), lambda i: (0, 0))],
        out_specs=pl.BlockSpec((r, c), lambda i: (0, 0)),
        out_shape=jax.ShapeDtypeStruct((r, c), BF16),
        compiler_params=_params(("arbitrary",), 40),
        name="cast_bf16",
    )(w)


def _router_probs(hn, wr_ref):
    hi = hn.astype(BF16)
    lo = (hn - hi.astype(F32)).astype(BF16)
    l_hi = jnp.dot(hi, wr_ref[...], preferred_element_type=F32)
    l_lo = jnp.dot(lo, wr_ref[...], preferred_element_type=F32)
    logits = l_hi + pltpu.roll(l_hi, LANES - N_EXPERTS, 1) + l_lo
    lane = lax.broadcasted_iota(jnp.int32, logits.shape, 1)
    logits = jnp.where(lane < N_EXPERTS, logits, NEG)
    (aff,) = _softmax_parts([logits])
    return aff


def _xblock_body(x_ref, gc_ref, wq_ref, k_ref, v_ref, wo_ref, gf_ref, wr_ref,
                 x2_ref, h3_ref, aff_ref, atm_ref):
    scale = HEAD_DIM ** -0.5
    x = x_ref[...]
    h2 = _rmsnorm_rows(x, gc_ref[...]).astype(BF16)
    q = jnp.dot(h2, wq_ref[...], preferred_element_type=F32).astype(BF16)
    heads = []
    for h in range(X_HEADS):
        cols = slice(h * HEAD_DIM, (h + 1) * HEAD_DIM)
        s = lax.dot_general(q[:, cols], k_ref[:, cols], _NT, preferred_element_type=F32) * scale
        (p,) = _softmax_parts([s])
        heads.append(jnp.dot(p.astype(BF16), v_ref[:, cols], preferred_element_type=F32).astype(BF16))
    o = jnp.concatenate(heads, axis=1)
    x2 = x + jnp.dot(o, wo_ref[...], preferred_element_type=F32)
    x2_ref[...] = x2
    hn = _rmsnorm_rows(x2, gf_ref[...])
    h3_ref[...] = hn.astype(h3_ref.dtype)
    aff = _router_probs(hn, wr_ref)
    atm_ref[...] = aff
    aff_ref[0] = aff.T[:N_EXPERTS, :]


def cross_attention_block(x1, g_cross, wq, kx, vx, wo, g_ffn, w_router, batch, seq, mem_len, bm=256):
    m, d = x1.shape
    nt = seq // bm
    w_hi = w_router.astype(BF16)
    w_lo = (w_router - w_hi.astype(F32)).astype(BF16)
    wr = jnp.concatenate([w_hi, w_lo, jnp.zeros((d, LANES - 2 * N_EXPERTS), BF16)], axis=1)
    const = lambda i: (0, 0)
    return pl.pallas_call(
        _xblock_body,
        grid=(m // bm,),
        in_specs=[pl.BlockSpec((bm, d), lambda i: (i, 0)),
                  pl.BlockSpec((1, d), const),
                  pl.BlockSpec((d, X_W), const),
                  pl.BlockSpec((mem_len, X_W), lambda i: (i // nt, 0)),
                  pl.BlockSpec((mem_len, X_W), lambda i: (i // nt, 0)),
                  pl.BlockSpec((X_W, d), const),
                  pl.BlockSpec((1, d), const),
                  pl.BlockSpec((d, LANES), const)],
        out_specs=[pl.BlockSpec((bm, d), lambda i: (i, 0)),
                   pl.BlockSpec((bm, d), lambda i: (i, 0)),
                   pl.BlockSpec((1, N_EXPERTS, bm), lambda i: (i // nt, 0, i % nt)),
                   pl.BlockSpec((bm, LANES), lambda i: (i, 0))],
        out_shape=[jax.ShapeDtypeStruct((m, d), F32),
                   jax.ShapeDtypeStruct((m, d), BF16),
                   jax.ShapeDtypeStruct((batch, N_EXPERTS, seq), F32),
                   jax.ShapeDtypeStruct((m, LANES), F32)],
        compiler_params=_params(("arbitrary",), 48),
        name="cross_attention_block",
    )(x1, g_cross.reshape(1, d), cast_bf16(wq), kx, vx, cast_bf16(wo), g_ffn.reshape(1, d), wr)


CUM_CHUNK = 256


def _excl_cumsum_lanes(x01, tri):
    n = x01.shape[1]
    carry = jnp.zeros((x01.shape[0], 1), F32)
    out = []
    for c in range(n // CUM_CHUNK):
        xc = x01[:, c * CUM_CHUNK:(c + 1) * CUM_CHUNK]
        out.append(jnp.dot(xc.astype(BF16), tri, preferred_element_type=F32) + carry)
        carry = carry + xc.sum(axis=1, keepdims=True)
    return jnp.concatenate(out, axis=1)


RANK_CHUNK = 256
SUBLANES = 8


def _topk_body(aff_ref, atm_ref, slot_ref, cgt_ref, *, cap):
    n_exp, s = aff_ref.shape[1], aff_ref.shape[2]
    capf = jnp.float32(cap)
    for e in range(n_exp):
        a_row = aff_ref[0, e:e + 1, :]

        def chunk(c, acc, a_row=a_row, e=e):
            r = pl.multiple_of(c * RANK_CHUNK, RANK_CHUNK)
            col = atm_ref[pl.ds(r, RANK_CHUNK), e:e + 1]
            above = jnp.where(col > a_row, 1.0, 0.0)
            return acc + above.reshape(RANK_CHUNK // SUBLANES, SUBLANES, s).sum(axis=0)

        acc = lax.fori_loop(0, s // RANK_CHUNK, chunk, jnp.zeros((SUBLANES, s), F32))
        cgt_ref[e:e + 1, :] = acc.sum(axis=0, keepdims=True)
    cgt = cgt_ref[...]
    ri = lax.broadcasted_iota(jnp.int32, (CUM_CHUNK, CUM_CHUNK), 0)
    cj = lax.broadcasted_iota(jnp.int32, (CUM_CHUNK, CUM_CHUNK), 1)
    tri = jnp.where(ri < cj, 1.0, 0.0).astype(BF16)
    cand = jnp.where(cgt < capf, 1.0, 0.0)
    extra = cand.sum(axis=1, keepdims=True) - capf
    g_last = jnp.where(cand > 0.5, cgt, -1.0).max(axis=1, keepdims=True)
    tie = jnp.where(cgt == g_last, cand, 0.0)
    keep = tie.sum(axis=1, keepdims=True) - extra
    sel = cand - jnp.where(_excl_cumsum_lanes(tie, tri) >= keep, tie, 0.0)
    pos = _excl_cumsum_lanes(sel, tri)
    slot_ref[0] = jnp.where(sel > 0.5, pos, -1.0).astype(jnp.int32)


def expert_topk(aff_t, aff_tm, cap):
    b, e, s = aff_t.shape
    return pl.pallas_call(
        functools.partial(_topk_body, cap=cap),
        grid=(b,),
        in_specs=[pl.BlockSpec((1, e, s), lambda i: (i, 0, 0)),
                  pl.BlockSpec((s, LANES), lambda i: (i, 0))],
        out_specs=pl.BlockSpec((1, e, s), lambda i: (i, 0, 0)),
        out_shape=jax.ShapeDtypeStruct((b, e, s), jnp.int32),
        scratch_shapes=[pltpu.VMEM((e, s), F32)],
        compiler_params=_params(("arbitrary",), 32),
        name="expert_topk",
    )(aff_t, aff_tm)


def _gather_body(slot_ref, aff_ref, h_ref, xg_ref, val_ref, *, cap):
    e = pl.program_id(1)
    srow = slot_ref[0, pl.ds(e, 1), :]
    arow = aff_ref[0, pl.ds(e, 1), :]
    ci = lax.broadcasted_iota(jnp.int32, (cap, srow.shape[1]), 0)
    hit = srow == ci
    onehot = jnp.where(hit, 1.0, 0.0).astype(BF16)
    xg_ref[0] = jnp.dot(onehot, h_ref[0], preferred_element_type=F32).astype(xg_ref.dtype)
    val_ref[0] = jnp.where(hit, arow, 0.0).sum(axis=1, keepdims=True)


def expert_gather(slot, aff_t, h3, cap):
    b, e, s = slot.shape
    d = h3.shape[-1]
    return pl.pallas_call(
        functools.partial(_gather_body, cap=cap),
        grid=(b, e),
        in_specs=[pl.BlockSpec((1, e, s), lambda bi, ei: (bi, 0, 0)),
                  pl.BlockSpec((1, e, s), lambda bi, ei: (bi, 0, 0)),
                  pl.BlockSpec((1, s, d), lambda bi, ei: (bi, 0, 0))],
        out_specs=[pl.BlockSpec((1, cap, d), lambda bi, ei: (ei, bi, 0)),
                   pl.BlockSpec((1, cap, 1), lambda bi, ei: (ei, bi, 0))],
        out_shape=[jax.ShapeDtypeStruct((e, b * cap, d), BF16),
                   jax.ShapeDtypeStruct((e, b * cap, 1), F32)],
        compiler_params=_params(("arbitrary", "arbitrary"), 56),
        name="expert_gather",
    )(slot, aff_t, h3)


def _expert_up_body(x_ref, wg_ref, wu_ref, o_ref):
    x = x_ref[0]
    a = jnp.dot(x, wg_ref[0].astype(BF16), preferred_element_type=F32)
    u = jnp.dot(x, wu_ref[0].astype(BF16), preferred_element_type=F32)
    o_ref[0] = (jax.nn.silu(a) * u).astype(o_ref.dtype)


def expert_up(xg, w_gate, w_up, tf=256):
    e, rows, d = xg.shape
    f = w_gate.shape[-1]
    return pl.pallas_call(
        _expert_up_body,
        grid=(e, f // tf),
        in_specs=[pl.BlockSpec((1, rows, d), lambda ei, fi: (ei, 0, 0)),
                  pl.BlockSpec((1, d, tf), lambda ei, fi: (ei, 0, fi)),
                  pl.BlockSpec((1, d, tf), lambda ei, fi: (ei, 0, fi))],
        out_specs=pl.BlockSpec((1, rows, tf), lambda ei, fi: (ei, 0, fi)),
        out_shape=jax.ShapeDtypeStruct((e, rows, f), BF16),
        compiler_params=_params(("arbitrary", "arbitrary"), 56),
        name="expert_up",
    )(xg, w_gate, w_up)


def _expert_down_body(h_ref, wd_ref, val_ref, o_ref):
    y = jnp.dot(h_ref[0], wd_ref[0].astype(BF16), preferred_element_type=F32)
    o_ref[0] = (y * val_ref[0]).astype(o_ref.dtype)


def expert_down(hmid, w_down, valc, tn=512):
    e, rows, f = hmid.shape
    d = w_down.shape[-1]
    return pl.pallas_call(
        _expert_down_body,
        grid=(e, d // tn),
        in_specs=[pl.BlockSpec((1, rows, f), lambda ei, ni: (ei, 0, 0)),
                  pl.BlockSpec((1, f, tn), lambda ei, ni: (ei, 0, ni)),
                  pl.BlockSpec((1, rows, 1), lambda ei, ni: (ei, 0, 0))],
        out_specs=pl.BlockSpec((1, rows, tn), lambda ei, ni: (ei, 0, ni)),
        out_shape=jax.ShapeDtypeStruct((e, rows, d), BF16),
        compiler_params=_params(("arbitrary", "arbitrary"), 48),
        name="expert_down",
    )(hmid, w_down, valc)


COMBINE_EG = 4


COMBINE_TN = 512


def _combine_body(slot_ref, y_ref, x_ref, g_ref, o_ref, oht_ref, *, cap):
    eg = pl.program_id(2)
    ts, d = o_ref.shape
    ci = lax.broadcasted_iota(jnp.int32, (cap, ts), 0)
    for k in range(COMBINE_EG):
        srow = slot_ref[0, pl.ds(eg * COMBINE_EG + k, 1), :]
        oht_ref[:, k * cap:(k + 1) * cap] = jnp.where(srow == ci, 1.0, 0.0).T.astype(BF16)

    @pl.when(eg == 0)
    def _():
        o_ref[...] = x_ref[...]

    for c in range(d // COMBINE_TN):
        cols = slice(c * COMBINE_TN, (c + 1) * COMBINE_TN)
        rows = y_ref[:, :, cols].reshape(COMBINE_EG * cap, COMBINE_TN)
        o_ref[:, cols] += jnp.dot(oht_ref[...], rows, preferred_element_type=F32)

    @pl.when(eg == pl.num_programs(2) - 1)
    def _():
        o_ref[...] = _rmsnorm_rows(o_ref[...], g_ref[...])


def expert_combine(slot, y, x2d, g, cap, ts=512):
    b, e, s = slot.shape
    d = x2d.shape[-1]
    nt = s // ts
    assert e // COMBINE_EG >= 2
    return pl.pallas_call(
        functools.partial(_combine_body, cap=cap),
        grid=(b, nt, e // COMBINE_EG),
        in_specs=[pl.BlockSpec((1, e, ts), lambda bi, si, gi: (bi, 0, si)),
                  pl.BlockSpec((COMBINE_EG, cap, d), lambda bi, si, gi: (gi, bi, 0)),
                  pl.BlockSpec((ts, d), lambda bi, si, gi: (bi * nt + si, 0)),
                  pl.BlockSpec((1, d), lambda bi, si, gi: (0, 0))],
        out_specs=pl.BlockSpec((ts, d), lambda bi, si, gi: (bi * nt + si, 0)),
        out_shape=jax.ShapeDtypeStruct(x2d.shape, F32),
        scratch_shapes=[pltpu.VMEM((ts, COMBINE_EG * cap), BF16)],
        compiler_params=_params(("arbitrary", "arbitrary", "arbitrary"), 60),
        name="expert_combine",
    )(slot, y, x2d, g.reshape(1, d))


def _rotary_tables(seq):
    half = ROT_DIM // 2
    inv = ROPE_THETA ** (-jnp.arange(half, dtype=F32) * 2.0 / ROT_DIM)
    ang = jnp.arange(seq).astype(F32)[:, None] * inv[None, :]
    cos, sin = jnp.cos(ang), jnp.sin(ang)
    ones = jnp.ones((seq, HEAD_DIM - ROT_DIM), F32)
    zeros = jnp.zeros((seq, HEAD_DIM - ROT_DIM), F32)
    zh = jnp.zeros((seq, half), F32)
    c = jnp.concatenate([cos, cos, ones], axis=1)
    s1 = jnp.concatenate([-sin, zh, zeros], axis=1)
    s2 = jnp.concatenate([zh, sin, zeros], axis=1)
    return c, s1, s2


def kernel(x, mem, norm_mix, w_in, b_gate, sink, rpb, w_branch_a, w_branch_b, w_out,
           norm_cross, norm_mem, wq_x, wk_x, wv_x, wo_x, norm_ffn, w_router,
           w_gate, w_up, w_down, norm_final):
    batch, seq, d = x.shape
    mem_len = mem.shape[1]
    m = batch * seq
    assert norm_mix.shape[0] == 1, "final RMSNorm is fused into the single layer's last kernel"
    cap = EC_CAPACITY * seq // N_EXPERTS
    bm, bn = 1024, 512
    sb = seq // bm
    x0 = x.reshape(m, d)

    h = rmsnorm(x0, norm_mix[0], BF16)
    rot = _rotary_tables(seq)
    rot_specs = [pl.BlockSpec((bm, HEAD_DIM), lambda j, i: (i % sb, 0))] * 3
    qk = matmul(h, w_in[0], col_off=0, n_cols=QA_W + KVA_W, bm=bm, bn=bn, out_dtype=BF16,
                epilogue=_ep_rotary, extras=rot, extra_specs=rot_specs, name="in_proj_rotary")
    vqkv = matmul(h, w_in[0], col_off=QA_W + KVA_W, n_cols=KVA_W + 3 * QB_W, bm=bm, bn=bn,
                  out_dtype=BF16, name="in_proj_plain")
    g_off = QA_W + 2 * KVA_W + 3 * QB_W
    gates = matmul(h, w_in[0], col_off=g_off, n_cols=2 * d, bm=bm, bn=bn, out_dtype=BF16,
                   epilogue=_ep_sigmoid, extras=(b_gate[0].reshape(1, 2 * d),),
                   extra_specs=[pl.BlockSpec((1, bn), lambda j, i: (0, j))], name="in_proj_gates")
    oa = window_attention(qk, vqkv, sink[0], batch, seq)
    ob = neighbourhood_attention(vqkv, _nbr_bias_table(rpb[0], seq), batch, seq)
    merged = branch_merge(oa, ob, w_branch_a[0], w_branch_b[0], gates)
    res_spec = [pl.BlockSpec((bm, bn), lambda j, i: (i, j))]
    x1 = matmul(merged, w_out[0], col_off=0, n_cols=d, bm=bm, bn=bn, out_dtype=F32,
                epilogue=_ep_residual, extras=(x0,), extra_specs=res_spec, name="out_proj")

    mn = rmsnorm(mem.reshape(batch * mem_len, d), norm_mem[0], BF16)
    kx = matmul(mn, wk_x[0], col_off=0, n_cols=X_W, bm=bm, bn=bn, out_dtype=BF16, name="xattn_k")
    vx = matmul(mn, wv_x[0], col_off=0, n_cols=X_W, bm=bm, bn=bn, out_dtype=BF16, name="xattn_v")
    x2, h3, aff_t, aff_tm = cross_attention_block(
        x1, norm_cross[0], wq_x[0], kx, vx, wo_x[0], norm_ffn[0], w_router[0], batch, seq, mem_len)

    slot = expert_topk(aff_t, aff_tm, cap)
    xg, valc = expert_gather(slot, aff_t, h3.reshape(batch, seq, d), cap)
    hmid = expert_up(xg, w_gate[0], w_up[0])
    y = expert_down(hmid, w_down[0], valc)
    out = expert_combine(slot, y, x2, norm_final, cap)
    return out.reshape(batch, seq, d)
```

```python
import functools

import jax
import jax.numpy as jnp
from jax import lax
from jax.experimental import pallas as pl
from jax.experimental.pallas import tpu as pltpu

F32 = jnp.float32
BF16 = jnp.bfloat16

HEAD_DIM = 128
A_HEADS = 16
A_KV_HEADS = 4
A_GROUP = A_HEADS // A_KV_HEADS
WINDOW = 128
A_BLOCK = 128
ROT_DIM = HEAD_DIM // 4
ROPE_THETA = 500000.0
B_HEADS = 16
GRID_W = 64
NA_KH_MAX = 8
NA_KW = 16
X_HEADS = 4
N_EXPERTS = 16
EC_CAPACITY = 2
EPS = 1e-6
NEG = -1e30
LANES = 128
MIB = 1024 * 1024

QA_W = A_HEADS * HEAD_DIM
KVA_W = A_KV_HEADS * HEAD_DIM
QB_W = B_HEADS * HEAD_DIM
X_W = X_HEADS * HEAD_DIM

_NT = (((1,), (1,)), ((), ()))
_TN = (((0,), (0,)), ((), ()))


def _params(semantics, vmem_mib):
    return pltpu.CompilerParams(dimension_semantics=semantics,
                                vmem_limit_bytes=vmem_mib * MIB)


def _cast_rows(src_ref, dst_ref, rows, chunk=256):
    def body(k, carry):
        r = pl.multiple_of(k * chunk, chunk)
        dst_ref[pl.ds(r, chunk), :] = src_ref[pl.ds(r, chunk), :].astype(dst_ref.dtype)
        return carry
    lax.fori_loop(0, rows // chunk, body, 0)


def _rmsnorm_rows(x, g):
    ms = jnp.mean(x * x, axis=-1, keepdims=True)
    return x * lax.rsqrt(ms + EPS) * g


def _rmsnorm_body(x_ref, g_ref, o_ref):
    o_ref[...] = _rmsnorm_rows(x_ref[...], g_ref[...]).astype(o_ref.dtype)


def rmsnorm(x2d, g, out_dtype, bm=256):
    m, d = x2d.shape
    return pl.pallas_call(
        _rmsnorm_body,
        grid=(m // bm,),
        in_specs=[pl.BlockSpec((bm, d), lambda i: (i, 0)),
                  pl.BlockSpec((1, d), lambda i: (0, 0))],
        out_specs=pl.BlockSpec((bm, d), lambda i: (i, 0)),
        out_shape=jax.ShapeDtypeStruct((m, d), out_dtype),
        compiler_params=_params(("arbitrary",), 40),
        name="rmsnorm",
    )(x2d, g.reshape(1, d))


def _ep_store(acc, o_ref):
    o_ref[...] = acc.astype(o_ref.dtype)


def _ep_residual(acc, o_ref, r_ref):
    o_ref[...] = (r_ref[...] + acc).astype(o_ref.dtype)


def _ep_sigmoid(acc, o_ref, b_ref):
    o_ref[...] = jax.nn.sigmoid(acc + b_ref[...]).astype(o_ref.dtype)


def _ep_rotary(acc, o_ref, c_ref, s1_ref, s2_ref):
    c, s1, s2 = c_ref[...], s1_ref[...], s2_ref[...]
    half = ROT_DIM // 2
    for h in range(acc.shape[1] // HEAD_DIM):
        a = acc[:, h * HEAD_DIM:(h + 1) * HEAD_DIM]
        r = a * c + pltpu.roll(a, HEAD_DIM - half, 1) * s1 + pltpu.roll(a, half, 1) * s2
        o_ref[:, h * HEAD_DIM:(h + 1) * HEAD_DIM] = r.astype(o_ref.dtype)


def _mm_body(*refs, n_extra, epilogue, k_rows):
    a_ref, w_ref = refs[0], refs[1]
    extra = refs[2:2 + n_extra]
    o_ref = refs[2 + n_extra]
    wb_ref = refs[3 + n_extra]

    @pl.when(pl.program_id(1) == 0)
    def _():
        _cast_rows(w_ref, wb_ref, k_rows)

    acc = jnp.dot(a_ref[...], wb_ref[...], preferred_element_type=F32)
    epilogue(acc, o_ref, *extra)


def matmul(a, w, *, col_off, n_cols, bm, bn, out_dtype, epilogue=_ep_store,
           extras=(), extra_specs=(), vmem_mib=56, name="matmul"):
    m, k = a.shape
    off = col_off // bn
    assert col_off % bn == 0 and n_cols % bn == 0 and m % bm == 0
    body = functools.partial(_mm_body, n_extra=len(extras), epilogue=epilogue, k_rows=k)
    return pl.pallas_call(
        body,
        grid=(n_cols // bn, m // bm),
        in_specs=[pl.BlockSpec((bm, k), lambda j, i: (i, 0)),
                  pl.BlockSpec((k, bn), lambda j, i: (0, j + off))] + list(extra_specs),
        out_specs=pl.BlockSpec((bm, bn), lambda j, i: (i, j)),
        out_shape=jax.ShapeDtypeStruct((m, n_cols), out_dtype),
        scratch_shapes=[pltpu.VMEM((k, bn), BF16)],
        compiler_params=_params(("arbitrary", "arbitrary"), vmem_mib),
        name=name,
    )(a, w, *extras)


def _softmax_parts(parts, extra_col=None):
    m = parts[0].max(axis=1, keepdims=True)
    for p in parts[1:]:
        m = jnp.maximum(m, p.max(axis=1, keepdims=True))
    if extra_col is not None:
        m = jnp.maximum(m, extra_col)
    es = [jnp.exp(p - m) for p in parts]
    den = es[0].sum(axis=1, keepdims=True)
    for e in es[1:]:
        den = den + e.sum(axis=1, keepdims=True)
    if extra_col is not None:
        den = den + jnp.exp(extra_col - m)
    inv = 1.0 / den
    return [e * inv for e in es]


WIN_UNROLL = 4


def _win_body(sink_ref, q_ref, k_ref, v_ref, o_ref, *, seq):
    kv = pl.program_id(1)
    nb = seq // A_BLOCK
    scale = HEAD_DIM ** -0.5
    rows = A_GROUP * A_BLOCK
    qi = lax.broadcasted_iota(jnp.int32, (rows, A_BLOCK), 0) % A_BLOCK
    ci = lax.broadcasted_iota(jnp.int32, (rows, A_BLOCK), 1)
    sink_b = jnp.concatenate(
        [jnp.full((A_BLOCK, HEAD_DIM), sink_ref[kv * A_GROUP + g], F32) for g in range(A_GROUP)], axis=0)

    def scores(n):
        r0 = pl.multiple_of(n * A_BLOCK, A_BLOCK)
        rp = pl.multiple_of(jnp.maximum(n - 1, 0) * A_BLOCK, A_BLOCK)
        rn = pl.multiple_of(jnp.minimum(n + 1, nb - 1) * A_BLOCK, A_BLOCK)
        off_p = jnp.where(n > 0, 0, 2 * A_BLOCK)
        off_n = jnp.where(n < nb - 1, 0, 2 * A_BLOCK)
        q = jnp.concatenate(
            [q_ref[pl.ds(r0, A_BLOCK), g * HEAD_DIM:(g + 1) * HEAD_DIM] for g in range(A_GROUP)],
            axis=0)
        sp = lax.dot_general(q, k_ref[pl.ds(rp, A_BLOCK), :], _NT, preferred_element_type=F32) * scale
        sc = lax.dot_general(q, k_ref[pl.ds(r0, A_BLOCK), :], _NT, preferred_element_type=F32) * scale
        sn = lax.dot_general(q, k_ref[pl.ds(rn, A_BLOCK), :], _NT, preferred_element_type=F32) * scale
        sp = jnp.where(ci >= qi + off_p, sp, NEG)
        sn = jnp.where(ci <= qi - off_n, sn, NEG)
        return (rp, r0, rn), (sp, sc, sn)

    def exps(parts):
        m = jnp.maximum(jnp.maximum(parts[0], parts[1]), parts[2]).max(axis=1, keepdims=True)
        m = jnp.maximum(jnp.broadcast_to(m, sink_b.shape), sink_b)
        es = [jnp.exp(p - m) for p in parts]
        den = (es[0] + es[1] + es[2]).sum(axis=1, keepdims=True)
        den = jnp.broadcast_to(den, sink_b.shape) + jnp.exp(sink_b - m)
        return [e.astype(BF16) for e in es], 1.0 / den

    def body(it, carry):
        blocks = [scores(it * WIN_UNROLL + u) for u in range(WIN_UNROLL)]
        probs = [exps(parts) for _, parts in blocks]
        for (rows_kv, _), (es, inv) in zip(blocks, probs):
            o = jnp.dot(es[0], v_ref[pl.ds(rows_kv[0], A_BLOCK), :], preferred_element_type=F32)
            o = o + jnp.dot(es[1], v_ref[pl.ds(rows_kv[1], A_BLOCK), :], preferred_element_type=F32)
            o = o + jnp.dot(es[2], v_ref[pl.ds(rows_kv[2], A_BLOCK), :], preferred_element_type=F32)
            o = o * inv
            for g in range(A_GROUP):
                o_ref[pl.ds(rows_kv[1], A_BLOCK), g * HEAD_DIM:(g + 1) * HEAD_DIM] = (
                    o[g * A_BLOCK:(g + 1) * A_BLOCK].astype(o_ref.dtype))
        return carry

    lax.fori_loop(0, nb // WIN_UNROLL, body, 0)


def window_attention(qk, vqkv, sink, batch, seq):
    gw = A_GROUP * HEAD_DIM
    k_blk0 = QA_W // HEAD_DIM
    return pl.pallas_call(
        functools.partial(_win_body, seq=seq),
        grid=(batch, A_KV_HEADS),
        in_specs=[pl.BlockSpec(memory_space=pltpu.SMEM),
                  pl.BlockSpec((seq, gw), lambda b, h: (b, h)),
                  pl.BlockSpec((seq, HEAD_DIM), lambda b, h: (b, k_blk0 + h)),
                  pl.BlockSpec((seq, HEAD_DIM), lambda b, h: (b, h))],
        out_specs=pl.BlockSpec((seq, gw), lambda b, h: (b, h)),
        out_shape=jax.ShapeDtypeStruct((batch * seq, QA_W), BF16),
        compiler_params=_params(("arbitrary", "arbitrary"), 32),
        name="window_attention",
    )(sink, qk, qk, vqkv)


NBR_HG = 4
NBR_ROWS = 4


def _nbr_body(q_ref, k_ref, v_ref, bias_ref, o_ref, *, seq):
    rows = seq // GRID_W
    kh = min(NA_KH_MAX, rows)
    strip = kh * GRID_W
    scale = HEAD_DIM ** -0.5

    def body(it, carry):
        units = []
        for rr in range(NBR_ROWS):
            r = it * NBR_ROWS + rr
            rs = jnp.clip(r - kh // 2, 0, rows - kh)
            q0 = pl.multiple_of(r * GRID_W, GRID_W)
            k0 = pl.multiple_of(rs * GRID_W, GRID_W)
            for h in range(NBR_HG):
                units.append((q0, k0, r - rs, h, slice(h * HEAD_DIM, (h + 1) * HEAD_DIM)))
        ss = [lax.dot_general(q_ref[pl.ds(q0, GRID_W), cols], k_ref[pl.ds(k0, strip), cols], _NT,
                              preferred_element_type=F32) * scale + bias_ref[h, var]
              for q0, k0, var, h, cols in units]
        ps = []
        for s in ss:
            e = jnp.exp(s - s.max(axis=1, keepdims=True))
            ps.append((e.astype(BF16), 1.0 / e.sum(axis=1, keepdims=True)))
        for (q0, k0, var, h, cols), (e, inv) in zip(units, ps):
            o = jnp.dot(e, v_ref[pl.ds(k0, strip), cols], preferred_element_type=F32) * inv
            o_ref[pl.ds(q0, GRID_W), cols] = o.astype(o_ref.dtype)
        return carry

    lax.fori_loop(0, rows // NBR_ROWS, body, 0)


def neighbourhood_attention(vqkv, bias_tbl, batch, seq):
    gw = NBR_HG * HEAD_DIM
    q0, k0, v0 = KVA_W // gw, (KVA_W + QB_W) // gw, (KVA_W + 2 * QB_W) // gw
    kh = bias_tbl.shape[1]
    return pl.pallas_call(
        functools.partial(_nbr_body, seq=seq),
        grid=(B_HEADS // NBR_HG, batch),
        in_specs=[pl.BlockSpec((seq, gw), lambda g, b: (b, q0 + g)),
                  pl.BlockSpec((seq, gw), lambda g, b: (b, k0 + g)),
                  pl.BlockSpec((seq, gw), lambda g, b: (b, v0 + g)),
                  pl.BlockSpec((NBR_HG, kh, GRID_W, kh * GRID_W), lambda g, b: (g, 0, 0, 0))],
        out_specs=pl.BlockSpec((seq, gw), lambda g, b: (b, g)),
        out_shape=jax.ShapeDtypeStruct((batch * seq, QB_W), BF16),
        compiler_params=_params(("arbitrary", "arbitrary"), 40),
        name="neighbourhood_attention",
    )(vqkv, vqkv, vqkv, bias_tbl)


def _bias_table_body(rpb_ref, o_ref, *, kh):
    h = pl.program_id(0)
    n_dr, n_dc = 2 * NA_KH_MAX - 1, 2 * NA_KW - 1
    c = lax.broadcasted_iota(jnp.int32, (GRID_W, LANES), 0)
    lane = lax.broadcasted_iota(jnp.int32, (GRID_W, LANES), 1)
    kc = lane % GRID_W
    diff = jnp.clip(kc - c + NA_KW - 1, 0, n_dc - 1)
    cs = jnp.clip(c - NA_KW // 2, 0, GRID_W - NA_KW)
    col_ok = (kc >= cs) & (kc < cs + NA_KW)
    slabs = []
    for dr in range(n_dr):
        acc = jnp.zeros((GRID_W, LANES), F32)
        for d in range(n_dc):
            acc = jnp.where(diff == d, rpb_ref[(h * n_dr + dr) * n_dc + d], acc)
        slabs.append(jnp.where(col_ok, acc, NEG))
    left = lane < GRID_W
    for var in range(kh):
        for jp in range(kh * GRID_W // LANES):
            dr0 = 2 * jp - var + NA_KH_MAX - 1
            o_ref[0, var, :, jp * LANES:(jp + 1) * LANES] = jnp.where(left, slabs[dr0], slabs[dr0 + 1])


def _nbr_bias_table(rpb, seq):
    rows = seq // GRID_W
    kh = min(NA_KH_MAX, rows)
    heads = rpb.shape[0]
    assert kh == NA_KH_MAX and 2 * GRID_W == LANES
    return pl.pallas_call(
        functools.partial(_bias_table_body, kh=kh),
        grid=(heads,),
        in_specs=[pl.BlockSpec(memory_space=pltpu.SMEM)],
        out_specs=pl.BlockSpec((1, kh, GRID_W, kh * GRID_W), lambda h: (h, 0, 0, 0)),
        out_shape=jax.ShapeDtypeStruct((heads, kh, GRID_W, kh * GRID_W), F32),
        compiler_params=_params(("arbitrary",), 16),
        name="nbr_bias_table",
    )(rpb.astype(F32).reshape(-1))


def _merge_body(oa_ref, ob_ref, wa_ref, wb_ref, g0_ref, g1_ref, o_ref, wa_s, wb_s, *, k_rows):
    @pl.when(pl.program_id(1) == 0)
    def _():
        _cast_rows(wa_ref, wa_s, k_rows)
        _cast_rows(wb_ref, wb_s, k_rows)

    ya = jnp.dot(oa_ref[...], wa_s[...], preferred_element_type=F32)
    yb = jnp.dot(ob_ref[...], wb_s[...], preferred_element_type=F32)
    o_ref[...] = (g0_ref[...].astype(F32) * ya + g1_ref[...].astype(F32) * yb).astype(o_ref.dtype)


def branch_merge(oa, ob, wa, wb, gates, bm=1024, bn=512):
    m, k = oa.shape
    n = wa.shape[1]
    g1_off = n // bn
    return pl.pallas_call(
        functools.partial(_merge_body, k_rows=k),
        grid=(n // bn, m // bm),
        in_specs=[pl.BlockSpec((bm, k), lambda j, i: (i, 0)),
                  pl.BlockSpec((bm, k), lambda j, i: (i, 0)),
                  pl.BlockSpec((k, bn), lambda j, i: (0, j)),
                  pl.BlockSpec((k, bn), lambda j, i: (0, j)),
                  pl.BlockSpec((bm, bn), lambda j, i: (i, j)),
                  pl.BlockSpec((bm, bn), lambda j, i: (i, j + g1_off))],
        out_specs=pl.BlockSpec((bm, bn), lambda j, i: (i, j)),
        out_shape=jax.ShapeDtypeStruct((m, n), BF16),
        scratch_shapes=[pltpu.VMEM((k, bn), BF16), pltpu.VMEM((k, bn), BF16)],
        compiler_params=_params(("arbitrary", "arbitrary"), 56),
        name="branch_merge",
    )(oa, ob, wa, wb, gates, gates)


def _cast_body(x_ref, o_ref):
    o_ref[...] = x_ref[...].astype(o_ref.dtype)


def cast_bf16(w):
    r, c = w.shape
    return pl.pallas_call(
        _cast_body,
        grid=(1,),
        in_specs=[pl.BlockSpec((r, c), lambda i: (0, 0))],
        out_specs=pl.BlockSpec((r, c), lambda i: (0, 0)),
        out_shape=jax.ShapeDtypeStruct((r, c), BF16),
        compiler_params=_params(("arbitrary",), 40),
        name="cast_bf16",
    )(w)


def _router_probs(hn, wr_ref):
    hi = hn.astype(BF16)
    lo = (hn - hi.astype(F32)).astype(BF16)
    l_hi = jnp.dot(hi, wr_ref[...], preferred_element_type=F32)
    l_lo = jnp.dot(lo, wr_ref[...], preferred_element_type=F32)
    logits = l_hi + pltpu.roll(l_hi, LANES - N_EXPERTS, 1) + l_lo
    lane = lax.broadcasted_iota(jnp.int32, logits.shape, 1)
    logits = jnp.where(lane < N_EXPERTS, logits, NEG)
    (aff,) = _softmax_parts([logits])
    return aff


def _xblock_body(x_ref, gc_ref, wq_ref, k_ref, v_ref, wo_ref, gf_ref, wr_ref,
                 x2_ref, h3_ref, aff_ref, atm_ref):
    scale = HEAD_DIM ** -0.5
    x = x_ref[...]
    h2 = _rmsnorm_rows(x, gc_ref[...]).astype(BF16)
    q = jnp.dot(h2, wq_ref[...], preferred_element_type=F32).astype(BF16)
    heads = []
    for h in range(X_HEADS):
        cols = slice(h * HEAD_DIM, (h + 1) * HEAD_DIM)
        s = lax.dot_general(q[:, cols], k_ref[:, cols], _NT, preferred_element_type=F32) * scale
        (p,) = _softmax_parts([s])
        heads.append(jnp.dot(p.astype(BF16), v_ref[:, cols], preferred_element_type=F32).astype(BF16))
    o = jnp.concatenate(heads, axis=1)
    x2 = x + jnp.dot(o, wo_ref[...], preferred_element_type=F32)
    x2_ref[...] = x2
    hn = _rmsnorm_rows(x2, gf_ref[...])
    half = hn.shape[1] // 2
    h3_ref[...] = pltpu.pack_elementwise([hn[:, :half], hn[:, half:]], packed_dtype=BF16)
    aff = _router_probs(hn, wr_ref)
    atm_ref[...] = aff
    aff_ref[0] = aff.T[:N_EXPERTS, :]


def cross_attention_block(x1, g_cross, wq, kx, vx, wo, g_ffn, w_router, batch, seq, mem_len, bm=256):
    m, d = x1.shape
    nt = seq // bm
    w_hi = w_router.astype(BF16)
    w_lo = (w_router - w_hi.astype(F32)).astype(BF16)
    wr = jnp.concatenate([w_hi, w_lo, jnp.zeros((d, LANES - 2 * N_EXPERTS), BF16)], axis=1)
    const = lambda i: (0, 0)
    return pl.pallas_call(
        _xblock_body,
        grid=(m // bm,),
        in_specs=[pl.BlockSpec((bm, d), lambda i: (i, 0)),
                  pl.BlockSpec((1, d), const),
                  pl.BlockSpec((d, X_W), const),
                  pl.BlockSpec((mem_len, X_W), lambda i: (i // nt, 0)),
                  pl.BlockSpec((mem_len, X_W), lambda i: (i // nt, 0)),
                  pl.BlockSpec((X_W, d), const),
                  pl.BlockSpec((1, d), const),
                  pl.BlockSpec((d, LANES), const)],
        out_specs=[pl.BlockSpec((bm, d), lambda i: (i, 0)),
                   pl.BlockSpec((bm, d // 2), lambda i: (i, 0)),
                   pl.BlockSpec((1, N_EXPERTS, bm), lambda i: (i // nt, 0, i % nt)),
                   pl.BlockSpec((bm, LANES), lambda i: (i, 0))],
        out_shape=[jax.ShapeDtypeStruct((m, d), F32),
                   jax.ShapeDtypeStruct((m, d // 2), jnp.int32),
                   jax.ShapeDtypeStruct((batch, N_EXPERTS, seq), F32),
                   jax.ShapeDtypeStruct((m, LANES), F32)],
        compiler_params=_params(("arbitrary",), 48),
        name="cross_attention_block",
    )(x1, g_cross.reshape(1, d), cast_bf16(wq), kx, vx, cast_bf16(wo), g_ffn.reshape(1, d), wr)


CUM_CHUNK = 256


def _excl_cumsum_lanes(x01, tri):
    n = x01.shape[1]
    carry = jnp.zeros((x01.shape[0], 1), F32)
    out = []
    for c in range(n // CUM_CHUNK):
        xc = x01[:, c * CUM_CHUNK:(c + 1) * CUM_CHUNK]
        out.append(jnp.dot(xc.astype(BF16), tri, preferred_element_type=F32) + carry)
        carry = carry + xc.sum(axis=1, keepdims=True)
    return jnp.concatenate(out, axis=1)


RANK_CHUNK = 256
SUBLANES = 8


def _topk_body(aff_ref, atm_ref, slot_ref, cgt_ref, *, cap):
    n_exp, s = aff_ref.shape[1], aff_ref.shape[2]
    capf = jnp.float32(cap)
    for e in range(n_exp):
        a_row = aff_ref[0, e:e + 1, :]

        def chunk(c, acc, a_row=a_row, e=e):
            r = pl.multiple_of(c * RANK_CHUNK, RANK_CHUNK)
            col = atm_ref[pl.ds(r, RANK_CHUNK), e:e + 1]
            above = jnp.where(col > a_row, 1.0, 0.0)
            return acc + above.reshape(RANK_CHUNK // SUBLANES, SUBLANES, s).sum(axis=0)

        acc = lax.fori_loop(0, s // RANK_CHUNK, chunk, jnp.zeros((SUBLANES, s), F32))
        cgt_ref[e:e + 1, :] = acc.sum(axis=0, keepdims=True)
    cgt = cgt_ref[...]
    ri = lax.broadcasted_iota(jnp.int32, (CUM_CHUNK, CUM_CHUNK), 0)
    cj = lax.broadcasted_iota(jnp.int32, (CUM_CHUNK, CUM_CHUNK), 1)
    tri = jnp.where(ri < cj, 1.0, 0.0).astype(BF16)
    cand = jnp.where(cgt < capf, 1.0, 0.0)
    extra = cand.sum(axis=1, keepdims=True) - capf
    g_last = jnp.where(cand > 0.5, cgt, -1.0).max(axis=1, keepdims=True)
    tie = jnp.where(cgt == g_last, cand, 0.0)
    keep = tie.sum(axis=1, keepdims=True) - extra
    sel = cand - jnp.where(_excl_cumsum_lanes(tie, tri) >= keep, tie, 0.0)
    pos = _excl_cumsum_lanes(sel, tri)
    slot_ref[0] = jnp.where(sel > 0.5, pos, -1.0).astype(jnp.int32)


def expert_topk(aff_t, aff_tm, cap):
    b, e, s = aff_t.shape
    return pl.pallas_call(
        functools.partial(_topk_body, cap=cap),
        grid=(b,),
        in_specs=[pl.BlockSpec((1, e, s), lambda i: (i, 0, 0)),
                  pl.BlockSpec((s, LANES), lambda i: (i, 0))],
        out_specs=pl.BlockSpec((1, e, s), lambda i: (i, 0, 0)),
        out_shape=jax.ShapeDtypeStruct((b, e, s), jnp.int32),
        scratch_shapes=[pltpu.VMEM((e, s), F32)],
        compiler_params=_params(("arbitrary",), 32),
        name="expert_topk",
    )(aff_t, aff_tm)


def _slot_index_body(slot_ref, aff_ref, idx_ref, val_ref, *, cap):
    b = pl.program_id(0)
    n_exp, s = slot_ref.shape[1], slot_ref.shape[2]
    ci = lax.broadcasted_iota(jnp.int32, (cap, s), 0)
    tok = lax.broadcasted_iota(jnp.int32, (cap, s), 1).astype(F32)
    base = (b * s).astype(F32)
    for e in range(n_exp):
        hit = slot_ref[0, e:e + 1, :] == ci
        idx = jnp.where(hit, tok, 0.0).sum(axis=1, keepdims=True) + base
        idx_ref[e] = idx.astype(jnp.int32)
        val_ref[e] = jnp.where(hit, aff_ref[0, e:e + 1, :], 0.0).sum(axis=1, keepdims=True)


def slot_index(slot, aff_t, cap):
    b, e, s = slot.shape
    return pl.pallas_call(
        functools.partial(_slot_index_body, cap=cap),
        grid=(b,),
        in_specs=[pl.BlockSpec((1, e, s), lambda bi: (bi, 0, 0)),
                  pl.BlockSpec((1, e, s), lambda bi: (bi, 0, 0))],
        out_specs=[pl.BlockSpec((e, cap, 1), lambda bi: (0, bi, 0)),
                   pl.BlockSpec((e, cap, 1), lambda bi: (0, bi, 0))],
        out_shape=[jax.ShapeDtypeStruct((e, b * cap, 1), jnp.int32),
                   jax.ShapeDtypeStruct((e, b * cap, 1), F32)],
        compiler_params=_params(("arbitrary",), 32),
        name="slot_index",
    )(slot, aff_t)


UNPACK_ROWS = 128


def _row_copy(idx_ref, hp_ref, gbuf, sem, expert, rows, row):
    tok = idx_ref[expert * rows + row]
    return pltpu.make_async_copy(hp_ref.at[pl.ds(tok, 1)], gbuf.at[pl.ds(row, 1)], sem)


def _expert_up_body(idx_ref, hp_ref, wg_ref, wu_ref, o_ref, gbuf, xbf, sem, *, rows, per_step):
    e, f = pl.program_id(0), pl.program_id(1)
    n_e, n_f = pl.num_programs(0), pl.num_programs(1)
    half = gbuf.shape[1]

    def wait_all_rows():
        pltpu.make_async_copy(hp_ref.at[pl.ds(0, rows)], gbuf, sem).wait()

    @pl.when(jnp.logical_and(e == 0, f == 0))
    def _():
        def body(r, carry):
            _row_copy(idx_ref, hp_ref, gbuf, sem, 0, rows, r).start()
            return carry
        lax.fori_loop(0, rows, body, 0)

    @pl.when(f == 0)
    def _():
        wait_all_rows()

        def unpack(k, carry):
            r = pl.multiple_of(k * UNPACK_ROWS, UNPACK_ROWS)
            w = gbuf[pl.ds(r, UNPACK_ROWS), :]
            lo = pltpu.unpack_elementwise(w, index=0, packed_dtype=BF16, unpacked_dtype=F32)
            hi = pltpu.unpack_elementwise(w, index=1, packed_dtype=BF16, unpacked_dtype=F32)
            xbf[pl.ds(r, UNPACK_ROWS), :half] = lo.astype(BF16)
            xbf[pl.ds(r, UNPACK_ROWS), half:] = hi.astype(BF16)
            return carry
        lax.fori_loop(0, rows // UNPACK_ROWS, unpack, 0)

    nxt = jnp.minimum(e + 1, n_e - 1)
    for r in range(per_step):
        _row_copy(idx_ref, hp_ref, gbuf, sem, nxt, rows, f * per_step + r).start()

    x = xbf[...]
    a = jnp.dot(x, wg_ref[0].astype(BF16), preferred_element_type=F32)
    u = jnp.dot(x, wu_ref[0].astype(BF16), preferred_element_type=F32)
    o_ref[0] = (jax.nn.silu(a) * u).astype(o_ref.dtype)

    @pl.when(jnp.logical_and(e == n_e - 1, f == n_f - 1))
    def _():
        wait_all_rows()


def expert_up(idx, hp, w_gate, w_up, rows, tf=256):
    n_e, d, f = w_gate.shape
    assert hp.shape[1] * 2 == d and rows % (f // tf) == 0
    grid_spec = pltpu.PrefetchScalarGridSpec(
        num_scalar_prefetch=1,
        grid=(n_e, f // tf),
        in_specs=[pl.BlockSpec(memory_space=pl.ANY),
                  pl.BlockSpec((1, d, tf), lambda ei, fi, idx_ref: (ei, 0, fi)),
                  pl.BlockSpec((1, d, tf), lambda ei, fi, idx_ref: (ei, 0, fi))],
        out_specs=pl.BlockSpec((1, rows, tf), lambda ei, fi, idx_ref: (ei, 0, fi)),
        scratch_shapes=[pltpu.VMEM((rows, d // 2), jnp.int32),
                        pltpu.VMEM((rows, d), BF16),
                        pltpu.SemaphoreType.DMA(())],
    )
    return pl.pallas_call(
        functools.partial(_expert_up_body, rows=rows, per_step=rows // (f // tf)),
        grid_spec=grid_spec,
        out_shape=jax.ShapeDtypeStruct((n_e, rows, f), BF16),
        compiler_params=_params(("arbitrary", "arbitrary"), 56),
        name="expert_up",
    )(idx, hp, w_gate, w_up)


def _expert_down_body(h_ref, wd_ref, val_ref, o_ref):
    y = jnp.dot(h_ref[0], wd_ref[0].astype(BF16), preferred_element_type=F32)
    o_ref[0] = (y * val_ref[0]).astype(o_ref.dtype)


def expert_down(hmid, w_down, valc, tn=512):
    e, rows, f = hmid.shape
    d = w_down.shape[-1]
    return pl.pallas_call(
        _expert_down_body,
        grid=(e, d // tn),
        in_specs=[pl.BlockSpec((1, rows, f), lambda ei, ni: (ei, 0, 0)),
                  pl.BlockSpec((1, f, tn), lambda ei, ni: (ei, 0, ni)),
                  pl.BlockSpec((1, rows, 1), lambda ei, ni: (ei, 0, 0))],
        out_specs=pl.BlockSpec((1, rows, tn), lambda ei, ni: (ei, 0, ni)),
        out_shape=jax.ShapeDtypeStruct((e, rows, d), BF16),
        compiler_params=_params(("arbitrary", "arbitrary"), 48),
        name="expert_down",
    )(hmid, w_down, valc)


COMBINE_EG = 4


COMBINE_TN = 512


def _combine_body(slot_ref, y_ref, x_ref, g_ref, o_ref, oht_ref, *, cap):
    eg = pl.program_id(2)
    ts, d = o_ref.shape
    ci = lax.broadcasted_iota(jnp.int32, (cap, ts), 0)
    for k in range(COMBINE_EG):
        srow = slot_ref[0, pl.ds(eg * COMBINE_EG + k, 1), :]
        oht_ref[:, k * cap:(k + 1) * cap] = jnp.where(srow == ci, 1.0, 0.0).T.astype(BF16)

    @pl.when(eg == 0)
    def _():
        o_ref[...] = x_ref[...]

    for c in range(d // COMBINE_TN):
        cols = slice(c * COMBINE_TN, (c + 1) * COMBINE_TN)
        rows = y_ref[:, :, cols].reshape(COMBINE_EG * cap, COMBINE_TN)
        o_ref[:, cols] += jnp.dot(oht_ref[...], rows, preferred_element_type=F32)

    @pl.when(eg == pl.num_programs(2) - 1)
    def _():
        o_ref[...] = _rmsnorm_rows(o_ref[...], g_ref[...])


def expert_combine(slot, y, x2d, g, cap, ts=512):
    b, e, s = slot.shape
    d = x2d.shape[-1]
    nt = s // ts
    assert e // COMBINE_EG >= 2
    return pl.pallas_call(
        functools.partial(_combine_body, cap=cap),
        grid=(b, nt, e // COMBINE_EG),
        in_specs=[pl.BlockSpec((1, e, ts), lambda bi, si, gi: (bi, 0, si)),
                  pl.BlockSpec((COMBINE_EG, cap, d), lambda bi, si, gi: (gi, bi, 0)),
                  pl.BlockSpec((ts, d), lambda bi, si, gi: (bi * nt + si, 0)),
                  pl.BlockSpec((1, d), lambda bi, si, gi: (0, 0))],
        out_specs=pl.BlockSpec((ts, d), lambda bi, si, gi: (bi * nt + si, 0)),
        out_shape=jax.ShapeDtypeStruct(x2d.shape, F32),
        scratch_shapes=[pltpu.VMEM((ts, COMBINE_EG * cap), BF16)],
        compiler_params=_params(("arbitrary", "arbitrary", "arbitrary"), 60),
        name="expert_combine",
    )(slot, y, x2d, g.reshape(1, d))


def _rotary_tables(seq):
    half = ROT_DIM // 2
    inv = ROPE_THETA ** (-jnp.arange(half, dtype=F32) * 2.0 / ROT_DIM)
    ang = jnp.arange(seq).astype(F32)[:, None] * inv[None, :]
    cos, sin = jnp.cos(ang), jnp.sin(ang)
    ones = jnp.ones((seq, HEAD_DIM - ROT_DIM), F32)
    zeros = jnp.zeros((seq, HEAD_DIM - ROT_DIM), F32)
    zh = jnp.zeros((seq, half), F32)
    c = jnp.concatenate([cos, cos, ones], axis=1)
    s1 = jnp.concatenate([-sin, zh, zeros], axis=1)
    s2 = jnp.concatenate([zh, sin, zeros], axis=1)
    return c, s1, s2


def kernel(x, mem, norm_mix, w_in, b_gate, sink, rpb, w_branch_a, w_branch_b, w_out,
           norm_cross, norm_mem, wq_x, wk_x, wv_x, wo_x, norm_ffn, w_router,
           w_gate, w_up, w_down, norm_final):
    batch, seq, d = x.shape
    mem_len = mem.shape[1]
    m = batch * seq
    assert norm_mix.shape[0] == 1, "final RMSNorm is fused into the single layer's last kernel"
    cap = EC_CAPACITY * seq // N_EXPERTS
    bm, bn = 1024, 512
    sb = seq // bm
    x0 = x.reshape(m, d)

    h = rmsnorm(x0, norm_mix[0], BF16)
    rot = _rotary_tables(seq)
    rot_specs = [pl.BlockSpec((bm, HEAD_DIM), lambda j, i: (i % sb, 0))] * 3
    qk = matmul(h, w_in[0], col_off=0, n_cols=QA_W + KVA_W, bm=bm, bn=bn, out_dtype=BF16,
                epilogue=_ep_rotary, extras=rot, extra_specs=rot_specs, name="in_proj_rotary")
    vqkv = matmul(h, w_in[0], col_off=QA_W + KVA_W, n_cols=KVA_W + 3 * QB_W, bm=bm, bn=bn,
                  out_dtype=BF16, name="in_proj_plain")
    g_off = QA_W + 2 * KVA_W + 3 * QB_W
    gates = matmul(h, w_in[0], col_off=g_off, n_cols=2 * d, bm=bm, bn=bn, out_dtype=BF16,
                   epilogue=_ep_sigmoid, extras=(b_gate[0].reshape(1, 2 * d),),
                   extra_specs=[pl.BlockSpec((1, bn), lambda j, i: (0, j))], name="in_proj_gates")
    oa = window_attention(qk, vqkv, sink[0], batch, seq)
    ob = neighbourhood_attention(vqkv, _nbr_bias_table(rpb[0], seq), batch, seq)
    merged = branch_merge(oa, ob, w_branch_a[0], w_branch_b[0], gates)
    res_spec = [pl.BlockSpec((bm, bn), lambda j, i: (i, j))]
    x1 = matmul(merged, w_out[0], col_off=0, n_cols=d, bm=bm, bn=bn, out_dtype=F32,
                epilogue=_ep_residual, extras=(x0,), extra_specs=res_spec, name="out_proj")

    mn = rmsnorm(mem.reshape(batch * mem_len, d), norm_mem[0], BF16)
    kx = matmul(mn, wk_x[0], col_off=0, n_cols=X_W, bm=bm, bn=bn, out_dtype=BF16, name="xattn_k")
    vx = matmul(mn, wv_x[0], col_off=0, n_cols=X_W, bm=bm, bn=bn, out_dtype=BF16, name="xattn_v")
    x2, h3p, aff_t, aff_tm = cross_attention_block(
        x1, norm_cross[0], wq_x[0], kx, vx, wo_x[0], norm_ffn[0], w_router[0], batch, seq, mem_len)

    slot = expert_topk(aff_t, aff_tm, cap)
    idx, valc = slot_index(slot, aff_t, cap)
    hmid = expert_up(idx.reshape(-1), h3p, w_gate[0], w_up[0], batch * cap)
    y = expert_down(hmid, w_down[0], valc)
    out = expert_combine(slot, y, x2, norm_final, cap)
    return out.reshape(batch, seq, d)
```

```python
import functools

import jax
import jax.numpy as jnp
from jax import lax
from jax.experimental import pallas as pl
from jax.experimental.pallas import tpu as pltpu

F32 = jnp.float32
BF16 = jnp.bfloat16

HEAD_DIM = 128
A_HEADS = 16
A_KV_HEADS = 4
A_GROUP = A_HEADS // A_KV_HEADS
WINDOW = 128
A_BLOCK = 128
ROT_DIM = HEAD_DIM // 4
ROPE_THETA = 500000.0
B_HEADS = 16
GRID_W = 64
NA_KH_MAX = 8
NA_KW = 16
X_HEADS = 4
N_EXPERTS = 16
EC_CAPACITY = 2
EPS = 1e-6
NEG = -1e30
LANES = 128
MIB = 1024 * 1024

QA_W = A_HEADS * HEAD_DIM
KVA_W = A_KV_HEADS * HEAD_DIM
QB_W = B_HEADS * HEAD_DIM
X_W = X_HEADS * HEAD_DIM

_NT = (((1,), (1,)), ((), ()))
_TN = (((0,), (0,)), ((), ()))


def _params(semantics, vmem_mib):
    return pltpu.CompilerParams(dimension_semantics=semantics,
                                vmem_limit_bytes=vmem_mib * MIB)


def _cast_rows(src_ref, dst_ref, rows, chunk=256):
    def body(k, carry):
        r = pl.multiple_of(k * chunk, chunk)
        dst_ref[pl.ds(r, chunk), :] = src_ref[pl.ds(r, chunk), :].astype(dst_ref.dtype)
        return carry
    lax.fori_loop(0, rows // chunk, body, 0)


def _rmsnorm_rows(x, g):
    ms = jnp.mean(x * x, axis=-1, keepdims=True)
    return x * lax.rsqrt(ms + EPS) * g


def _rmsnorm_body(x_ref, g_ref, o_ref):
    o_ref[...] = _rmsnorm_rows(x_ref[...], g_ref[...]).astype(o_ref.dtype)


def rmsnorm(x2d, g, out_dtype, bm=256):
    m, d = x2d.shape
    return pl.pallas_call(
        _rmsnorm_body,
        grid=(m // bm,),
        in_specs=[pl.BlockSpec((bm, d), lambda i: (i, 0)),
                  pl.BlockSpec((1, d), lambda i: (0, 0))],
        out_specs=pl.BlockSpec((bm, d), lambda i: (i, 0)),
        out_shape=jax.ShapeDtypeStruct((m, d), out_dtype),
        compiler_params=_params(("arbitrary",), 40),
        name="rmsnorm",
    )(x2d, g.reshape(1, d))


def _ep_store(acc, o_ref):
    o_ref[...] = acc.astype(o_ref.dtype)


def _ep_residual(acc, o_ref, r_ref):
    o_ref[...] = (r_ref[...] + acc).astype(o_ref.dtype)


def _ep_sigmoid(acc, o_ref, b_ref):
    o_ref[...] = jax.nn.sigmoid(acc + b_ref[...]).astype(o_ref.dtype)


def _ep_rotary(acc, o_ref, c_ref, s1_ref, s2_ref):
    c, s1, s2 = c_ref[...], s1_ref[...], s2_ref[...]
    half = ROT_DIM // 2
    for h in range(acc.shape[1] // HEAD_DIM):
        a = acc[:, h * HEAD_DIM:(h + 1) * HEAD_DIM]
        r = a * c + pltpu.roll(a, HEAD_DIM - half, 1) * s1 + pltpu.roll(a, half, 1) * s2
        o_ref[:, h * HEAD_DIM:(h + 1) * HEAD_DIM] = r.astype(o_ref.dtype)


def _mm_body(*refs, n_extra, epilogue, k_rows):
    a_ref, w_ref = refs[0], refs[1]
    extra = refs[2:2 + n_extra]
    o_ref = refs[2 + n_extra]
    wb_ref = refs[3 + n_extra]

    @pl.when(pl.program_id(1) == 0)
    def _():
        _cast_rows(w_ref, wb_ref, k_rows)

    acc = jnp.dot(a_ref[...], wb_ref[...], preferred_element_type=F32)
    epilogue(acc, o_ref, *extra)


def matmul(a, w, *, col_off, n_cols, bm, bn, out_dtype, epilogue=_ep_store,
           extras=(), extra_specs=(), vmem_mib=56, name="matmul"):
    m, k = a.shape
    off = col_off // bn
    assert col_off % bn == 0 and n_cols % bn == 0 and m % bm == 0
    body = functools.partial(_mm_body, n_extra=len(extras), epilogue=epilogue, k_rows=k)
    return pl.pallas_call(
        body,
        grid=(n_cols // bn, m // bm),
        in_specs=[pl.BlockSpec((bm, k), lambda j, i: (i, 0)),
                  pl.BlockSpec((k, bn), lambda j, i: (0, j + off))] + list(extra_specs),
        out_specs=pl.BlockSpec((bm, bn), lambda j, i: (i, j)),
        out_shape=jax.ShapeDtypeStruct((m, n_cols), out_dtype),
        scratch_shapes=[pltpu.VMEM((k, bn), BF16)],
        compiler_params=_params(("arbitrary", "arbitrary"), vmem_mib),
        name=name,
    )(a, w, *extras)


def _mm_rows_body(*refs, n_extra, epilogue):
    a_ref, w_ref = refs[0], refs[1]
    extra = refs[2:2 + n_extra]
    o_ref = refs[2 + n_extra]
    acc = jnp.dot(a_ref[...], w_ref[...].astype(BF16), preferred_element_type=F32)
    epilogue(acc, o_ref, *extra)


def matmul_rows(a, w, *, col_off, n_cols, bm, bn, out_dtype, epilogue=_ep_store,
                extras=(), extra_specs=(), vmem_mib=56, name="matmul_rows"):
    m, k = a.shape
    off = col_off // bn
    assert col_off % bn == 0 and n_cols % bn == 0 and m % bm == 0
    body = functools.partial(_mm_rows_body, n_extra=len(extras), epilogue=epilogue)
    return pl.pallas_call(
        body,
        grid=(m // bm, n_cols // bn),
        in_specs=[pl.BlockSpec((bm, k), lambda i, j: (i, 0)),
                  pl.BlockSpec((k, bn), lambda i, j: (0, j + off))] + list(extra_specs),
        out_specs=pl.BlockSpec((bm, bn), lambda i, j: (i, j)),
        out_shape=jax.ShapeDtypeStruct((m, n_cols), out_dtype),
        compiler_params=_params(("arbitrary", "arbitrary"), vmem_mib),
        name=name,
    )(a, w, *extras)


def _softmax_parts(parts, extra_col=None):
    m = parts[0].max(axis=1, keepdims=True)
    for p in parts[1:]:
        m = jnp.maximum(m, p.max(axis=1, keepdims=True))
    if extra_col is not None:
        m = jnp.maximum(m, extra_col)
    es = [jnp.exp(p - m) for p in parts]
    den = es[0].sum(axis=1, keepdims=True)
    for e in es[1:]:
        den = den + e.sum(axis=1, keepdims=True)
    if extra_col is not None:
        den = den + jnp.exp(extra_col - m)
    inv = 1.0 / den
    return [e * inv for e in es]


WIN_UNROLL = 4


def _win_body(sink_ref, q_ref, k_ref, v_ref, o_ref, *, seq):
    kv = pl.program_id(1)
    nb = seq // A_BLOCK
    scale = HEAD_DIM ** -0.5
    rows = A_GROUP * A_BLOCK
    qi = lax.broadcasted_iota(jnp.int32, (rows, A_BLOCK), 0) % A_BLOCK
    ci = lax.broadcasted_iota(jnp.int32, (rows, A_BLOCK), 1)
    sink_b = jnp.concatenate(
        [jnp.full((A_BLOCK, HEAD_DIM), sink_ref[kv * A_GROUP + g], F32) for g in range(A_GROUP)], axis=0)

    def scores(n):
        r0 = pl.multiple_of(n * A_BLOCK, A_BLOCK)
        rp = pl.multiple_of(jnp.maximum(n - 1, 0) * A_BLOCK, A_BLOCK)
        rn = pl.multiple_of(jnp.minimum(n + 1, nb - 1) * A_BLOCK, A_BLOCK)
        off_p = jnp.where(n > 0, 0, 2 * A_BLOCK)
        off_n = jnp.where(n < nb - 1, 0, 2 * A_BLOCK)
        q = jnp.concatenate(
            [q_ref[pl.ds(r0, A_BLOCK), g * HEAD_DIM:(g + 1) * HEAD_DIM] for g in range(A_GROUP)],
            axis=0)
        sp = lax.dot_general(q, k_ref[pl.ds(rp, A_BLOCK), :], _NT, preferred_element_type=F32) * scale
        sc = lax.dot_general(q, k_ref[pl.ds(r0, A_BLOCK), :], _NT, preferred_element_type=F32) * scale
        sn = lax.dot_general(q, k_ref[pl.ds(rn, A_BLOCK), :], _NT, preferred_element_type=F32) * scale
        sp = jnp.where(ci >= qi + off_p, sp, NEG)
        sn = jnp.where(ci <= qi - off_n, sn, NEG)
        return (rp, r0, rn), (sp, sc, sn)

    def exps(parts):
        m = jnp.maximum(jnp.maximum(parts[0], parts[1]), parts[2]).max(axis=1, keepdims=True)
        m = jnp.maximum(jnp.broadcast_to(m, sink_b.shape), sink_b)
        es = [jnp.exp(p - m) for p in parts]
        den = (es[0] + es[1] + es[2]).sum(axis=1, keepdims=True)
        den = jnp.broadcast_to(den, sink_b.shape) + jnp.exp(sink_b - m)
        return [e.astype(BF16) for e in es], 1.0 / den

    def body(it, carry):
        blocks = [scores(it * WIN_UNROLL + u) for u in range(WIN_UNROLL)]
        probs = [exps(parts) for _, parts in blocks]
        for (rows_kv, _), (es, inv) in zip(blocks, probs):
            o = jnp.dot(es[0], v_ref[pl.ds(rows_kv[0], A_BLOCK), :], preferred_element_type=F32)
            o = o + jnp.dot(es[1], v_ref[pl.ds(rows_kv[1], A_BLOCK), :], preferred_element_type=F32)
            o = o + jnp.dot(es[2], v_ref[pl.ds(rows_kv[2], A_BLOCK), :], preferred_element_type=F32)
            o = o * inv
            for g in range(A_GROUP):
                o_ref[pl.ds(rows_kv[1], A_BLOCK), g * HEAD_DIM:(g + 1) * HEAD_DIM] = (
                    o[g * A_BLOCK:(g + 1) * A_BLOCK].astype(o_ref.dtype))
        return carry

    lax.fori_loop(0, nb // WIN_UNROLL, body, 0)


def window_attention(qk, vqkv, sink, batch, seq):
    gw = A_GROUP * HEAD_DIM
    k_blk0 = QA_W // HEAD_DIM
    return pl.pallas_call(
        functools.partial(_win_body, seq=seq),
        grid=(batch, A_KV_HEADS),
        in_specs=[pl.BlockSpec(memory_space=pltpu.SMEM),
                  pl.BlockSpec((seq, gw), lambda b, h: (b, h)),
                  pl.BlockSpec((seq, HEAD_DIM), lambda b, h: (b, k_blk0 + h)),
                  pl.BlockSpec((seq, HEAD_DIM), lambda b, h: (b, h))],
        out_specs=pl.BlockSpec((seq, gw), lambda b, h: (b, h)),
        out_shape=jax.ShapeDtypeStruct((batch * seq, QA_W), BF16),
        compiler_params=_params(("arbitrary", "arbitrary"), 32),
        name="window_attention",
    )(sink, qk, qk, vqkv)


NBR_HG = 4
NBR_ROWS = 4


def _nbr_body(q_ref, k_ref, v_ref, bias_ref, o_ref, *, seq):
    rows = seq // GRID_W
    kh = min(NA_KH_MAX, rows)
    strip = kh * GRID_W
    scale = HEAD_DIM ** -0.5

    def body(it, carry):
        units = []
        for rr in range(NBR_ROWS):
            r = it * NBR_ROWS + rr
            rs = jnp.clip(r - kh // 2, 0, rows - kh)
            q0 = pl.multiple_of(r * GRID_W, GRID_W)
            k0 = pl.multiple_of(rs * GRID_W, GRID_W)
            for h in range(NBR_HG):
                units.append((q0, k0, r - rs, h, slice(h * HEAD_DIM, (h + 1) * HEAD_DIM)))
        ss = [lax.dot_general(q_ref[pl.ds(q0, GRID_W), cols], k_ref[pl.ds(k0, strip), cols], _NT,
                              preferred_element_type=F32) * scale + bias_ref[h, var]
              for q0, k0, var, h, cols in units]
        ps = []
        for s in ss:
            e = jnp.exp(s - s.max(axis=1, keepdims=True))
            ps.append((e.astype(BF16), 1.0 / e.sum(axis=1, keepdims=True)))
        for (q0, k0, var, h, cols), (e, inv) in zip(units, ps):
            o = jnp.dot(e, v_ref[pl.ds(k0, strip), cols], preferred_element_type=F32) * inv
            o_ref[pl.ds(q0, GRID_W), cols] = o.astype(o_ref.dtype)
        return carry

    lax.fori_loop(0, rows // NBR_ROWS, body, 0)


def neighbourhood_attention(vqkv, bias_tbl, batch, seq):
    gw = NBR_HG * HEAD_DIM
    q0, k0, v0 = KVA_W // gw, (KVA_W + QB_W) // gw, (KVA_W + 2 * QB_W) // gw
    kh = bias_tbl.shape[1]
    return pl.pallas_call(
        functools.partial(_nbr_body, seq=seq),
        grid=(B_HEADS // NBR_HG, batch),
        in_specs=[pl.BlockSpec((seq, gw), lambda g, b: (b, q0 + g)),
                  pl.BlockSpec((seq, gw), lambda g, b: (b, k0 + g)),
                  pl.BlockSpec((seq, gw), lambda g, b: (b, v0 + g)),
                  pl.BlockSpec((NBR_HG, kh, GRID_W, kh * GRID_W), lambda g, b: (g, 0, 0, 0))],
        out_specs=pl.BlockSpec((seq, gw), lambda g, b: (b, g)),
        out_shape=jax.ShapeDtypeStruct((batch * seq, QB_W), BF16),
        compiler_params=_params(("arbitrary", "arbitrary"), 40),
        name="neighbourhood_attention",
    )(vqkv, vqkv, vqkv, bias_tbl)


def _bias_table_body(rpb_ref, o_ref, *, kh):
    h = pl.program_id(0)
    n_dr, n_dc = 2 * NA_KH_MAX - 1, 2 * NA_KW - 1
    c = lax.broadcasted_iota(jnp.int32, (GRID_W, LANES), 0)
    lane = lax.broadcasted_iota(jnp.int32, (GRID_W, LANES), 1)
    kc = lane % GRID_W
    diff = jnp.clip(kc - c + NA_KW - 1, 0, n_dc - 1)
    cs = jnp.clip(c - NA_KW // 2, 0, GRID_W - NA_KW)
    col_ok = (kc >= cs) & (kc < cs + NA_KW)
    slabs = []
    for dr in range(n_dr):
        acc = jnp.zeros((GRID_W, LANES), F32)
        for d in range(n_dc):
            acc = jnp.where(diff == d, rpb_ref[(h * n_dr + dr) * n_dc + d], acc)
        slabs.append(jnp.where(col_ok, acc, NEG))
    left = lane < GRID_W
    for var in range(kh):
        for jp in range(kh * GRID_W // LANES):
            dr0 = 2 * jp - var + NA_KH_MAX - 1
            o_ref[0, var, :, jp * LANES:(jp + 1) * LANES] = jnp.where(left, slabs[dr0], slabs[dr0 + 1])


def _nbr_bias_table(rpb, seq):
    rows = seq // GRID_W
    kh = min(NA_KH_MAX, rows)
    heads = rpb.shape[0]
    assert kh == NA_KH_MAX and 2 * GRID_W == LANES
    return pl.pallas_call(
        functools.partial(_bias_table_body, kh=kh),
        grid=(heads,),
        in_specs=[pl.BlockSpec(memory_space=pltpu.SMEM)],
        out_specs=pl.BlockSpec((1, kh, GRID_W, kh * GRID_W), lambda h: (h, 0, 0, 0)),
        out_shape=jax.ShapeDtypeStruct((heads, kh, GRID_W, kh * GRID_W), F32),
        compiler_params=_params(("arbitrary",), 16),
        name="nbr_bias_table",
    )(rpb.astype(F32).reshape(-1))


def _merge_body(oa_ref, ob_ref, wa_ref, wb_ref, g0_ref, g1_ref, o_ref, wa_s, wb_s, *, k_rows):
    @pl.when(pl.program_id(1) == 0)
    def _():
        _cast_rows(wa_ref, wa_s, k_rows)
        _cast_rows(wb_ref, wb_s, k_rows)

    ya = jnp.dot(oa_ref[...], wa_s[...], preferred_element_type=F32)
    yb = jnp.dot(ob_ref[...], wb_s[...], preferred_element_type=F32)
    o_ref[...] = (g0_ref[...].astype(F32) * ya + g1_ref[...].astype(F32) * yb).astype(o_ref.dtype)


def branch_merge(oa, ob, wa, wb, gates, bm=1024, bn=512):
    m, k = oa.shape
    n = wa.shape[1]
    g1_off = n // bn
    return pl.pallas_call(
        functools.partial(_merge_body, k_rows=k),
        grid=(n // bn, m // bm),
        in_specs=[pl.BlockSpec((bm, k), lambda j, i: (i, 0)),
                  pl.BlockSpec((bm, k), lambda j, i: (i, 0)),
                  pl.BlockSpec((k, bn), lambda j, i: (0, j)),
                  pl.BlockSpec((k, bn), lambda j, i: (0, j)),
                  pl.BlockSpec((bm, bn), lambda j, i: (i, j)),
                  pl.BlockSpec((bm, bn), lambda j, i: (i, j + g1_off))],
        out_specs=pl.BlockSpec((bm, bn), lambda j, i: (i, j)),
        out_shape=jax.ShapeDtypeStruct((m, n), BF16),
        scratch_shapes=[pltpu.VMEM((k, bn), BF16), pltpu.VMEM((k, bn), BF16)],
        compiler_params=_params(("arbitrary", "arbitrary"), 56),
        name="branch_merge",
    )(oa, ob, wa, wb, gates, gates)


def _cast_body(x_ref, o_ref):
    o_ref[...] = x_ref[...].astype(o_ref.dtype)


def cast_bf16(w):
    r, c = w.shape
    return pl.pallas_call(
        _cast_body,
        grid=(1,),
        in_specs=[pl.BlockSpec((r, c), lambda i: (0, 0))],
        out_specs=pl.BlockSpec((r, c), lambda i: (0, 0)),
        out_shape=jax.ShapeDtypeStruct((r, c), BF16),
        compiler_params=_params(("arbitrary",), 40),
        name="cast_bf16",
    )(w)


def _router_probs(hn, wr_ref):
    hi = hn.astype(BF16)
    lo = (hn - hi.astype(F32)).astype(BF16)
    l_hi = jnp.dot(hi, wr_ref[...], preferred_element_type=F32)
    l_lo = jnp.dot(lo, wr_ref[...], preferred_element_type=F32)
    logits = l_hi + pltpu.roll(l_hi, LANES - N_EXPERTS, 1) + l_lo
    lane = lax.broadcasted_iota(jnp.int32, logits.shape, 1)
    logits = jnp.where(lane < N_EXPERTS, logits, NEG)
    (aff,) = _softmax_parts([logits])
    return aff


def _xblock_body(x_ref, gc_ref, wq_ref, k_ref, v_ref, wo_ref, gf_ref, wr_ref,
                 x2_ref, h3_ref, aff_ref, atm_ref):
    scale = HEAD_DIM ** -0.5
    x = x_ref[...]
    h2 = _rmsnorm_rows(x, gc_ref[...]).astype(BF16)
    q = jnp.dot(h2, wq_ref[...], preferred_element_type=F32).astype(BF16)
    heads = []
    for h in range(X_HEADS):
        cols = slice(h * HEAD_DIM, (h + 1) * HEAD_DIM)
        s = lax.dot_general(q[:, cols], k_ref[:, cols], _NT, preferred_element_type=F32) * scale
        (p,) = _softmax_parts([s])
        heads.append(jnp.dot(p.astype(BF16), v_ref[:, cols], preferred_element_type=F32).astype(BF16))
    o = jnp.concatenate(heads, axis=1)
    x2 = x + jnp.dot(o, wo_ref[...], preferred_element_type=F32)
    x2_ref[...] = x2
    hn = _rmsnorm_rows(x2, gf_ref[...])
    half = hn.shape[1] // 2
    packed = pltpu.pack_elementwise([hn[:, :half], hn[:, half:]], packed_dtype=BF16)
    h3_ref[...] = pltpu.bitcast(packed, jnp.uint32)
    aff = _router_probs(hn, wr_ref)
    atm_ref[...] = aff
    aff_ref[0] = aff.T[:N_EXPERTS, :]


def cross_attention_block(x1, g_cross, wq, kx, vx, wo, g_ffn, w_router, batch, seq, mem_len, bm=256):
    m, d = x1.shape
    nt = seq // bm
    w_hi = w_router.astype(BF16)
    w_lo = (w_router - w_hi.astype(F32)).astype(BF16)
    wr = jnp.concatenate([w_hi, w_lo, jnp.zeros((d, LANES - 2 * N_EXPERTS), BF16)], axis=1)
    const = lambda i: (0, 0)
    return pl.pallas_call(
        _xblock_body,
        grid=(m // bm,),
        in_specs=[pl.BlockSpec((bm, d), lambda i: (i, 0)),
                  pl.BlockSpec((1, d), const),
                  pl.BlockSpec((d, X_W), const),
                  pl.BlockSpec((mem_len, X_W), lambda i: (i // nt, 0)),
                  pl.BlockSpec((mem_len, X_W), lambda i: (i // nt, 0)),
                  pl.BlockSpec((X_W, d), const),
                  pl.BlockSpec((1, d), const),
                  pl.BlockSpec((d, LANES), const)],
        out_specs=[pl.BlockSpec((bm, d), lambda i: (i, 0)),
                   pl.BlockSpec((bm, d // 2), lambda i: (i, 0)),
                   pl.BlockSpec((1, N_EXPERTS, bm), lambda i: (i // nt, 0, i % nt)),
                   pl.BlockSpec((bm, LANES), lambda i: (i, 0))],
        out_shape=[jax.ShapeDtypeStruct((m, d), F32),
                   jax.ShapeDtypeStruct((m, d // 2), jnp.uint32),
                   jax.ShapeDtypeStruct((batch, N_EXPERTS, seq), F32),
                   jax.ShapeDtypeStruct((m, LANES), F32)],
        compiler_params=_params(("arbitrary",), 48),
        name="cross_attention_block",
    )(x1, g_cross.reshape(1, d), cast_bf16(wq), kx, vx, cast_bf16(wo), g_ffn.reshape(1, d), wr)


CUM_CHUNK = 256


def _excl_cumsum_lanes(x01, tri):
    n = x01.shape[1]
    carry = jnp.zeros((x01.shape[0], 1), F32)
    out = []
    for c in range(n // CUM_CHUNK):
        xc = x01[:, c * CUM_CHUNK:(c + 1) * CUM_CHUNK]
        out.append(jnp.dot(xc.astype(BF16), tri, preferred_element_type=F32) + carry)
        carry = carry + xc.sum(axis=1, keepdims=True)
    return jnp.concatenate(out, axis=1)


RANK_CHUNK = 256
SUBLANES = 8


def _topk_body(aff_ref, atm_ref, slot_ref, cgt_ref, *, cap):
    n_exp, s = aff_ref.shape[1], aff_ref.shape[2]
    capf = jnp.float32(cap)
    for e in range(n_exp):
        a_row = aff_ref[0, e:e + 1, :]

        def chunk(c, acc, a_row=a_row, e=e):
            r = pl.multiple_of(c * RANK_CHUNK, RANK_CHUNK)
            col = atm_ref[pl.ds(r, RANK_CHUNK), e:e + 1]
            above = jnp.where(col > a_row, 1.0, 0.0)
            return acc + above.reshape(RANK_CHUNK // SUBLANES, SUBLANES, s).sum(axis=0)

        acc = lax.fori_loop(0, s // RANK_CHUNK, chunk, jnp.zeros((SUBLANES, s), F32))
        cgt_ref[e:e + 1, :] = acc.sum(axis=0, keepdims=True)
    cgt = cgt_ref[...]
    ri = lax.broadcasted_iota(jnp.int32, (CUM_CHUNK, CUM_CHUNK), 0)
    cj = lax.broadcasted_iota(jnp.int32, (CUM_CHUNK, CUM_CHUNK), 1)
    tri = jnp.where(ri < cj, 1.0, 0.0).astype(BF16)
    cand = jnp.where(cgt < capf, 1.0, 0.0)
    extra = cand.sum(axis=1, keepdims=True) - capf
    g_last = jnp.where(cand > 0.5, cgt, -1.0).max(axis=1, keepdims=True)
    tie = jnp.where(cgt == g_last, cand, 0.0)
    keep = tie.sum(axis=1, keepdims=True) - extra
    sel = cand - jnp.where(_excl_cumsum_lanes(tie, tri) >= keep, tie, 0.0)
    pos = _excl_cumsum_lanes(sel, tri)
    slot_ref[0] = jnp.where(sel > 0.5, pos, -1.0).astype(jnp.int32)


def expert_topk(aff_t, aff_tm, cap):
    b, e, s = aff_t.shape
    return pl.pallas_call(
        functools.partial(_topk_body, cap=cap),
        grid=(b,),
        in_specs=[pl.BlockSpec((1, e, s), lambda i: (i, 0, 0)),
                  pl.BlockSpec((s, LANES), lambda i: (i, 0))],
        out_specs=pl.BlockSpec((1, e, s), lambda i: (i, 0, 0)),
        out_shape=jax.ShapeDtypeStruct((b, e, s), jnp.int32),
        scratch_shapes=[pltpu.VMEM((e, s), F32)],
        compiler_params=_params(("arbitrary",), 32),
        name="expert_topk",
    )(aff_t, aff_tm)


def _slot_index_body(slot_ref, aff_ref, idx_ref, val_ref, *, cap):
    b = pl.program_id(0)
    n_exp, s = slot_ref.shape[1], slot_ref.shape[2]
    ci = lax.broadcasted_iota(jnp.int32, (cap, s), 0)
    tok = lax.broadcasted_iota(jnp.int32, (cap, s), 1).astype(F32)
    base = (b * s).astype(F32)
    for e in range(n_exp):
        hit = slot_ref[0, e:e + 1, :] == ci
        idx = jnp.where(hit, tok, 0.0).sum(axis=1, keepdims=True) + base
        idx_ref[e] = idx.astype(jnp.int32)
        val_ref[e] = jnp.where(hit, aff_ref[0, e:e + 1, :], 0.0).sum(axis=1, keepdims=True)


def slot_index(slot, aff_t, cap):
    b, e, s = slot.shape
    return pl.pallas_call(
        functools.partial(_slot_index_body, cap=cap),
        grid=(b,),
        in_specs=[pl.BlockSpec((1, e, s), lambda bi: (bi, 0, 0)),
                  pl.BlockSpec((1, e, s), lambda bi: (bi, 0, 0))],
        out_specs=[pl.BlockSpec((e, cap, 1), lambda bi: (0, bi, 0)),
                   pl.BlockSpec((e, cap, 1), lambda bi: (0, bi, 0))],
        out_shape=[jax.ShapeDtypeStruct((e, b * cap, 1), jnp.int32),
                   jax.ShapeDtypeStruct((e, b * cap, 1), F32)],
        compiler_params=_params(("arbitrary",), 32),
        name="slot_index",
    )(slot, aff_t)


UNPACK_ROWS = 128


def _row_copy(idx_ref, hp_ref, gbuf, sem, expert, rows, row):
    tok = idx_ref[expert * rows + row]
    return pltpu.make_async_copy(hp_ref.at[pl.ds(tok, 1)], gbuf.at[pl.ds(row, 1)], sem)


def _expert_up_body(idx_ref, hp_ref, wg_ref, wu_ref, o_ref, gbuf, xbf, sem, *, rows, per_step):
    e, f = pl.program_id(0), pl.program_id(1)
    n_e, n_f = pl.num_programs(0), pl.num_programs(1)
    half = gbuf.shape[1]

    def wait_all_rows():
        pltpu.make_async_copy(hp_ref.at[pl.ds(0, rows)], gbuf, sem).wait()

    @pl.when(jnp.logical_and(e == 0, f == 0))
    def _():
        def body(r, carry):
            _row_copy(idx_ref, hp_ref, gbuf, sem, 0, rows, r).start()
            return carry
        lax.fori_loop(0, rows, body, 0)

    @pl.when(f == 0)
    def _():
        wait_all_rows()

        def unpack(k, carry):
            r = pl.multiple_of(k * UNPACK_ROWS, UNPACK_ROWS)
            w = gbuf[pl.ds(r, UNPACK_ROWS), :]
            lo = pltpu.unpack_elementwise(w, index=0, packed_dtype=BF16, unpacked_dtype=F32)
            hi = pltpu.unpack_elementwise(w, index=1, packed_dtype=BF16, unpacked_dtype=F32)
            xbf[pl.ds(r, UNPACK_ROWS), :half] = lo.astype(BF16)
            xbf[pl.ds(r, UNPACK_ROWS), half:] = hi.astype(BF16)
            return carry
        lax.fori_loop(0, rows // UNPACK_ROWS, unpack, 0)

    nxt = jnp.minimum(e + 1, n_e - 1)
    for r in range(per_step):
        _row_copy(idx_ref, hp_ref, gbuf, sem, nxt, rows, f * per_step + r).start()

    x = xbf[...]
    a = jnp.dot(x, wg_ref[0].astype(BF16), preferred_element_type=F32)
    u = jnp.dot(x, wu_ref[0].astype(BF16), preferred_element_type=F32)
    o_ref[0] = (jax.nn.silu(a) * u).astype(o_ref.dtype)

    @pl.when(jnp.logical_and(e == n_e - 1, f == n_f - 1))
    def _():
        wait_all_rows()


def expert_up(idx, hp, w_gate, w_up, rows, tf=256):
    n_e, d, f = w_gate.shape
    assert hp.shape[1] * 2 == d and rows % (f // tf) == 0
    grid_spec = pltpu.PrefetchScalarGridSpec(
        num_scalar_prefetch=1,
        grid=(n_e, f // tf),
        in_specs=[pl.BlockSpec(memory_space=pl.ANY),
                  pl.BlockSpec((1, d, tf), lambda ei, fi, idx_ref: (ei, 0, fi)),
                  pl.BlockSpec((1, d, tf), lambda ei, fi, idx_ref: (ei, 0, fi))],
        out_specs=pl.BlockSpec((1, rows, tf), lambda ei, fi, idx_ref: (ei, 0, fi)),
        scratch_shapes=[pltpu.VMEM((rows, d // 2), jnp.uint32),
                        pltpu.VMEM((rows, d), BF16),
                        pltpu.SemaphoreType.DMA(())],
    )
    return pl.pallas_call(
        functools.partial(_expert_up_body, rows=rows, per_step=rows // (f // tf)),
        grid_spec=grid_spec,
        out_shape=jax.ShapeDtypeStruct((n_e, rows, f), BF16),
        compiler_params=_params(("arbitrary", "arbitrary"), 56),
        name="expert_up",
    )(idx, hp, w_gate, w_up)


def _expert_down_body(h_ref, wd_ref, val_ref, o_ref):
    y = jnp.dot(h_ref[0], wd_ref[0].astype(BF16), preferred_element_type=F32)
    o_ref[0] = (y * val_ref[0]).astype(o_ref.dtype)


def expert_down(hmid, w_down, valc, tn=1024):
    e, rows, f = hmid.shape
    d = w_down.shape[-1]
    return pl.pallas_call(
        _expert_down_body,
        grid=(e, d // tn),
        in_specs=[pl.BlockSpec((1, rows, f), lambda ei, ni: (ei, 0, 0)),
                  pl.BlockSpec((1, f, tn), lambda ei, ni: (ei, 0, ni)),
                  pl.BlockSpec((1, rows, 1), lambda ei, ni: (ei, 0, 0))],
        out_specs=pl.BlockSpec((1, rows, tn), lambda ei, ni: (ei, 0, ni)),
        out_shape=jax.ShapeDtypeStruct((e, rows, d), BF16),
        compiler_params=_params(("arbitrary", "arbitrary"), 48),
        name="expert_down",
    )(hmid, w_down, valc)


COMBINE_EG = 4


COMBINE_TN = 512


def _combine_body(slot_ref, y_ref, x_ref, g_ref, o_ref, oht_ref, *, cap):
    eg = pl.program_id(2)
    ts, d = o_ref.shape
    ci = lax.broadcasted_iota(jnp.int32, (cap, ts), 0)
    for k in range(COMBINE_EG):
        srow = slot_ref[0, pl.ds(eg * COMBINE_EG + k, 1), :]
        oht_ref[:, k * cap:(k + 1) * cap] = jnp.where(srow == ci, 1.0, 0.0).T.astype(BF16)

    @pl.when(eg == 0)
    def _():
        o_ref[...] = x_ref[...]

    for c in range(d // COMBINE_TN):
        cols = slice(c * COMBINE_TN, (c + 1) * COMBINE_TN)
        rows = y_ref[:, :, cols].reshape(COMBINE_EG * cap, COMBINE_TN)
        o_ref[:, cols] += jnp.dot(oht_ref[...], rows, preferred_element_type=F32)

    @pl.when(eg == pl.num_programs(2) - 1)
    def _():
        o_ref[...] = _rmsnorm_rows(o_ref[...], g_ref[...])


def expert_combine(slot, y, x2d, g, cap, ts=512):
    b, e, s = slot.shape
    d = x2d.shape[-1]
    nt = s // ts
    assert e // COMBINE_EG >= 2
    return pl.pallas_call(
        functools.partial(_combine_body, cap=cap),
        grid=(b, nt, e // COMBINE_EG),
        in_specs=[pl.BlockSpec((1, e, ts), lambda bi, si, gi: (bi, 0, si)),
                  pl.BlockSpec((COMBINE_EG, cap, d), lambda bi, si, gi: (gi, bi, 0)),
                  pl.BlockSpec((ts, d), lambda bi, si, gi: (bi * nt + si, 0)),
                  pl.BlockSpec((1, d), lambda bi, si, gi: (0, 0))],
        out_specs=pl.BlockSpec((ts, d), lambda bi, si, gi: (bi * nt + si, 0)),
        out_shape=jax.ShapeDtypeStruct(x2d.shape, F32),
        scratch_shapes=[pltpu.VMEM((ts, COMBINE_EG * cap), BF16)],
        compiler_params=_params(("arbitrary", "arbitrary", "arbitrary"), 60),
        name="expert_combine",
    )(slot, y, x2d, g.reshape(1, d))


def _rotary_tables(seq):
    half = ROT_DIM // 2
    inv = ROPE_THETA ** (-jnp.arange(half, dtype=F32) * 2.0 / ROT_DIM)
    ang = jnp.arange(seq).astype(F32)[:, None] * inv[None, :]
    cos, sin = jnp.cos(ang), jnp.sin(ang)
    ones = jnp.ones((seq, HEAD_DIM - ROT_DIM), F32)
    zeros = jnp.zeros((seq, HEAD_DIM - ROT_DIM), F32)
    zh = jnp.zeros((seq, half), F32)
    c = jnp.concatenate([cos, cos, ones], axis=1)
    s1 = jnp.concatenate([-sin, zh, zeros], axis=1)
    s2 = jnp.concatenate([zh, sin, zeros], axis=1)
    return c, s1, s2


def kernel(x, mem, norm_mix, w_in, b_gate, sink, rpb, w_branch_a, w_branch_b, w_out,
           norm_cross, norm_mem, wq_x, wk_x, wv_x, wo_x, norm_ffn, w_router,
           w_gate, w_up, w_down, norm_final):
    batch, seq, d = x.shape
    mem_len = mem.shape[1]
    m = batch * seq
    assert norm_mix.shape[0] == 1, "final RMSNorm is fused into the single layer's last kernel"
    cap = EC_CAPACITY * seq // N_EXPERTS
    bm, bn = 1024, 512
    sb = seq // bm
    x0 = x.reshape(m, d)

    h = rmsnorm(x0, norm_mix[0], BF16)
    rot = _rotary_tables(seq)
    rot_specs = [pl.BlockSpec((bm, HEAD_DIM), lambda j, i: (i % sb, 0))] * 3
    qk = matmul(h, w_in[0], col_off=0, n_cols=QA_W + KVA_W, bm=bm, bn=bn, out_dtype=BF16,
                epilogue=_ep_rotary, extras=rot, extra_specs=rot_specs, name="in_proj_rotary")
    vqkv = matmul_rows(h, w_in[0], col_off=QA_W + KVA_W, n_cols=KVA_W + 3 * QB_W, bm=2 * bm, bn=bn // 2,
                       out_dtype=BF16, name="in_proj_plain")
    g_off = QA_W + 2 * KVA_W + 3 * QB_W
    gates = matmul(h, w_in[0], col_off=g_off, n_cols=2 * d, bm=bm, bn=bn, out_dtype=BF16,
                   epilogue=_ep_sigmoid, extras=(b_gate[0].reshape(1, 2 * d),),
                   extra_specs=[pl.BlockSpec((1, bn), lambda j, i: (0, j))], name="in_proj_gates")
    oa = window_attention(qk, vqkv, sink[0], batch, seq)
    ob = neighbourhood_attention(vqkv, _nbr_bias_table(rpb[0], seq), batch, seq)
    merged = branch_merge(oa, ob, w_branch_a[0], w_branch_b[0], gates)
    res_spec = [pl.BlockSpec((bm, bn), lambda j, i: (i, j))]
    x1 = matmul(merged, w_out[0], col_off=0, n_cols=d, bm=bm, bn=bn, out_dtype=F32,
                epilogue=_ep_residual, extras=(x0,), extra_specs=res_spec, name="out_proj")

    mn = rmsnorm(mem.reshape(batch * mem_len, d), norm_mem[0], BF16)
    kx = matmul(mn, wk_x[0], col_off=0, n_cols=X_W, bm=bm, bn=bn, out_dtype=BF16, name="xattn_k")
    vx = matmul(mn, wv_x[0], col_off=0, n_cols=X_W, bm=bm, bn=bn, out_dtype=BF16, name="xattn_v")
    x2, h3p, aff_t, aff_tm = cross_attention_block(
        x1, norm_cross[0], wq_x[0], kx, vx, wo_x[0], norm_ffn[0], w_router[0], batch, seq, mem_len)

    slot = expert_topk(aff_t, aff_tm, cap)
    idx, valc = slot_index(slot, aff_t, cap)
    hmid = expert_up(idx.reshape(-1), h3p, w_gate[0], w_up[0], batch * cap)
    y = expert_down(hmid, w_down[0], valc)
    out = expert_combine(slot, y, x2, norm_final, cap)
    return out.reshape(batch, seq, d)
```

```python
import functools

import jax
import jax.numpy as jnp
from jax import lax
from jax.experimental import pallas as pl
from jax.experimental.pallas import tpu as pltpu

F32 = jnp.float32
BF16 = jnp.bfloat16

HEAD_DIM = 128
A_HEADS = 16
A_KV_HEADS = 4
A_GROUP = A_HEADS // A_KV_HEADS
WINDOW = 128
A_BLOCK = 128
ROT_DIM = HEAD_DIM // 4
ROPE_THETA = 500000.0
B_HEADS = 16
GRID_W = 64
NA_KH_MAX = 8
NA_KW = 16
X_HEADS = 4
N_EXPERTS = 16
EC_CAPACITY = 2
EPS = 1e-6
NEG = -1e30
LANES = 128
MIB = 1024 * 1024

QA_W = A_HEADS * HEAD_DIM
KVA_W = A_KV_HEADS * HEAD_DIM
QB_W = B_HEADS * HEAD_DIM
X_W = X_HEADS * HEAD_DIM

_NT = (((1,), (1,)), ((), ()))
_TN = (((0,), (0,)), ((), ()))


def _params(semantics, vmem_mib):
    return pltpu.CompilerParams(dimension_semantics=semantics,
                                vmem_limit_bytes=vmem_mib * MIB)


def _cast_rows(src_ref, dst_ref, rows, chunk=256):
    def body(k, carry):
        r = pl.multiple_of(k * chunk, chunk)
        dst_ref[pl.ds(r, chunk), :] = src_ref[pl.ds(r, chunk), :].astype(dst_ref.dtype)
        return carry
    lax.fori_loop(0, rows // chunk, body, 0)


def _rmsnorm_rows(x, g):
    ms = jnp.mean(x * x, axis=-1, keepdims=True)
    return x * lax.rsqrt(ms + EPS) * g


def _rmsnorm_body(x_ref, g_ref, o_ref):
    o_ref[...] = _rmsnorm_rows(x_ref[...], g_ref[...]).astype(o_ref.dtype)


def rmsnorm(x2d, g, out_dtype, bm=256):
    m, d = x2d.shape
    return pl.pallas_call(
        _rmsnorm_body,
        grid=(m // bm,),
        in_specs=[pl.BlockSpec((bm, d), lambda i: (i, 0)),
                  pl.BlockSpec((1, d), lambda i: (0, 0))],
        out_specs=pl.BlockSpec((bm, d), lambda i: (i, 0)),
        out_shape=jax.ShapeDtypeStruct((m, d), out_dtype),
        compiler_params=_params(("arbitrary",), 40),
        name="rmsnorm",
    )(x2d, g.reshape(1, d))


def _ep_store(acc, o_ref):
    o_ref[...] = acc.astype(o_ref.dtype)


def _ep_residual(acc, o_ref, r_ref):
    o_ref[...] = (r_ref[...] + acc).astype(o_ref.dtype)


def _ep_sigmoid(acc, o_ref, b_ref):
    o_ref[...] = jax.nn.sigmoid(acc + b_ref[...]).astype(o_ref.dtype)


def _ep_rotary(acc, o_ref, c_ref, s1_ref, s2_ref):
    c, s1, s2 = c_ref[...], s1_ref[...], s2_ref[...]
    half = ROT_DIM // 2
    for h in range(acc.shape[1] // HEAD_DIM):
        a = acc[:, h * HEAD_DIM:(h + 1) * HEAD_DIM]
        r = a * c + pltpu.roll(a, HEAD_DIM - half, 1) * s1 + pltpu.roll(a, half, 1) * s2
        o_ref[:, h * HEAD_DIM:(h + 1) * HEAD_DIM] = r.astype(o_ref.dtype)


def _mm_body(*refs, n_extra, epilogue, k_rows):
    a_ref, w_ref = refs[0], refs[1]
    extra = refs[2:2 + n_extra]
    o_ref = refs[2 + n_extra]
    wb_ref = refs[3 + n_extra]

    @pl.when(pl.program_id(1) == 0)
    def _():
        _cast_rows(w_ref, wb_ref, k_rows)

    acc = jnp.dot(a_ref[...], wb_ref[...], preferred_element_type=F32)
    epilogue(acc, o_ref, *extra)


def matmul(a, w, *, col_off, n_cols, bm, bn, out_dtype, epilogue=_ep_store,
           extras=(), extra_specs=(), vmem_mib=56, name="matmul"):
    m, k = a.shape
    off = col_off // bn
    assert col_off % bn == 0 and n_cols % bn == 0 and m % bm == 0
    body = functools.partial(_mm_body, n_extra=len(extras), epilogue=epilogue, k_rows=k)
    return pl.pallas_call(
        body,
        grid=(n_cols // bn, m // bm),
        in_specs=[pl.BlockSpec((bm, k), lambda j, i: (i, 0)),
                  pl.BlockSpec((k, bn), lambda j, i: (0, j + off))] + list(extra_specs),
        out_specs=pl.BlockSpec((bm, bn), lambda j, i: (i, j)),
        out_shape=jax.ShapeDtypeStruct((m, n_cols), out_dtype),
        scratch_shapes=[pltpu.VMEM((k, bn), BF16)],
        compiler_params=_params(("arbitrary", "arbitrary"), vmem_mib),
        name=name,
    )(a, w, *extras)


def _mm_rows_body(*refs, n_extra, epilogue):
    a_ref, w_ref = refs[0], refs[1]
    extra = refs[2:2 + n_extra]
    o_ref = refs[2 + n_extra]
    acc = jnp.dot(a_ref[...], w_ref[...].astype(BF16), preferred_element_type=F32)
    epilogue(acc, o_ref, *extra)


def matmul_rows(a, w, *, col_off, n_cols, bm, bn, out_dtype, epilogue=_ep_store,
                extras=(), extra_specs=(), vmem_mib=56, name="matmul_rows"):
    m, k = a.shape
    off = col_off // bn
    assert col_off % bn == 0 and n_cols % bn == 0 and m % bm == 0
    body = functools.partial(_mm_rows_body, n_extra=len(extras), epilogue=epilogue)
    return pl.pallas_call(
        body,
        grid=(m // bm, n_cols // bn),
        in_specs=[pl.BlockSpec((bm, k), lambda i, j: (i, 0)),
                  pl.BlockSpec((k, bn), lambda i, j: (0, j + off))] + list(extra_specs),
        out_specs=pl.BlockSpec((bm, bn), lambda i, j: (i, j)),
        out_shape=jax.ShapeDtypeStruct((m, n_cols), out_dtype),
        compiler_params=_params(("arbitrary", "arbitrary"), vmem_mib),
        name=name,
    )(a, w, *extras)


def _softmax_parts(parts, extra_col=None):
    m = parts[0].max(axis=1, keepdims=True)
    for p in parts[1:]:
        m = jnp.maximum(m, p.max(axis=1, keepdims=True))
    if extra_col is not None:
        m = jnp.maximum(m, extra_col)
    es = [jnp.exp(p - m) for p in parts]
    den = es[0].sum(axis=1, keepdims=True)
    for e in es[1:]:
        den = den + e.sum(axis=1, keepdims=True)
    if extra_col is not None:
        den = den + jnp.exp(extra_col - m)
    inv = 1.0 / den
    return [e * inv for e in es]


WIN_UNROLL = 4


def _win_body(sink_ref, q_ref, k_ref, v_ref, o_ref, *, seq):
    kv = pl.program_id(1)
    nb = seq // A_BLOCK
    scale = HEAD_DIM ** -0.5
    rows = A_GROUP * A_BLOCK
    qi = lax.broadcasted_iota(jnp.int32, (rows, A_BLOCK), 0) % A_BLOCK
    ci = lax.broadcasted_iota(jnp.int32, (rows, A_BLOCK), 1)
    sink_b = jnp.concatenate(
        [jnp.full((A_BLOCK, HEAD_DIM), sink_ref[kv * A_GROUP + g], F32) for g in range(A_GROUP)], axis=0)

    def scores(n):
        r0 = pl.multiple_of(n * A_BLOCK, A_BLOCK)
        rp = pl.multiple_of(jnp.maximum(n - 1, 0) * A_BLOCK, A_BLOCK)
        rn = pl.multiple_of(jnp.minimum(n + 1, nb - 1) * A_BLOCK, A_BLOCK)
        off_p = jnp.where(n > 0, 0, 2 * A_BLOCK)
        off_n = jnp.where(n < nb - 1, 0, 2 * A_BLOCK)
        q = jnp.concatenate(
            [q_ref[pl.ds(r0, A_BLOCK), g * HEAD_DIM:(g + 1) * HEAD_DIM] for g in range(A_GROUP)],
            axis=0)
        sp = lax.dot_general(q, k_ref[pl.ds(rp, A_BLOCK), :], _NT, preferred_element_type=F32) * scale
        sc = lax.dot_general(q, k_ref[pl.ds(r0, A_BLOCK), :], _NT, preferred_element_type=F32) * scale
        sn = lax.dot_general(q, k_ref[pl.ds(rn, A_BLOCK), :], _NT, preferred_element_type=F32) * scale
        sp = jnp.where(ci >= qi + off_p, sp, NEG)
        sn = jnp.where(ci <= qi - off_n, sn, NEG)
        return (rp, r0, rn), (sp, sc, sn)

    def exps(parts):
        m = jnp.maximum(jnp.maximum(parts[0], parts[1]), parts[2]).max(axis=1, keepdims=True)
        m = jnp.maximum(jnp.broadcast_to(m, sink_b.shape), sink_b)
        es = [jnp.exp(p - m) for p in parts]
        den = (es[0] + es[1] + es[2]).sum(axis=1, keepdims=True)
        den = jnp.broadcast_to(den, sink_b.shape) + jnp.exp(sink_b - m)
        return [e.astype(BF16) for e in es], 1.0 / den

    def body(it, carry):
        blocks = [scores(it * WIN_UNROLL + u) for u in range(WIN_UNROLL)]
        probs = [exps(parts) for _, parts in blocks]
        for (rows_kv, _), (es, inv) in zip(blocks, probs):
            o = jnp.dot(es[0], v_ref[pl.ds(rows_kv[0], A_BLOCK), :], preferred_element_type=F32)
            o = o + jnp.dot(es[1], v_ref[pl.ds(rows_kv[1], A_BLOCK), :], preferred_element_type=F32)
            o = o + jnp.dot(es[2], v_ref[pl.ds(rows_kv[2], A_BLOCK), :], preferred_element_type=F32)
            o = o * inv
            for g in range(A_GROUP):
                o_ref[pl.ds(rows_kv[1], A_BLOCK), g * HEAD_DIM:(g + 1) * HEAD_DIM] = (
                    o[g * A_BLOCK:(g + 1) * A_BLOCK].astype(o_ref.dtype))
        return carry

    lax.fori_loop(0, nb // WIN_UNROLL, body, 0)


def window_attention(qk, vqkv, sink, batch, seq):
    gw = A_GROUP * HEAD_DIM
    k_blk0 = QA_W // HEAD_DIM
    return pl.pallas_call(
        functools.partial(_win_body, seq=seq),
        grid=(batch, A_KV_HEADS),
        in_specs=[pl.BlockSpec(memory_space=pltpu.SMEM),
                  pl.BlockSpec((seq, gw), lambda b, h: (b, h)),
                  pl.BlockSpec((seq, HEAD_DIM), lambda b, h: (b, k_blk0 + h)),
                  pl.BlockSpec((seq, HEAD_DIM), lambda b, h: (b, h))],
        out_specs=pl.BlockSpec((seq, gw), lambda b, h: (b, h)),
        out_shape=jax.ShapeDtypeStruct((batch * seq, QA_W), BF16),
        compiler_params=_params(("arbitrary", "arbitrary"), 32),
        name="window_attention",
    )(sink, qk, qk, vqkv)


NBR_HG = 4
NBR_ROWS = 4


def _nbr_body(q_ref, k_ref, v_ref, bias_ref, o_ref, *, seq):
    rows = seq // GRID_W
    kh = min(NA_KH_MAX, rows)
    strip = kh * GRID_W
    scale = HEAD_DIM ** -0.5

    def body(it, carry):
        units = []
        for rr in range(NBR_ROWS):
            r = it * NBR_ROWS + rr
            rs = jnp.clip(r - kh // 2, 0, rows - kh)
            q0 = pl.multiple_of(r * GRID_W, GRID_W)
            k0 = pl.multiple_of(rs * GRID_W, GRID_W)
            for h in range(NBR_HG):
                units.append((q0, k0, r - rs, h, slice(h * HEAD_DIM, (h + 1) * HEAD_DIM)))
        ss = [lax.dot_general(q_ref[pl.ds(q0, GRID_W), cols], k_ref[pl.ds(k0, strip), cols], _NT,
                              preferred_element_type=F32) * scale + bias_ref[h, var]
              for q0, k0, var, h, cols in units]
        ps = []
        for s in ss:
            e = jnp.exp(s - s.max(axis=1, keepdims=True))
            ps.append((e.astype(BF16), 1.0 / e.sum(axis=1, keepdims=True)))
        for (q0, k0, var, h, cols), (e, inv) in zip(units, ps):
            o = jnp.dot(e, v_ref[pl.ds(k0, strip), cols], preferred_element_type=F32) * inv
            o_ref[pl.ds(q0, GRID_W), cols] = o.astype(o_ref.dtype)
        return carry

    lax.fori_loop(0, rows // NBR_ROWS, body, 0)


def neighbourhood_attention(vqkv, bias_tbl, batch, seq):
    gw = NBR_HG * HEAD_DIM
    q0, k0, v0 = KVA_W // gw, (KVA_W + QB_W) // gw, (KVA_W + 2 * QB_W) // gw
    kh = bias_tbl.shape[1]
    return pl.pallas_call(
        functools.partial(_nbr_body, seq=seq),
        grid=(B_HEADS // NBR_HG, batch),
        in_specs=[pl.BlockSpec((seq, gw), lambda g, b: (b, q0 + g)),
                  pl.BlockSpec((seq, gw), lambda g, b: (b, k0 + g)),
                  pl.BlockSpec((seq, gw), lambda g, b: (b, v0 + g)),
                  pl.BlockSpec((NBR_HG, kh, GRID_W, kh * GRID_W), lambda g, b: (g, 0, 0, 0))],
        out_specs=pl.BlockSpec((seq, gw), lambda g, b: (b, g)),
        out_shape=jax.ShapeDtypeStruct((batch * seq, QB_W), BF16),
        compiler_params=_params(("arbitrary", "arbitrary"), 40),
        name="neighbourhood_attention",
    )(vqkv, vqkv, vqkv, bias_tbl)


def _bias_table_body(rpb_ref, o_ref, *, kh):
    h = pl.program_id(0)
    n_dr, n_dc = 2 * NA_KH_MAX - 1, 2 * NA_KW - 1
    c = lax.broadcasted_iota(jnp.int32, (GRID_W, LANES), 0)
    lane = lax.broadcasted_iota(jnp.int32, (GRID_W, LANES), 1)
    kc = lane % GRID_W
    diff = jnp.clip(kc - c + NA_KW - 1, 0, n_dc - 1)
    cs = jnp.clip(c - NA_KW // 2, 0, GRID_W - NA_KW)
    col_ok = (kc >= cs) & (kc < cs + NA_KW)
    slabs = []
    for dr in range(n_dr):
        acc = jnp.zeros((GRID_W, LANES), F32)
        for d in range(n_dc):
            acc = jnp.where(diff == d, rpb_ref[(h * n_dr + dr) * n_dc + d], acc)
        slabs.append(jnp.where(col_ok, acc, NEG))
    left = lane < GRID_W
    for var in range(kh):
        for jp in range(kh * GRID_W // LANES):
            dr0 = 2 * jp - var + NA_KH_MAX - 1
            o_ref[0, var, :, jp * LANES:(jp + 1) * LANES] = jnp.where(left, slabs[dr0], slabs[dr0 + 1])


def _nbr_bias_table(rpb, seq):
    rows = seq // GRID_W
    kh = min(NA_KH_MAX, rows)
    heads = rpb.shape[0]
    assert kh == NA_KH_MAX and 2 * GRID_W == LANES
    return pl.pallas_call(
        functools.partial(_bias_table_body, kh=kh),
        grid=(heads,),
        in_specs=[pl.BlockSpec(memory_space=pltpu.SMEM)],
        out_specs=pl.BlockSpec((1, kh, GRID_W, kh * GRID_W), lambda h: (h, 0, 0, 0)),
        out_shape=jax.ShapeDtypeStruct((heads, kh, GRID_W, kh * GRID_W), F32),
        compiler_params=_params(("arbitrary",), 16),
        name="nbr_bias_table",
    )(rpb.astype(F32).reshape(-1))


def _merge_body(oa_ref, ob_ref, wa_ref, wb_ref, g0_ref, g1_ref, o_ref, wa_s, wb_s, *, k_rows):
    @pl.when(pl.program_id(1) == 0)
    def _():
        _cast_rows(wa_ref, wa_s, k_rows)
        _cast_rows(wb_ref, wb_s, k_rows)

    ya = jnp.dot(oa_ref[...], wa_s[...], preferred_element_type=F32)
    yb = jnp.dot(ob_ref[...], wb_s[...], preferred_element_type=F32)
    o_ref[...] = (g0_ref[...].astype(F32) * ya + g1_ref[...].astype(F32) * yb).astype(o_ref.dtype)


def branch_merge(oa, ob, wa, wb, gates, bm=1024, bn=512):
    m, k = oa.shape
    n = wa.shape[1]
    g1_off = n // bn
    return pl.pallas_call(
        functools.partial(_merge_body, k_rows=k),
        grid=(n // bn, m // bm),
        in_specs=[pl.BlockSpec((bm, k), lambda j, i: (i, 0)),
                  pl.BlockSpec((bm, k), lambda j, i: (i, 0)),
                  pl.BlockSpec((k, bn), lambda j, i: (0, j)),
                  pl.BlockSpec((k, bn), lambda j, i: (0, j)),
                  pl.BlockSpec((bm, bn), lambda j, i: (i, j)),
                  pl.BlockSpec((bm, bn), lambda j, i: (i, j + g1_off))],
        out_specs=pl.BlockSpec((bm, bn), lambda j, i: (i, j)),
        out_shape=jax.ShapeDtypeStruct((m, n), BF16),
        scratch_shapes=[pltpu.VMEM((k, bn), BF16), pltpu.VMEM((k, bn), BF16)],
        compiler_params=_params(("arbitrary", "arbitrary"), 56),
        name="branch_merge",
    )(oa, ob, wa, wb, gates, gates)


def _cast_body(x_ref, o_ref):
    o_ref[...] = x_ref[...].astype(o_ref.dtype)


def cast_bf16(w):
    r, c = w.shape
    return pl.pallas_call(
        _cast_body,
        grid=(1,),
        in_specs=[pl.BlockSpec((r, c), lambda i: (0, 0))],
        out_specs=pl.BlockSpec((r, c), lambda i: (0, 0)),
        out_shape=jax.ShapeDtypeStruct((r, c), BF16),
        compiler_params=_params(("arbitrary",), 40),
        name="cast_bf16",
    )(w)


def _router_probs(hn, wr_ref):
    hi = hn.astype(BF16)
    lo = (hn - hi.astype(F32)).astype(BF16)
    l_hi = jnp.dot(hi, wr_ref[...], preferred_element_type=F32)
    l_lo = jnp.dot(lo, wr_ref[...], preferred_element_type=F32)
    logits = l_hi + pltpu.roll(l_hi, LANES - N_EXPERTS, 1) + l_lo
    lane = lax.broadcasted_iota(jnp.int32, logits.shape, 1)
    logits = jnp.where(lane < N_EXPERTS, logits, NEG)
    (aff,) = _softmax_parts([logits])
    return aff


def _xblock_body(x_ref, gc_ref, wq_ref, k_ref, v_ref, wo_ref, gf_ref, wr_ref,
                 x2_ref, h3_ref, aff_ref, atm_ref):
    scale = HEAD_DIM ** -0.5
    x = x_ref[...]
    h2 = _rmsnorm_rows(x, gc_ref[...]).astype(BF16)
    q = jnp.dot(h2, wq_ref[...], preferred_element_type=F32).astype(BF16)
    heads = []
    for h in range(X_HEADS):
        cols = slice(h * HEAD_DIM, (h + 1) * HEAD_DIM)
        s = lax.dot_general(q[:, cols], k_ref[:, cols], _NT, preferred_element_type=F32) * scale
        (p,) = _softmax_parts([s])
        heads.append(jnp.dot(p.astype(BF16), v_ref[:, cols], preferred_element_type=F32).astype(BF16))
    o = jnp.concatenate(heads, axis=1)
    x2 = x + jnp.dot(o, wo_ref[...], preferred_element_type=F32)
    x2_ref[...] = x2
    hn = _rmsnorm_rows(x2, gf_ref[...])
    half = hn.shape[1] // 2
    packed = pltpu.pack_elementwise([hn[:, :half], hn[:, half:]], packed_dtype=BF16)
    h3_ref[...] = pltpu.bitcast(packed, jnp.uint32)
    aff = _router_probs(hn, wr_ref)
    atm_ref[...] = aff
    aff_ref[0] = aff.T[:N_EXPERTS, :]


def cross_attention_block(x1, g_cross, wq, kx, vx, wo, g_ffn, w_router, batch, seq, mem_len, bm=256):
    m, d = x1.shape
    nt = seq // bm
    w_hi = w_router.astype(BF16)
    w_lo = (w_router - w_hi.astype(F32)).astype(BF16)
    wr = jnp.concatenate([w_hi, w_lo, jnp.zeros((d, LANES - 2 * N_EXPERTS), BF16)], axis=1)
    const = lambda i: (0, 0)
    return pl.pallas_call(
        _xblock_body,
        grid=(m // bm,),
        in_specs=[pl.BlockSpec((bm, d), lambda i: (i, 0)),
                  pl.BlockSpec((1, d), const),
                  pl.BlockSpec((d, X_W), const),
                  pl.BlockSpec((mem_len, X_W), lambda i: (i // nt, 0)),
                  pl.BlockSpec((mem_len, X_W), lambda i: (i // nt, 0)),
                  pl.BlockSpec((X_W, d), const),
                  pl.BlockSpec((1, d), const),
                  pl.BlockSpec((d, LANES), const)],
        out_specs=[pl.BlockSpec((bm, d), lambda i: (i, 0)),
                   pl.BlockSpec((bm, d // 2), lambda i: (i, 0)),
                   pl.BlockSpec((1, N_EXPERTS, bm), lambda i: (i // nt, 0, i % nt)),
                   pl.BlockSpec((bm, LANES), lambda i: (i, 0))],
        out_shape=[jax.ShapeDtypeStruct((m, d), F32),
                   jax.ShapeDtypeStruct((m, d // 2), jnp.uint32),
                   jax.ShapeDtypeStruct((batch, N_EXPERTS, seq), F32),
                   jax.ShapeDtypeStruct((m, LANES), F32)],
        compiler_params=_params(("arbitrary",), 48),
        name="cross_attention_block",
    )(x1, g_cross.reshape(1, d), cast_bf16(wq), kx, vx, cast_bf16(wo), g_ffn.reshape(1, d), wr)


CUM_CHUNK = 256


def _excl_cumsum_lanes(x01, tri):
    n = x01.shape[1]
    carry = jnp.zeros((x01.shape[0], 1), F32)
    out = []
    for c in range(n // CUM_CHUNK):
        xc = x01[:, c * CUM_CHUNK:(c + 1) * CUM_CHUNK]
        out.append(jnp.dot(xc.astype(BF16), tri, preferred_element_type=F32) + carry)
        carry = carry + xc.sum(axis=1, keepdims=True)
    return jnp.concatenate(out, axis=1)


RANK_CHUNK = 256
SUBLANES = 8


COMBINE_TS = 256
COMBINE_W = 64


def _topk_body(aff_ref, atm_ref, slot_ref, first_ref, cgt_ref, *, cap):
    n_exp, s = aff_ref.shape[1], aff_ref.shape[2]
    capf = jnp.float32(cap)
    for e in range(n_exp):
        a_row = aff_ref[0, e:e + 1, :]

        def chunk(c, acc, a_row=a_row, e=e):
            r = pl.multiple_of(c * RANK_CHUNK, RANK_CHUNK)
            col = atm_ref[pl.ds(r, RANK_CHUNK), e:e + 1]
            above = jnp.where(col > a_row, 1.0, 0.0)
            return acc + above.reshape(RANK_CHUNK // SUBLANES, SUBLANES, s).sum(axis=0)

        acc = lax.fori_loop(0, s // RANK_CHUNK, chunk, jnp.zeros((SUBLANES, s), F32))
        cgt_ref[e:e + 1, :] = acc.sum(axis=0, keepdims=True)
    cgt = cgt_ref[...]
    ri = lax.broadcasted_iota(jnp.int32, (CUM_CHUNK, CUM_CHUNK), 0)
    cj = lax.broadcasted_iota(jnp.int32, (CUM_CHUNK, CUM_CHUNK), 1)
    tri = jnp.where(ri < cj, 1.0, 0.0).astype(BF16)
    cand = jnp.where(cgt < capf, 1.0, 0.0)
    extra = cand.sum(axis=1, keepdims=True) - capf
    g_last = jnp.where(cand > 0.5, cgt, -1.0).max(axis=1, keepdims=True)
    tie = jnp.where(cgt == g_last, cand, 0.0)
    keep = tie.sum(axis=1, keepdims=True) - extra
    sel = cand - jnp.where(_excl_cumsum_lanes(tie, tri) >= keep, tie, 0.0)
    pos = _excl_cumsum_lanes(sel, tri)
    slot_ref[0] = jnp.where(sel > 0.5, pos, -1.0).astype(jnp.int32)
    lane_s = lax.broadcasted_iota(jnp.int32, (n_exp, s), 1)
    lane_k = lax.broadcasted_iota(jnp.int32, (n_exp, LANES), 1)
    first = jnp.zeros((n_exp, LANES), F32)
    for k in range(s // COMBINE_TS):
        at_k = jnp.where(lane_s == k * COMBINE_TS, pos, 0.0).sum(axis=1, keepdims=True)
        first = jnp.where(lane_k == k, at_k, first)
    first_ref[0] = first.astype(jnp.int32)


def expert_topk(aff_t, aff_tm, cap):
    b, e, s = aff_t.shape
    return pl.pallas_call(
        functools.partial(_topk_body, cap=cap),
        grid=(b,),
        in_specs=[pl.BlockSpec((1, e, s), lambda i: (i, 0, 0)),
                  pl.BlockSpec((s, LANES), lambda i: (i, 0))],
        out_specs=[pl.BlockSpec((1, e, s), lambda i: (i, 0, 0)),
                   pl.BlockSpec((1, e, LANES), lambda i: (i, 0, 0))],
        out_shape=[jax.ShapeDtypeStruct((b, e, s), jnp.int32),
                   jax.ShapeDtypeStruct((b, e, LANES), jnp.int32)],
        scratch_shapes=[pltpu.VMEM((e, s), F32)],
        compiler_params=_params(("arbitrary",), 32),
        name="expert_topk",
    )(aff_t, aff_tm)


def _slot_index_body(slot_ref, aff_ref, idx_ref, val_ref, *, cap):
    b = pl.program_id(0)
    n_exp, s = slot_ref.shape[1], slot_ref.shape[2]
    ci = lax.broadcasted_iota(jnp.int32, (cap, s), 0)
    tok = lax.broadcasted_iota(jnp.int32, (cap, s), 1).astype(F32)
    base = (b * s).astype(F32)
    for e in range(n_exp):
        hit = slot_ref[0, e:e + 1, :] == ci
        idx = jnp.where(hit, tok, 0.0).sum(axis=1, keepdims=True) + base
        idx_ref[e] = idx.astype(jnp.int32)
        val_ref[e] = jnp.where(hit, aff_ref[0, e:e + 1, :], 0.0).sum(axis=1, keepdims=True)


def slot_index(slot, aff_t, cap):
    b, e, s = slot.shape
    return pl.pallas_call(
        functools.partial(_slot_index_body, cap=cap),
        grid=(b,),
        in_specs=[pl.BlockSpec((1, e, s), lambda bi: (bi, 0, 0)),
                  pl.BlockSpec((1, e, s), lambda bi: (bi, 0, 0))],
        out_specs=[pl.BlockSpec((e, cap, 1), lambda bi: (0, bi, 0)),
                   pl.BlockSpec((e, cap, 1), lambda bi: (0, bi, 0))],
        out_shape=[jax.ShapeDtypeStruct((e, b * cap, 1), jnp.int32),
                   jax.ShapeDtypeStruct((e, b * cap, 1), F32)],
        compiler_params=_params(("arbitrary",), 32),
        name="slot_index",
    )(slot, aff_t)


UNPACK_ROWS = 128


def _row_copy(idx_ref, hp_ref, gbuf, sem, expert, rows, row):
    tok = idx_ref[expert * rows + row]
    return pltpu.make_async_copy(hp_ref.at[pl.ds(tok, 1)], gbuf.at[pl.ds(row, 1)], sem)


def _expert_up_body(idx_ref, hp_ref, wg_ref, wu_ref, o_ref, gbuf, xbf, sem, *, rows, per_step):
    e, f = pl.program_id(0), pl.program_id(1)
    n_e, n_f = pl.num_programs(0), pl.num_programs(1)
    half = gbuf.shape[1]

    def wait_all_rows():
        pltpu.make_async_copy(hp_ref.at[pl.ds(0, rows)], gbuf, sem).wait()

    @pl.when(jnp.logical_and(e == 0, f == 0))
    def _():
        def body(r, carry):
            _row_copy(idx_ref, hp_ref, gbuf, sem, 0, rows, r).start()
            return carry
        lax.fori_loop(0, rows, body, 0)

    @pl.when(f == 0)
    def _():
        wait_all_rows()

        def unpack(k, carry):
            r = pl.multiple_of(k * UNPACK_ROWS, UNPACK_ROWS)
            w = gbuf[pl.ds(r, UNPACK_ROWS), :]
            lo = pltpu.unpack_elementwise(w, index=0, packed_dtype=BF16, unpacked_dtype=F32)
            hi = pltpu.unpack_elementwise(w, index=1, packed_dtype=BF16, unpacked_dtype=F32)
            xbf[pl.ds(r, UNPACK_ROWS), :half] = lo.astype(BF16)
            xbf[pl.ds(r, UNPACK_ROWS), half:] = hi.astype(BF16)
            return carry
        lax.fori_loop(0, rows // UNPACK_ROWS, unpack, 0)

    nxt = jnp.minimum(e + 1, n_e - 1)
    for r in range(per_step):
        _row_copy(idx_ref, hp_ref, gbuf, sem, nxt, rows, f * per_step + r).start()

    x = xbf[...]
    a = jnp.dot(x, wg_ref[0].astype(BF16), preferred_element_type=F32)
    u = jnp.dot(x, wu_ref[0].astype(BF16), preferred_element_type=F32)
    o_ref[0] = (jax.nn.silu(a) * u).astype(o_ref.dtype)

    @pl.when(jnp.logical_and(e == n_e - 1, f == n_f - 1))
    def _():
        wait_all_rows()


def expert_up(idx, hp, w_gate, w_up, rows, tf=256):
    n_e, d, f = w_gate.shape
    assert hp.shape[1] * 2 == d and rows % (f // tf) == 0
    grid_spec = pltpu.PrefetchScalarGridSpec(
        num_scalar_prefetch=1,
        grid=(n_e, f // tf),
        in_specs=[pl.BlockSpec(memory_space=pl.ANY),
                  pl.BlockSpec((1, d, tf), lambda ei, fi, idx_ref: (ei, 0, fi)),
                  pl.BlockSpec((1, d, tf), lambda ei, fi, idx_ref: (ei, 0, fi))],
        out_specs=pl.BlockSpec((1, rows, tf), lambda ei, fi, idx_ref: (ei, 0, fi)),
        scratch_shapes=[pltpu.VMEM((rows, d // 2), jnp.uint32),
                        pltpu.VMEM((rows, d), BF16),
                        pltpu.SemaphoreType.DMA(())],
    )
    return pl.pallas_call(
        functools.partial(_expert_up_body, rows=rows, per_step=rows // (f // tf)),
        grid_spec=grid_spec,
        out_shape=jax.ShapeDtypeStruct((n_e, rows, f), BF16),
        compiler_params=_params(("arbitrary", "arbitrary"), 56),
        name="expert_up",
    )(idx, hp, w_gate, w_up)


def _expert_down_body(h_ref, wd_ref, val_ref, o_ref):
    y = jnp.dot(h_ref[0], wd_ref[0].astype(BF16), preferred_element_type=F32)
    o_ref[0] = (y * val_ref[0]).astype(o_ref.dtype)


def expert_down(hmid, w_down, valc, tn=1024):
    e, rows, f = hmid.shape
    d = w_down.shape[-1]
    return pl.pallas_call(
        _expert_down_body,
        grid=(e, d // tn),
        in_specs=[pl.BlockSpec((1, rows, f), lambda ei, ni: (ei, 0, 0)),
                  pl.BlockSpec((1, f, tn), lambda ei, ni: (ei, 0, ni)),
                  pl.BlockSpec((1, rows, 1), lambda ei, ni: (ei, 0, 0))],
        out_specs=pl.BlockSpec((1, rows, tn), lambda ei, ni: (ei, 0, ni)),
        out_shape=jax.ShapeDtypeStruct((e, rows, d), BF16),
        compiler_params=_params(("arbitrary", "arbitrary"), 48),
        name="expert_down",
    )(hmid, w_down, valc)


COMBINE_TN = 512
ROW_ALIGN_BF16 = 16


def _combine_windows(first, cap, n_rows):
    b, e, nt = first.shape
    row0 = jnp.arange(b, dtype=jnp.int32)[:, None, None] * cap
    lo = first + row0
    hi = jnp.concatenate([first[:, :, 1:], jnp.full((b, e, 1), cap, jnp.int32)], axis=2) + row0
    start = jnp.minimum(lo // ROW_ALIGN_BF16 * ROW_ALIGN_BF16, n_rows - COMBINE_W)
    fast = jnp.all(hi - start <= COMBINE_W, axis=1)
    return start.transpose(0, 2, 1).reshape(-1), fast.astype(jnp.int32).reshape(-1)


def _combine_body(win_ref, fast_ref, slot_ref, y_hbm, x_ref, g_ref, o_ref,
                  ywin, oht, ybuf, ohs, wsem, ssem, *, cap, n_tiles):
    i = pl.program_id(0)
    n_exp = slot_ref.shape[1]
    ts, d = o_ref.shape
    w = COMBINE_W
    row0 = (i // n_tiles) * cap
    buf = i % 2

    def window_copy(step, e, b):
        start = pl.multiple_of(win_ref[step * n_exp + e], ROW_ALIGN_BF16)
        return pltpu.make_async_copy(y_hbm.at[e, pl.ds(start, w)], ywin.at[b, pl.ds(e * w, w)],
                                     wsem.at[b])

    @pl.when(i == 0)
    def _():
        for e in range(n_exp):
            window_copy(0, e, 0).start()

    @pl.when(i + 1 < pl.num_programs(0))
    def _():
        for e in range(n_exp):
            window_copy(i + 1, e, 1 - buf).start()

    pltpu.make_async_copy(y_hbm.at[0, pl.ds(0, n_exp * w)], ywin.at[buf], wsem.at[buf]).wait()

    @pl.when(fast_ref[i] == 1)
    def _():
        row = lax.broadcasted_iota(jnp.int32, (2 * w, ts), 0)
        upper = row >= w
        j = row % w
        for p in range(n_exp // 2):
            rel_a = slot_ref[0, 2 * p:2 * p + 1, :] + (row0 - win_ref[i * n_exp + 2 * p])
            rel_b = slot_ref[0, 2 * p + 1:2 * p + 2, :] + (row0 - win_ref[i * n_exp + 2 * p + 1])
            hit = jnp.where(upper, rel_b, rel_a) == j
            oht[:, p * 2 * w:(p + 1) * 2 * w] = jnp.where(hit, 1.0, 0.0).T.astype(BF16)
        for c in range(d // COMBINE_TN):
            cols = slice(c * COMBINE_TN, (c + 1) * COMBINE_TN)
            o_ref[:, cols] = x_ref[:, cols] + jnp.dot(oht[...], ywin[buf, :, cols],
                                                      preferred_element_type=F32)

    @pl.when(fast_ref[i] == 0)
    def _():
        o_ref[...] = x_ref[...]
        ci = lax.broadcasted_iota(jnp.int32, (cap, ts), 0)

        def body(e, carry):
            cp = pltpu.make_async_copy(y_hbm.at[e, pl.ds(pl.multiple_of(row0, cap), cap)], ybuf, ssem)
            cp.start()
            cp.wait()
            srow = slot_ref[0, pl.ds(e, 1), :]
            ohs[...] = jnp.where(srow == ci, 1.0, 0.0).T.astype(BF16)
            for c in range(d // COMBINE_TN):
                cols = slice(c * COMBINE_TN, (c + 1) * COMBINE_TN)
                o_ref[:, cols] += jnp.dot(ohs[...], ybuf[:, cols], preferred_element_type=F32)
            return carry
        lax.fori_loop(0, n_exp, body, 0)

    o_ref[...] = _rmsnorm_rows(o_ref[...], g_ref[...])


def expert_combine(slot, first, y, x2d, g, cap):
    b, e, s = slot.shape
    d = x2d.shape[-1]
    ts, w = COMBINE_TS, COMBINE_W
    nt = s // ts
    assert e % 2 == 0 and y.shape[1] >= e * w and cap % ROW_ALIGN_BF16 == 0
    win, fast = _combine_windows(first[:, :, :nt], cap, y.shape[1])
    grid_spec = pltpu.PrefetchScalarGridSpec(
        num_scalar_prefetch=2,
        grid=(b * nt,),
        in_specs=[pl.BlockSpec((1, e, ts), lambda i, win_ref, fast_ref: (i // nt, 0, i % nt)),
                  pl.BlockSpec(memory_space=pl.ANY),
                  pl.BlockSpec((ts, d), lambda i, win_ref, fast_ref: (i, 0)),
                  pl.BlockSpec((1, d), lambda i, win_ref, fast_ref: (0, 0))],
        out_specs=pl.BlockSpec((ts, d), lambda i, win_ref, fast_ref: (i, 0)),
        scratch_shapes=[pltpu.VMEM((2, e * w, d), BF16),
                        pltpu.VMEM((ts, e * w), BF16),
                        pltpu.VMEM((cap, d), BF16),
                        pltpu.VMEM((ts, cap), BF16),
                        pltpu.SemaphoreType.DMA((2,)),
                        pltpu.SemaphoreType.DMA(())],
    )
    return pl.pallas_call(
        functools.partial(_combine_body, cap=cap, n_tiles=nt),
        grid_spec=grid_spec,
        out_shape=jax.ShapeDtypeStruct(x2d.shape, F32),
        compiler_params=_params(("arbitrary",), 48),
        name="expert_combine",
    )(win, fast, slot, y, x2d, g.reshape(1, d))


def _rotary_tables(seq):
    half = ROT_DIM // 2
    inv = ROPE_THETA ** (-jnp.arange(half, dtype=F32) * 2.0 / ROT_DIM)
    ang = jnp.arange(seq).astype(F32)[:, None] * inv[None, :]
    cos, sin = jnp.cos(ang), jnp.sin(ang)
    ones = jnp.ones((seq, HEAD_DIM - ROT_DIM), F32)
    zeros = jnp.zeros((seq, HEAD_DIM - ROT_DIM), F32)
    zh = jnp.zeros((seq, half), F32)
    c = jnp.concatenate([cos, cos, ones], axis=1)
    s1 = jnp.concatenate([-sin, zh, zeros], axis=1)
    s2 = jnp.concatenate([zh, sin, zeros], axis=1)
    return c, s1, s2


def kernel(x, mem, norm_mix, w_in, b_gate, sink, rpb, w_branch_a, w_branch_b, w_out,
           norm_cross, norm_mem, wq_x, wk_x, wv_x, wo_x, norm_ffn, w_router,
           w_gate, w_up, w_down, norm_final):
    batch, seq, d = x.shape
    mem_len = mem.shape[1]
    m = batch * seq
    assert norm_mix.shape[0] == 1, "final RMSNorm is fused into the single layer's last kernel"
    cap = EC_CAPACITY * seq // N_EXPERTS
    bm, bn = 1024, 512
    sb = seq // bm
    x0 = x.reshape(m, d)

    h = rmsnorm(x0, norm_mix[0], BF16)
    rot = _rotary_tables(seq)
    rot_specs = [pl.BlockSpec((bm, HEAD_DIM), lambda j, i: (i % sb, 0))] * 3
    qk = matmul(h, w_in[0], col_off=0, n_cols=QA_W + KVA_W, bm=bm, bn=bn, out_dtype=BF16,
                epilogue=_ep_rotary, extras=rot, extra_specs=rot_specs, name="in_proj_rotary")
    vqkv = matmul_rows(h, w_in[0], col_off=QA_W + KVA_W, n_cols=KVA_W + 3 * QB_W, bm=2 * bm, bn=bn // 2,
                       out_dtype=BF16, name="in_proj_plain")
    g_off = QA_W + 2 * KVA_W + 3 * QB_W
    gates = matmul(h, w_in[0], col_off=g_off, n_cols=2 * d, bm=bm, bn=bn, out_dtype=BF16,
                   epilogue=_ep_sigmoid, extras=(b_gate[0].reshape(1, 2 * d),),
                   extra_specs=[pl.BlockSpec((1, bn), lambda j, i: (0, j))], name="in_proj_gates")
    oa = window_attention(qk, vqkv, sink[0], batch, seq)
    ob = neighbourhood_attention(vqkv, _nbr_bias_table(rpb[0], seq), batch, seq)
    merged = branch_merge(oa, ob, w_branch_a[0], w_branch_b[0], gates)
    res_spec = [pl.BlockSpec((bm, bn), lambda j, i: (i, j))]
    x1 = matmul(merged, w_out[0], col_off=0, n_cols=d, bm=bm, bn=bn, out_dtype=F32,
                epilogue=_ep_residual, extras=(x0,), extra_specs=res_spec, name="out_proj")

    mn = rmsnorm(mem.reshape(batch * mem_len, d), norm_mem[0], BF16)
    kx = matmul(mn, wk_x[0], col_off=0, n_cols=X_W, bm=bm, bn=bn, out_dtype=BF16, name="xattn_k")
    vx = matmul(mn, wv_x[0], col_off=0, n_cols=X_W, bm=bm, bn=bn, out_dtype=BF16, name="xattn_v")
    x2, h3p, aff_t, aff_tm = cross_attention_block(
        x1, norm_cross[0], wq_x[0], kx, vx, wo_x[0], norm_ffn[0], w_router[0], batch, seq, mem_len)

    slot, first = expert_topk(aff_t, aff_tm, cap)
    idx, valc = slot_index(slot, aff_t, cap)
    hmid = expert_up(idx.reshape(-1), h3p, w_gate[0], w_up[0], batch * cap)
    y = expert_down(hmid, w_down[0], valc)
    out = expert_combine(slot, first, y, x2, norm_final, cap)
    return out.reshape(batch, seq, d)
```

```python
import functools

import jax
import jax.numpy as jnp
from jax import lax
from jax.experimental import pallas as pl
from jax.experimental.pallas import tpu as pltpu

F32 = jnp.float32
BF16 = jnp.bfloat16

HEAD_DIM = 128
A_HEADS = 16
A_KV_HEADS = 4
A_GROUP = A_HEADS // A_KV_HEADS
WINDOW = 128
A_BLOCK = 128
ROT_DIM = HEAD_DIM // 4
ROPE_THETA = 500000.0
B_HEADS = 16
GRID_W = 64
NA_KH_MAX = 8
NA_KW = 16
X_HEADS = 4
N_EXPERTS = 16
EC_CAPACITY = 2
EPS = 1e-6
NEG = -1e30
LANES = 128
MIB = 1024 * 1024

QA_W = A_HEADS * HEAD_DIM
KVA_W = A_KV_HEADS * HEAD_DIM
QB_W = B_HEADS * HEAD_DIM
X_W = X_HEADS * HEAD_DIM

_NT = (((1,), (1,)), ((), ()))
_TN = (((0,), (0,)), ((), ()))


def _params(semantics, vmem_mib):
    return pltpu.CompilerParams(dimension_semantics=semantics,
                                vmem_limit_bytes=vmem_mib * MIB)


def _cast_rows(src_ref, dst_ref, rows, chunk=256):
    def body(k, carry):
        r = pl.multiple_of(k * chunk, chunk)
        dst_ref[pl.ds(r, chunk), :] = src_ref[pl.ds(r, chunk), :].astype(dst_ref.dtype)
        return carry
    lax.fori_loop(0, rows // chunk, body, 0)


def _rmsnorm_rows(x, g):
    ms = jnp.mean(x * x, axis=-1, keepdims=True)
    return x * lax.rsqrt(ms + EPS) * g


def _rmsnorm_body(x_ref, g_ref, o_ref):
    o_ref[...] = _rmsnorm_rows(x_ref[...], g_ref[...]).astype(o_ref.dtype)


def rmsnorm(x2d, g, out_dtype, bm=256):
    m, d = x2d.shape
    return pl.pallas_call(
        _rmsnorm_body,
        grid=(m // bm,),
        in_specs=[pl.BlockSpec((bm, d), lambda i: (i, 0)),
                  pl.BlockSpec((1, d), lambda i: (0, 0))],
        out_specs=pl.BlockSpec((bm, d), lambda i: (i, 0)),
        out_shape=jax.ShapeDtypeStruct((m, d), out_dtype),
        compiler_params=_params(("arbitrary",), 40),
        name="rmsnorm",
    )(x2d, g.reshape(1, d))


ROW_SPLIT = 4


def _row_parts(n_rows):
    step = n_rows // ROW_SPLIT
    return [slice(p * step, (p + 1) * step) for p in range(ROW_SPLIT)]


def _ep_store(acc, rows, o_ref):
    o_ref[rows, :] = acc.astype(o_ref.dtype)


def _ep_residual(acc, rows, o_ref, r_ref):
    o_ref[rows, :] = (r_ref[rows, :] + acc).astype(o_ref.dtype)


def _ep_sigmoid(acc, rows, o_ref, b_ref):
    o_ref[rows, :] = jax.nn.sigmoid(acc + b_ref[...]).astype(o_ref.dtype)


def _ep_rotary(acc, rows, o_ref, c_ref, s1_ref, s2_ref):
    c, s1, s2 = c_ref[rows, :], s1_ref[rows, :], s2_ref[rows, :]
    half = ROT_DIM // 2
    for h in range(acc.shape[1] // HEAD_DIM):
        a = acc[:, h * HEAD_DIM:(h + 1) * HEAD_DIM]
        r = a * c + pltpu.roll(a, HEAD_DIM - half, 1) * s1 + pltpu.roll(a, half, 1) * s2
        o_ref[rows, h * HEAD_DIM:(h + 1) * HEAD_DIM] = r.astype(o_ref.dtype)


def _mm_body(*refs, n_extra, epilogue, k_rows):
    a_ref, w_ref = refs[0], refs[1]
    extra = refs[2:2 + n_extra]
    o_ref = refs[2 + n_extra]
    wb_ref = refs[3 + n_extra]

    @pl.when(pl.program_id(1) == 0)
    def _():
        _cast_rows(w_ref, wb_ref, k_rows)

    for rows in _row_parts(a_ref.shape[0]):
        acc = jnp.dot(a_ref[rows, :], wb_ref[...], preferred_element_type=F32)
        epilogue(acc, rows, o_ref, *extra)


def matmul(a, w, *, col_off, n_cols, bm, bn, out_dtype, epilogue=_ep_store,
           extras=(), extra_specs=(), vmem_mib=56, name="matmul"):
    m, k = a.shape
    off = col_off // bn
    assert col_off % bn == 0 and n_cols % bn == 0 and m % bm == 0
    body = functools.partial(_mm_body, n_extra=len(extras), epilogue=epilogue, k_rows=k)
    return pl.pallas_call(
        body,
        grid=(n_cols // bn, m // bm),
        in_specs=[pl.BlockSpec((bm, k), lambda j, i: (i, 0)),
                  pl.BlockSpec((k, bn), lambda j, i: (0, j + off))] + list(extra_specs),
        out_specs=pl.BlockSpec((bm, bn), lambda j, i: (i, j)),
        out_shape=jax.ShapeDtypeStruct((m, n_cols), out_dtype),
        scratch_shapes=[pltpu.VMEM((k, bn), BF16)],
        compiler_params=_params(("arbitrary", "arbitrary"), vmem_mib),
        name=name,
    )(a, w, *extras)


def _mm_rows_body(*refs, n_extra, epilogue):
    a_ref, w_ref = refs[0], refs[1]
    extra = refs[2:2 + n_extra]
    o_ref = refs[2 + n_extra]
    wb = w_ref[...].astype(BF16)
    for rows in _row_parts(a_ref.shape[0]):
        acc = jnp.dot(a_ref[rows, :], wb, preferred_element_type=F32)
        epilogue(acc, rows, o_ref, *extra)


def matmul_rows(a, w, *, col_off, n_cols, bm, bn, out_dtype, epilogue=_ep_store,
                extras=(), extra_specs=(), vmem_mib=56, name="matmul_rows"):
    m, k = a.shape
    off = col_off // bn
    assert col_off % bn == 0 and n_cols % bn == 0 and m % bm == 0
    body = functools.partial(_mm_rows_body, n_extra=len(extras), epilogue=epilogue)
    return pl.pallas_call(
        body,
        grid=(m // bm, n_cols // bn),
        in_specs=[pl.BlockSpec((bm, k), lambda i, j: (i, 0)),
                  pl.BlockSpec((k, bn), lambda i, j: (0, j + off))] + list(extra_specs),
        out_specs=pl.BlockSpec((bm, bn), lambda i, j: (i, j)),
        out_shape=jax.ShapeDtypeStruct((m, n_cols), out_dtype),
        compiler_params=_params(("arbitrary", "arbitrary"), vmem_mib),
        name=name,
    )(a, w, *extras)


def _softmax_parts(parts, extra_col=None):
    m = parts[0].max(axis=1, keepdims=True)
    for p in parts[1:]:
        m = jnp.maximum(m, p.max(axis=1, keepdims=True))
    if extra_col is not None:
        m = jnp.maximum(m, extra_col)
    es = [jnp.exp(p - m) for p in parts]
    den = es[0].sum(axis=1, keepdims=True)
    for e in es[1:]:
        den = den + e.sum(axis=1, keepdims=True)
    if extra_col is not None:
        den = den + jnp.exp(extra_col - m)
    inv = 1.0 / den
    return [e * inv for e in es]


WIN_UNROLL = 4


def _win_body(sink_ref, q_ref, k_ref, v_ref, o_ref, *, seq):
    kv = pl.program_id(1)
    nb = seq // A_BLOCK
    scale = HEAD_DIM ** -0.5
    rows = A_GROUP * A_BLOCK
    qi = lax.broadcasted_iota(jnp.int32, (rows, A_BLOCK), 0) % A_BLOCK
    ci = lax.broadcasted_iota(jnp.int32, (rows, A_BLOCK), 1)
    sink_b = jnp.concatenate(
        [jnp.full((A_BLOCK, HEAD_DIM), sink_ref[kv * A_GROUP + g], F32) for g in range(A_GROUP)], axis=0)

    def scores(n):
        r0 = pl.multiple_of(n * A_BLOCK, A_BLOCK)
        rp = pl.multiple_of(jnp.maximum(n - 1, 0) * A_BLOCK, A_BLOCK)
        rn = pl.multiple_of(jnp.minimum(n + 1, nb - 1) * A_BLOCK, A_BLOCK)
        off_p = jnp.where(n > 0, 0, 2 * A_BLOCK)
        off_n = jnp.where(n < nb - 1, 0, 2 * A_BLOCK)
        q = jnp.concatenate(
            [q_ref[pl.ds(r0, A_BLOCK), g * HEAD_DIM:(g + 1) * HEAD_DIM] for g in range(A_GROUP)],
            axis=0)
        sp = lax.dot_general(q, k_ref[pl.ds(rp, A_BLOCK), :], _NT, preferred_element_type=F32) * scale
        sc = lax.dot_general(q, k_ref[pl.ds(r0, A_BLOCK), :], _NT, preferred_element_type=F32) * scale
        sn = lax.dot_general(q, k_ref[pl.ds(rn, A_BLOCK), :], _NT, preferred_element_type=F32) * scale
        sp = jnp.where(ci >= qi + off_p, sp, NEG)
        sn = jnp.where(ci <= qi - off_n, sn, NEG)
        return (rp, r0, rn), (sp, sc, sn)

    def exps(parts):
        m = jnp.maximum(jnp.maximum(parts[0], parts[1]), parts[2]).max(axis=1, keepdims=True)
        m = jnp.maximum(jnp.broadcast_to(m, sink_b.shape), sink_b)
        es = [jnp.exp(p - m) for p in parts]
        den = (es[0] + es[1] + es[2]).sum(axis=1, keepdims=True)
        den = jnp.broadcast_to(den, sink_b.shape) + jnp.exp(sink_b - m)
        return [e.astype(BF16) for e in es], 1.0 / den

    def body(it, carry):
        blocks = [scores(it * WIN_UNROLL + u) for u in range(WIN_UNROLL)]
        probs = [exps(parts) for _, parts in blocks]
        for (rows_kv, _), (es, inv) in zip(blocks, probs):
            o = jnp.dot(es[0], v_ref[pl.ds(rows_kv[0], A_BLOCK), :], preferred_element_type=F32)
            o = o + jnp.dot(es[1], v_ref[pl.ds(rows_kv[1], A_BLOCK), :], preferred_element_type=F32)
            o = o + jnp.dot(es[2], v_ref[pl.ds(rows_kv[2], A_BLOCK), :], preferred_element_type=F32)
            o = o * inv
            for g in range(A_GROUP):
                o_ref[pl.ds(rows_kv[1], A_BLOCK), g * HEAD_DIM:(g + 1) * HEAD_DIM] = (
                    o[g * A_BLOCK:(g + 1) * A_BLOCK].astype(o_ref.dtype))
        return carry

    lax.fori_loop(0, nb // WIN_UNROLL, body, 0)


def window_attention(qk, vqkv, sink, batch, seq):
    gw = A_GROUP * HEAD_DIM
    k_blk0 = QA_W // HEAD_DIM
    return pl.pallas_call(
        functools.partial(_win_body, seq=seq),
        grid=(batch, A_KV_HEADS),
        in_specs=[pl.BlockSpec(memory_space=pltpu.SMEM),
                  pl.BlockSpec((seq, gw), lambda b, h: (b, h)),
                  pl.BlockSpec((seq, HEAD_DIM), lambda b, h: (b, k_blk0 + h)),
                  pl.BlockSpec((seq, HEAD_DIM), lambda b, h: (b, h))],
        out_specs=pl.BlockSpec((seq, gw), lambda b, h: (b, h)),
        out_shape=jax.ShapeDtypeStruct((batch * seq, QA_W), BF16),
        compiler_params=_params(("arbitrary", "arbitrary"), 32),
        name="window_attention",
    )(sink, qk, qk, vqkv)


NBR_HG = 4
NBR_ROWS = 4


def _nbr_body(q_ref, k_ref, v_ref, bias_ref, o_ref, *, seq):
    rows = seq // GRID_W
    kh = min(NA_KH_MAX, rows)
    strip = kh * GRID_W
    scale = HEAD_DIM ** -0.5

    def body(it, carry):
        units = []
        for rr in range(NBR_ROWS):
            r = it * NBR_ROWS + rr
            rs = jnp.clip(r - kh // 2, 0, rows - kh)
            q0 = pl.multiple_of(r * GRID_W, GRID_W)
            k0 = pl.multiple_of(rs * GRID_W, GRID_W)
            for h in range(NBR_HG):
                units.append((q0, k0, r - rs, h, slice(h * HEAD_DIM, (h + 1) * HEAD_DIM)))
        ss = [lax.dot_general(q_ref[pl.ds(q0, GRID_W), cols], k_ref[pl.ds(k0, strip), cols], _NT,
                              preferred_element_type=F32) * scale + bias_ref[h, var]
              for q0, k0, var, h, cols in units]
        ps = []
        for s in ss:
            e = jnp.exp(s - s.max(axis=1, keepdims=True))
            ps.append((e.astype(BF16), 1.0 / e.sum(axis=1, keepdims=True)))
        for (q0, k0, var, h, cols), (e, inv) in zip(units, ps):
            o = jnp.dot(e, v_ref[pl.ds(k0, strip), cols], preferred_element_type=F32) * inv
            o_ref[pl.ds(q0, GRID_W), cols] = o.astype(o_ref.dtype)
        return carry

    lax.fori_loop(0, rows // NBR_ROWS, body, 0)


def neighbourhood_attention(vqkv, bias_tbl, batch, seq):
    gw = NBR_HG * HEAD_DIM
    q0, k0, v0 = KVA_W // gw, (KVA_W + QB_W) // gw, (KVA_W + 2 * QB_W) // gw
    kh = bias_tbl.shape[1]
    return pl.pallas_call(
        functools.partial(_nbr_body, seq=seq),
        grid=(B_HEADS // NBR_HG, batch),
        in_specs=[pl.BlockSpec((seq, gw), lambda g, b: (b, q0 + g)),
                  pl.BlockSpec((seq, gw), lambda g, b: (b, k0 + g)),
                  pl.BlockSpec((seq, gw), lambda g, b: (b, v0 + g)),
                  pl.BlockSpec((NBR_HG, kh, GRID_W, kh * GRID_W), lambda g, b: (g, 0, 0, 0))],
        out_specs=pl.BlockSpec((seq, gw), lambda g, b: (b, g)),
        out_shape=jax.ShapeDtypeStruct((batch * seq, QB_W), BF16),
        compiler_params=_params(("arbitrary", "arbitrary"), 40),
        name="neighbourhood_attention",
    )(vqkv, vqkv, vqkv, bias_tbl)


def _bias_table_body(rpb_ref, o_ref, *, kh):
    h = pl.program_id(0)
    n_dr, n_dc = 2 * NA_KH_MAX - 1, 2 * NA_KW - 1
    c = lax.broadcasted_iota(jnp.int32, (GRID_W, LANES), 0)
    lane = lax.broadcasted_iota(jnp.int32, (GRID_W, LANES), 1)
    kc = lane % GRID_W
    diff = jnp.clip(kc - c + NA_KW - 1, 0, n_dc - 1)
    cs = jnp.clip(c - NA_KW // 2, 0, GRID_W - NA_KW)
    col_ok = (kc >= cs) & (kc < cs + NA_KW)
    slabs = []
    for dr in range(n_dr):
        acc = jnp.zeros((GRID_W, LANES), F32)
        for d in range(n_dc):
            acc = jnp.where(diff == d, rpb_ref[(h * n_dr + dr) * n_dc + d], acc)
        slabs.append(jnp.where(col_ok, acc, NEG))
    left = lane < GRID_W
    for var in range(kh):
        for jp in range(kh * GRID_W // LANES):
            dr0 = 2 * jp - var + NA_KH_MAX - 1
            o_ref[0, var, :, jp * LANES:(jp + 1) * LANES] = jnp.where(left, slabs[dr0], slabs[dr0 + 1])


def _nbr_bias_table(rpb, seq):
    rows = seq // GRID_W
    kh = min(NA_KH_MAX, rows)
    heads = rpb.shape[0]
    assert kh == NA_KH_MAX and 2 * GRID_W == LANES
    return pl.pallas_call(
        functools.partial(_bias_table_body, kh=kh),
        grid=(heads,),
        in_specs=[pl.BlockSpec(memory_space=pltpu.SMEM)],
        out_specs=pl.BlockSpec((1, kh, GRID_W, kh * GRID_W), lambda h: (h, 0, 0, 0)),
        out_shape=jax.ShapeDtypeStruct((heads, kh, GRID_W, kh * GRID_W), F32),
        compiler_params=_params(("arbitrary",), 16),
        name="nbr_bias_table",
    )(rpb.astype(F32).reshape(-1))


def _merge_body(oa_ref, ob_ref, wa_ref, wb_ref, g0_ref, g1_ref, o_ref, wa_s, wb_s, *, k_rows):
    @pl.when(pl.program_id(1) == 0)
    def _():
        _cast_rows(wa_ref, wa_s, k_rows)
        _cast_rows(wb_ref, wb_s, k_rows)

    for rows in _row_parts(oa_ref.shape[0]):
        ya = jnp.dot(oa_ref[rows, :], wa_s[...], preferred_element_type=F32)
        yb = jnp.dot(ob_ref[rows, :], wb_s[...], preferred_element_type=F32)
        o_ref[rows, :] = (g0_ref[rows, :].astype(F32) * ya
                          + g1_ref[rows, :].astype(F32) * yb).astype(o_ref.dtype)


def branch_merge(oa, ob, wa, wb, gates, bm=1024, bn=512):
    m, k = oa.shape
    n = wa.shape[1]
    g1_off = n // bn
    return pl.pallas_call(
        functools.partial(_merge_body, k_rows=k),
        grid=(n // bn, m // bm),
        in_specs=[pl.BlockSpec((bm, k), lambda j, i: (i, 0)),
                  pl.BlockSpec((bm, k), lambda j, i: (i, 0)),
                  pl.BlockSpec((k, bn), lambda j, i: (0, j)),
                  pl.BlockSpec((k, bn), lambda j, i: (0, j)),
                  pl.BlockSpec((bm, bn), lambda j, i: (i, j)),
                  pl.BlockSpec((bm, bn), lambda j, i: (i, j + g1_off))],
        out_specs=pl.BlockSpec((bm, bn), lambda j, i: (i, j)),
        out_shape=jax.ShapeDtypeStruct((m, n), BF16),
        scratch_shapes=[pltpu.VMEM((k, bn), BF16), pltpu.VMEM((k, bn), BF16)],
        compiler_params=_params(("arbitrary", "arbitrary"), 56),
        name="branch_merge",
    )(oa, ob, wa, wb, gates, gates)


def _cast_body(x_ref, o_ref):
    o_ref[...] = x_ref[...].astype(o_ref.dtype)


def cast_bf16(w):
    r, c = w.shape
    return pl.pallas_call(
        _cast_body,
        grid=(1,),
        in_specs=[pl.BlockSpec((r, c), lambda i: (0, 0))],
        out_specs=pl.BlockSpec((r, c), lambda i: (0, 0)),
        out_shape=jax.ShapeDtypeStruct((r, c), BF16),
        compiler_params=_params(("arbitrary",), 40),
        name="cast_bf16",
    )(w)


def _router_probs(hn, wr_ref):
    hi = hn.astype(BF16)
    lo = (hn - hi.astype(F32)).astype(BF16)
    l_hi = jnp.dot(hi, wr_ref[...], preferred_element_type=F32)
    l_lo = jnp.dot(lo, wr_ref[...], preferred_element_type=F32)
    logits = l_hi + pltpu.roll(l_hi, LANES - N_EXPERTS, 1) + l_lo
    lane = lax.broadcasted_iota(jnp.int32, logits.shape, 1)
    logits = jnp.where(lane < N_EXPERTS, logits, NEG)
    (aff,) = _softmax_parts([logits])
    return aff


def _xblock_body(x_ref, gc_ref, wq_ref, k_ref, v_ref, wo_ref, gf_ref, wr_ref,
                 x2_ref, h3_ref, aff_ref):
    scale = HEAD_DIM ** -0.5
    half = x_ref.shape[1] // 2
    parts = [slice(p * XB_ROWS, (p + 1) * XB_ROWS) for p in range(x_ref.shape[0] // XB_ROWS)]
    h2s = [_rmsnorm_rows(x_ref[r, :], gc_ref[...]).astype(BF16) for r in parts]
    qs = [jnp.dot(h2, wq_ref[...], preferred_element_type=F32).astype(BF16) for h2 in h2s]
    os = []
    for q in qs:
        heads = []
        for h in range(X_HEADS):
            cols = slice(h * HEAD_DIM, (h + 1) * HEAD_DIM)
            s = lax.dot_general(q[:, cols], k_ref[:, cols], _NT, preferred_element_type=F32) * scale
            (p,) = _softmax_parts([s])
            heads.append(jnp.dot(p.astype(BF16), v_ref[:, cols],
                                 preferred_element_type=F32).astype(BF16))
        os.append(jnp.concatenate(heads, axis=1))
    for r, o in zip(parts, os):
        x2_ref[r, :] = x_ref[r, :] + jnp.dot(o, wo_ref[...], preferred_element_type=F32)
    for r in parts:
        hn = _rmsnorm_rows(x2_ref[r, :], gf_ref[...])
        packed = pltpu.pack_elementwise([hn[:, :half], hn[:, half:]], packed_dtype=BF16)
        h3_ref[r, :] = pltpu.bitcast(packed, jnp.uint32)
        aff = _router_probs(hn, wr_ref)
        aff_ref[0, :, r] = aff.T[:N_EXPERTS, :]


XB_ROWS = 256


def cross_attention_block(x1, g_cross, wq, kx, vx, wo, g_ffn, w_router, batch, seq, mem_len, bm=512):
    m, d = x1.shape
    nt = seq // bm
    w_hi = w_router.astype(BF16)
    w_lo = (w_router - w_hi.astype(F32)).astype(BF16)
    wr = jnp.concatenate([w_hi, w_lo, jnp.zeros((d, LANES - 2 * N_EXPERTS), BF16)], axis=1)
    const = lambda i: (0, 0)
    once = pl.Buffered(1)
    return pl.pallas_call(
        _xblock_body,
        grid=(m // bm,),
        in_specs=[pl.BlockSpec((bm, d), lambda i: (i, 0)),
                  pl.BlockSpec((1, d), const),
                  pl.BlockSpec((d, X_W), const, pipeline_mode=once),
                  pl.BlockSpec((mem_len, X_W), lambda i: (i // nt, 0)),
                  pl.BlockSpec((mem_len, X_W), lambda i: (i // nt, 0)),
                  pl.BlockSpec((X_W, d), const, pipeline_mode=once),
                  pl.BlockSpec((1, d), const),
                  pl.BlockSpec((d, LANES), const, pipeline_mode=once)],
        out_specs=[pl.BlockSpec((bm, d), lambda i: (i, 0)),
                   pl.BlockSpec((bm, d // 2), lambda i: (i, 0)),
                   pl.BlockSpec((1, N_EXPERTS, bm), lambda i: (i // nt, 0, i % nt))],
        out_shape=[jax.ShapeDtypeStruct((m, d), F32),
                   jax.ShapeDtypeStruct((m, d // 2), jnp.uint32),
                   jax.ShapeDtypeStruct((batch, N_EXPERTS, seq), F32)],
        compiler_params=_params(("arbitrary",), 58),
        name="cross_attention_block",
    )(x1, g_cross.reshape(1, d), cast_bf16(wq), kx, vx, cast_bf16(wo), g_ffn.reshape(1, d), wr)


CUM_CHUNK = 256


def _excl_cumsum_lanes(x01, tri):
    n = x01.shape[1]
    carry = jnp.zeros((x01.shape[0], 1), F32)
    out = []
    for c in range(n // CUM_CHUNK):
        xc = x01[:, c * CUM_CHUNK:(c + 1) * CUM_CHUNK]
        out.append(jnp.dot(xc.astype(BF16), tri, preferred_element_type=F32) + carry)
        carry = carry + xc.sum(axis=1, keepdims=True)
    return jnp.concatenate(out, axis=1)


COMBINE_TS = 256
COMBINE_W = 64


def _topk_body(aff_ref, slot_ref, first_ref, *, cap):
    a = aff_ref[0]
    n_exp, s = a.shape
    capf = jnp.float32(cap)
    lane_s = lax.broadcasted_iota(jnp.int32, (n_exp, s), 1)

    def count(mask):
        return jnp.where(mask, 1.0, 0.0).sum(axis=1, keepdims=True)

    bits = pltpu.bitcast(a, jnp.int32)

    def search(i, t):
        cand = t | jnp.left_shift(jnp.int32(1), 30 - i)
        return jnp.where(count(bits >= cand) >= capf, cand, t)

    t = lax.fori_loop(0, 31, search, jnp.zeros((n_exp, 1), jnp.int32))
    at = jnp.where(bits == t, lane_s, s).astype(F32).min(axis=1, keepdims=True).astype(jnp.int32)
    pivot = jnp.where(lane_s == at, a, 0.0).sum(axis=1, keepdims=True)

    def stats(p):
        return p, count(a > p), count(a >= p)

    def wrong(state):
        _, n_gt, n_ge = state
        return jnp.where(jnp.logical_or(n_gt >= capf, n_ge < capf), 1.0, 0.0).sum() > 0.0

    def step(state):
        p, n_gt, n_ge = state
        up = jnp.where(a > p, a, jnp.inf).min(axis=1, keepdims=True)
        down = jnp.where(a < p, a, -jnp.inf).max(axis=1, keepdims=True)
        return stats(jnp.where(n_gt >= capf, up, jnp.where(n_ge < capf, down, p)))

    pivot, n_gt, _ = lax.while_loop(wrong, step, stats(pivot))
    ri = lax.broadcasted_iota(jnp.int32, (CUM_CHUNK, CUM_CHUNK), 0)
    cj = lax.broadcasted_iota(jnp.int32, (CUM_CHUNK, CUM_CHUNK), 1)
    tri = jnp.where(ri < cj, 1.0, 0.0).astype(BF16)
    eq = jnp.where(a == pivot, 1.0, 0.0)
    sel = jnp.where(a > pivot, 1.0, 0.0) + jnp.where(_excl_cumsum_lanes(eq, tri) < capf - n_gt, eq, 0.0)
    pos = _excl_cumsum_lanes(sel, tri)
    slot_ref[0] = jnp.where(sel > 0.5, pos, -1.0).astype(jnp.int32)
    lane_k = lax.broadcasted_iota(jnp.int32, (n_exp, LANES), 1)
    first = jnp.zeros((n_exp, LANES), F32)
    for k in range(s // COMBINE_TS):
        at_k = jnp.where(lane_s == k * COMBINE_TS, pos, 0.0).sum(axis=1, keepdims=True)
        first = jnp.where(lane_k == k, at_k, first)
    first_ref[0] = first.astype(jnp.int32)


def expert_topk(aff_t, cap):
    b, e, s = aff_t.shape
    return pl.pallas_call(
        functools.partial(_topk_body, cap=cap),
        grid=(b,),
        in_specs=[pl.BlockSpec((1, e, s), lambda i: (i, 0, 0))],
        out_specs=[pl.BlockSpec((1, e, s), lambda i: (i, 0, 0)),
                   pl.BlockSpec((1, e, LANES), lambda i: (i, 0, 0))],
        out_shape=[jax.ShapeDtypeStruct((b, e, s), jnp.int32),
                   jax.ShapeDtypeStruct((b, e, LANES), jnp.int32)],
        compiler_params=_params(("arbitrary",), 32),
        name="expert_topk",
    )(aff_t)


def _slot_index_body(slot_ref, aff_ref, idx_ref, val_ref, *, cap):
    b = pl.program_id(0)
    n_exp, s = slot_ref.shape[1], slot_ref.shape[2]
    ci = lax.broadcasted_iota(jnp.int32, (cap, s), 0)
    tok = lax.broadcasted_iota(jnp.int32, (cap, s), 1).astype(F32)
    base = (b * s).astype(F32)
    for e in range(n_exp):
        hit = slot_ref[0, e:e + 1, :] == ci
        idx = jnp.where(hit, tok, 0.0).sum(axis=1, keepdims=True) + base
        idx_ref[e] = idx.astype(jnp.int32)
        val_ref[e] = jnp.where(hit, aff_ref[0, e:e + 1, :], 0.0).sum(axis=1, keepdims=True)


def slot_index(slot, aff_t, cap):
    b, e, s = slot.shape
    return pl.pallas_call(
        functools.partial(_slot_index_body, cap=cap),
        grid=(b,),
        in_specs=[pl.BlockSpec((1, e, s), lambda bi: (bi, 0, 0)),
                  pl.BlockSpec((1, e, s), lambda bi: (bi, 0, 0))],
        out_specs=[pl.BlockSpec((e, cap, 1), lambda bi: (0, bi, 0)),
                   pl.BlockSpec((e, cap, 1), lambda bi: (0, bi, 0))],
        out_shape=[jax.ShapeDtypeStruct((e, b * cap, 1), jnp.int32),
                   jax.ShapeDtypeStruct((e, b * cap, 1), F32)],
        compiler_params=_params(("arbitrary",), 32),
        name="slot_index",
    )(slot, aff_t)


UNPACK_ROWS = 128


def _row_copy(idx_ref, hp_ref, gbuf, sem, expert, rows, row):
    tok = idx_ref[expert * rows + row]
    return pltpu.make_async_copy(hp_ref.at[pl.ds(tok, 1)], gbuf.at[pl.ds(row, 1)], sem)


def _expert_up_body(idx_ref, hp_ref, wg_ref, wu_ref, o_ref, gbuf, xbf, sem, *, rows, per_step):
    e, f = pl.program_id(0), pl.program_id(1)
    n_e, n_f = pl.num_programs(0), pl.num_programs(1)
    half = gbuf.shape[1]

    def wait_all_rows():
        pltpu.make_async_copy(hp_ref.at[pl.ds(0, rows)], gbuf, sem).wait()

    @pl.when(jnp.logical_and(e == 0, f == 0))
    def _():
        def body(r, carry):
            _row_copy(idx_ref, hp_ref, gbuf, sem, 0, rows, r).start()
            return carry
        lax.fori_loop(0, rows, body, 0)

    @pl.when(f == 0)
    def _():
        wait_all_rows()

        def unpack(k, carry):
            r = pl.multiple_of(k * UNPACK_ROWS, UNPACK_ROWS)
            w = gbuf[pl.ds(r, UNPACK_ROWS), :]
            lo = pltpu.unpack_elementwise(w, index=0, packed_dtype=BF16, unpacked_dtype=F32)
            hi = pltpu.unpack_elementwise(w, index=1, packed_dtype=BF16, unpacked_dtype=F32)
            xbf[pl.ds(r, UNPACK_ROWS), :half] = lo.astype(BF16)
            xbf[pl.ds(r, UNPACK_ROWS), half:] = hi.astype(BF16)
            return carry
        lax.fori_loop(0, rows // UNPACK_ROWS, unpack, 0)

    nxt = jnp.minimum(e + 1, n_e - 1)
    for r in range(per_step):
        _row_copy(idx_ref, hp_ref, gbuf, sem, nxt, rows, f * per_step + r).start()

    wg, wu = wg_ref[0].astype(BF16), wu_ref[0].astype(BF16)
    for part in _row_parts(rows):
        a = jnp.dot(xbf[part, :], wg, preferred_element_type=F32)
        u = jnp.dot(xbf[part, :], wu, preferred_element_type=F32)
        o_ref[0, part, :] = (jax.nn.silu(a) * u).astype(o_ref.dtype)

    @pl.when(jnp.logical_and(e == n_e - 1, f == n_f - 1))
    def _():
        wait_all_rows()


def expert_up(idx, hp, w_gate, w_up, rows, tf=256):
    n_e, d, f = w_gate.shape
    assert hp.shape[1] * 2 == d and rows % (f // tf) == 0
    grid_spec = pltpu.PrefetchScalarGridSpec(
        num_scalar_prefetch=1,
        grid=(n_e, f // tf),
        in_specs=[pl.BlockSpec(memory_space=pl.ANY),
                  pl.BlockSpec((1, d, tf), lambda ei, fi, idx_ref: (ei, 0, fi)),
                  pl.BlockSpec((1, d, tf), lambda ei, fi, idx_ref: (ei, 0, fi))],
        out_specs=pl.BlockSpec((1, rows, tf), lambda ei, fi, idx_ref: (ei, 0, fi)),
        scratch_shapes=[pltpu.VMEM((rows, d // 2), jnp.uint32),
                        pltpu.VMEM((rows, d), BF16),
                        pltpu.SemaphoreType.DMA(())],
    )
    return pl.pallas_call(
        functools.partial(_expert_up_body, rows=rows, per_step=rows // (f // tf)),
        grid_spec=grid_spec,
        out_shape=jax.ShapeDtypeStruct((n_e, rows, f), BF16),
        compiler_params=_params(("arbitrary", "arbitrary"), 56),
        name="expert_up",
    )(idx, hp, w_gate, w_up)


def _expert_down_body(h_ref, wd_ref, val_ref, o_ref):
    wd = wd_ref[0].astype(BF16)
    for part in _row_parts(h_ref.shape[1]):
        y = jnp.dot(h_ref[0, part, :], wd, preferred_element_type=F32)
        o_ref[0, part, :] = (y * val_ref[0, part, :]).astype(o_ref.dtype)


def expert_down(hmid, w_down, valc, tn=1024):
    e, rows, f = hmid.shape
    d = w_down.shape[-1]
    return pl.pallas_call(
        _expert_down_body,
        grid=(e, d // tn),
        in_specs=[pl.BlockSpec((1, rows, f), lambda ei, ni: (ei, 0, 0)),
                  pl.BlockSpec((1, f, tn), lambda ei, ni: (ei, 0, ni)),
                  pl.BlockSpec((1, rows, 1), lambda ei, ni: (ei, 0, 0))],
        out_specs=pl.BlockSpec((1, rows, tn), lambda ei, ni: (ei, 0, ni)),
        out_shape=jax.ShapeDtypeStruct((e, rows, d), BF16),
        compiler_params=_params(("arbitrary", "arbitrary"), 48),
        name="expert_down",
    )(hmid, w_down, valc)


COMBINE_TN = 512
ROW_ALIGN_BF16 = 16


def _combine_windows(first, cap, n_rows):
    b, e, nt = first.shape
    row0 = jnp.arange(b, dtype=jnp.int32)[:, None, None] * cap
    lo = first + row0
    hi = jnp.concatenate([first[:, :, 1:], jnp.full((b, e, 1), cap, jnp.int32)], axis=2) + row0
    start = jnp.minimum(lo // ROW_ALIGN_BF16 * ROW_ALIGN_BF16, n_rows - COMBINE_W)
    fast = jnp.all(hi - start <= COMBINE_W, axis=1)
    return start.transpose(0, 2, 1).reshape(-1), fast.astype(jnp.int32).reshape(-1)


def _combine_body(win_ref, fast_ref, slot_ref, y_hbm, x_ref, g_ref, o_ref,
                  ywin, oht, ybuf, ohs, wsem, ssem, *, cap, n_tiles):
    i = pl.program_id(0)
    n_exp = slot_ref.shape[1]
    ts, d = o_ref.shape
    w = COMBINE_W
    row0 = (i // n_tiles) * cap
    buf = i % 2

    def window_copy(step, e, b):
        start = pl.multiple_of(win_ref[step * n_exp + e], ROW_ALIGN_BF16)
        return pltpu.make_async_copy(y_hbm.at[e, pl.ds(start, w)], ywin.at[b, pl.ds(e * w, w)],
                                     wsem.at[b])

    @pl.when(i == 0)
    def _():
        for e in range(n_exp):
            window_copy(0, e, 0).start()

    @pl.when(i + 1 < pl.num_programs(0))
    def _():
        for e in range(n_exp):
            window_copy(i + 1, e, 1 - buf).start()

    pltpu.make_async_copy(y_hbm.at[0, pl.ds(0, n_exp * w)], ywin.at[buf], wsem.at[buf]).wait()

    @pl.when(fast_ref[i] == 1)
    def _():
        row = lax.broadcasted_iota(jnp.int32, (2 * w, ts), 0)
        upper = row >= w
        j = row % w
        for p in range(n_exp // 2):
            rel_a = slot_ref[0, 2 * p:2 * p + 1, :] + (row0 - win_ref[i * n_exp + 2 * p])
            rel_b = slot_ref[0, 2 * p + 1:2 * p + 2, :] + (row0 - win_ref[i * n_exp + 2 * p + 1])
            hit = jnp.where(upper, rel_b, rel_a) == j
            oht[:, p * 2 * w:(p + 1) * 2 * w] = jnp.where(hit, 1.0, 0.0).T.astype(BF16)
        for c in range(d // COMBINE_TN):
            cols = slice(c * COMBINE_TN, (c + 1) * COMBINE_TN)
            o_ref[:, cols] = x_ref[:, cols] + jnp.dot(oht[...], ywin[buf, :, cols],
                                                      preferred_element_type=F32)

    @pl.when(fast_ref[i] == 0)
    def _():
        o_ref[...] = x_ref[...]
        ci = lax.broadcasted_iota(jnp.int32, (cap, ts), 0)

        def body(e, carry):
            cp = pltpu.make_async_copy(y_hbm.at[e, pl.ds(pl.multiple_of(row0, cap), cap)], ybuf, ssem)
            cp.start()
            cp.wait()
            srow = slot_ref[0, pl.ds(e, 1), :]
            ohs[...] = jnp.where(srow == ci, 1.0, 0.0).T.astype(BF16)
            for c in range(d // COMBINE_TN):
                cols = slice(c * COMBINE_TN, (c + 1) * COMBINE_TN)
                o_ref[:, cols] += jnp.dot(ohs[...], ybuf[:, cols], preferred_element_type=F32)
            return carry
        lax.fori_loop(0, n_exp, body, 0)

    o_ref[...] = _rmsnorm_rows(o_ref[...], g_ref[...])


def expert_combine(slot, first, y, x2d, g, cap):
    b, e, s = slot.shape
    d = x2d.shape[-1]
    ts, w = COMBINE_TS, COMBINE_W
    nt = s // ts
    assert e % 2 == 0 and y.shape[1] >= e * w and cap % ROW_ALIGN_BF16 == 0
    win, fast = _combine_windows(first[:, :, :nt], cap, y.shape[1])
    grid_spec = pltpu.PrefetchScalarGridSpec(
        num_scalar_prefetch=2,
        grid=(b * nt,),
        in_specs=[pl.BlockSpec((1, e, ts), lambda i, win_ref, fast_ref: (i // nt, 0, i % nt)),
                  pl.BlockSpec(memory_space=pl.ANY),
                  pl.BlockSpec((ts, d), lambda i, win_ref, fast_ref: (i, 0)),
                  pl.BlockSpec((1, d), lambda i, win_ref, fast_ref: (0, 0))],
        out_specs=pl.BlockSpec((ts, d), lambda i, win_ref, fast_ref: (i, 0)),
        scratch_shapes=[pltpu.VMEM((2, e * w, d), BF16),
                        pltpu.VMEM((ts, e * w), BF16),
                        pltpu.VMEM((cap, d), BF16),
                        pltpu.VMEM((ts, cap), BF16),
                        pltpu.SemaphoreType.DMA((2,)),
                        pltpu.SemaphoreType.DMA(())],
    )
    return pl.pallas_call(
        functools.partial(_combine_body, cap=cap, n_tiles=nt),
        grid_spec=grid_spec,
        out_shape=jax.ShapeDtypeStruct(x2d.shape, F32),
        compiler_params=_params(("arbitrary",), 48),
        name="expert_combine",
    )(win, fast, slot, y, x2d, g.reshape(1, d))


def _rotary_tables(seq):
    half = ROT_DIM // 2
    inv = ROPE_THETA ** (-jnp.arange(half, dtype=F32) * 2.0 / ROT_DIM)
    ang = jnp.arange(seq).astype(F32)[:, None] * inv[None, :]
    cos, sin = jnp.cos(ang), jnp.sin(ang)
    ones = jnp.ones((seq, HEAD_DIM - ROT_DIM), F32)
    zeros = jnp.zeros((seq, HEAD_DIM - ROT_DIM), F32)
    zh = jnp.zeros((seq, half), F32)
    c = jnp.concatenate([cos, cos, ones], axis=1)
    s1 = jnp.concatenate([-sin, zh, zeros], axis=1)
    s2 = jnp.concatenate([zh, sin, zeros], axis=1)
    return c, s1, s2


def kernel(x, mem, norm_mix, w_in, b_gate, sink, rpb, w_branch_a, w_branch_b, w_out,
           norm_cross, norm_mem, wq_x, wk_x, wv_x, wo_x, norm_ffn, w_router,
           w_gate, w_up, w_down, norm_final):
    batch, seq, d = x.shape
    mem_len = mem.shape[1]
    m = batch * seq
    assert norm_mix.shape[0] == 1, "final RMSNorm is fused into the single layer's last kernel"
    cap = EC_CAPACITY * seq // N_EXPERTS
    bm, bn = 1024, 512
    sb = seq // bm
    x0 = x.reshape(m, d)

    h = rmsnorm(x0, norm_mix[0], BF16)
    rot = _rotary_tables(seq)
    rot_specs = [pl.BlockSpec((bm, HEAD_DIM), lambda j, i: (i % sb, 0))] * 3
    qk = matmul(h, w_in[0], col_off=0, n_cols=QA_W + KVA_W, bm=bm, bn=bn, out_dtype=BF16,
                epilogue=_ep_rotary, extras=rot, extra_specs=rot_specs, name="in_proj_rotary")
    vqkv = matmul_rows(h, w_in[0], col_off=QA_W + KVA_W, n_cols=KVA_W + 3 * QB_W, bm=2 * bm, bn=bn // 2,
                       out_dtype=BF16, name="in_proj_plain")
    g_off = QA_W + 2 * KVA_W + 3 * QB_W
    gates = matmul(h, w_in[0], col_off=g_off, n_cols=2 * d, bm=bm, bn=bn, out_dtype=BF16,
                   epilogue=_ep_sigmoid, extras=(b_gate[0].reshape(1, 2 * d),),
                   extra_specs=[pl.BlockSpec((1, bn), lambda j, i: (0, j))], name="in_proj_gates")
    oa = window_attention(qk, vqkv, sink[0], batch, seq)
    ob = neighbourhood_attention(vqkv, _nbr_bias_table(rpb[0], seq), batch, seq)
    merged = branch_merge(oa, ob, w_branch_a[0], w_branch_b[0], gates)
    res_spec = [pl.BlockSpec((bm, bn), lambda j, i: (i, j))]
    x1 = matmul(merged, w_out[0], col_off=0, n_cols=d, bm=bm, bn=bn, out_dtype=F32,
                epilogue=_ep_residual, extras=(x0,), extra_specs=res_spec, name="out_proj")

    mn = rmsnorm(mem.reshape(batch * mem_len, d), norm_mem[0], BF16)
    kx = matmul(mn, wk_x[0], col_off=0, n_cols=X_W, bm=bm, bn=bn, out_dtype=BF16, name="xattn_k")
    vx = matmul(mn, wv_x[0], col_off=0, n_cols=X_W, bm=bm, bn=bn, out_dtype=BF16, name="xattn_v")
    x2, h3p, aff_t = cross_attention_block(
        x1, norm_cross[0], wq_x[0], kx, vx, wo_x[0], norm_ffn[0], w_router[0], batch, seq, mem_len)

    slot, first = expert_topk(aff_t, cap)
    idx, valc = slot_index(slot, aff_t, cap)
    hmid = expert_up(idx.reshape(-1), h3p, w_gate[0], w_up[0], batch * cap)
    y = expert_down(hmid, w_down[0], valc)
    out = expert_combine(slot, first, y, x2, norm_final, cap)
    return out.reshape(batch, seq, d)
```

```python
import functools

import jax
import jax.numpy as jnp
from jax import lax
from jax.experimental import pallas as pl
from jax.experimental.pallas import tpu as pltpu

F32 = jnp.float32
BF16 = jnp.bfloat16

HEAD_DIM = 128
A_HEADS = 16
A_KV_HEADS = 4
A_GROUP = A_HEADS // A_KV_HEADS
WINDOW = 128
A_BLOCK = 128
ROT_DIM = HEAD_DIM // 4
ROPE_THETA = 500000.0
B_HEADS = 16
GRID_W = 64
NA_KH_MAX = 8
NA_KW = 16
X_HEADS = 4
N_EXPERTS = 16
EC_CAPACITY = 2
EPS = 1e-6
NEG = -1e30
LANES = 128
MIB = 1024 * 1024

QA_W = A_HEADS * HEAD_DIM
KVA_W = A_KV_HEADS * HEAD_DIM
QB_W = B_HEADS * HEAD_DIM
X_W = X_HEADS * HEAD_DIM

_NT = (((1,), (1,)), ((), ()))
_TN = (((0,), (0,)), ((), ()))


def _params(semantics, vmem_mib):
    return pltpu.CompilerParams(dimension_semantics=semantics,
                                vmem_limit_bytes=vmem_mib * MIB)


def _rmsnorm_rows(x, g):
    ms = jnp.mean(x * x, axis=-1, keepdims=True)
    return x * lax.rsqrt(ms + EPS) * g


def _rmsnorm_body(x_ref, g_ref, o_ref):
    o_ref[...] = _rmsnorm_rows(x_ref[...], g_ref[...]).astype(o_ref.dtype)


def rmsnorm(x2d, g, out_dtype, bm=256):
    m, d = x2d.shape
    return pl.pallas_call(
        _rmsnorm_body,
        grid=(m // bm,),
        in_specs=[pl.BlockSpec((bm, d), lambda i: (i, 0)),
                  pl.BlockSpec((1, d), lambda i: (0, 0))],
        out_specs=pl.BlockSpec((bm, d), lambda i: (i, 0)),
        out_shape=jax.ShapeDtypeStruct((m, d), out_dtype),
        compiler_params=_params(("arbitrary",), 40),
        name="rmsnorm",
    )(x2d, g.reshape(1, d))


ROW_SPLIT = 4


def _row_parts(n_rows):
    step = n_rows // ROW_SPLIT
    return [slice(p * step, (p + 1) * step) for p in range(ROW_SPLIT)]


def _ep_store(acc, rows, o_ref):
    o_ref[rows, :] = acc.astype(o_ref.dtype)


def _ep_residual(acc, rows, o_ref, r_ref):
    o_ref[rows, :] = (r_ref[rows, :] + acc).astype(o_ref.dtype)


def _ep_sigmoid(acc, rows, o_ref, b_ref):
    o_ref[rows, :] = jax.nn.sigmoid(acc + b_ref[...]).astype(o_ref.dtype)


def _ep_rotary(acc, rows, o_ref, c_ref, s1_ref, s2_ref):
    c, s1, s2 = c_ref[rows, :], s1_ref[rows, :], s2_ref[rows, :]
    half = ROT_DIM // 2
    for h in range(acc.shape[1] // HEAD_DIM):
        a = acc[:, h * HEAD_DIM:(h + 1) * HEAD_DIM]
        r = a * c + pltpu.roll(a, HEAD_DIM - half, 1) * s1 + pltpu.roll(a, half, 1) * s2
        o_ref[rows, h * HEAD_DIM:(h + 1) * HEAD_DIM] = r.astype(o_ref.dtype)


def _mm_rows_body(*refs, n_extra, epilogue):
    a_ref, w_ref = refs[0], refs[1]
    extra = refs[2:2 + n_extra]
    o_ref = refs[2 + n_extra]
    wb = w_ref[...].astype(BF16)
    for rows in _row_parts(a_ref.shape[0]):
        acc = jnp.dot(a_ref[rows, :], wb, preferred_element_type=F32)
        epilogue(acc, rows, o_ref, *extra)


def matmul_rows(a, w, *, col_off, n_cols, bm, bn, out_dtype, epilogue=_ep_store,
                extras=(), extra_specs=(), vmem_mib=56, name="matmul_rows"):
    m, k = a.shape
    off = col_off // bn
    assert col_off % bn == 0 and n_cols % bn == 0 and m % bm == 0
    body = functools.partial(_mm_rows_body, n_extra=len(extras), epilogue=epilogue)
    return pl.pallas_call(
        body,
        grid=(m // bm, n_cols // bn),
        in_specs=[pl.BlockSpec((bm, k), lambda i, j: (i, 0)),
                  pl.BlockSpec((k, bn), lambda i, j: (0, j + off))] + list(extra_specs),
        out_specs=pl.BlockSpec((bm, bn), lambda i, j: (i, j)),
        out_shape=jax.ShapeDtypeStruct((m, n_cols), out_dtype),
        compiler_params=_params(("arbitrary", "arbitrary"), vmem_mib),
        name=name,
    )(a, w, *extras)


def _softmax_parts(parts, extra_col=None):
    m = parts[0].max(axis=1, keepdims=True)
    for p in parts[1:]:
        m = jnp.maximum(m, p.max(axis=1, keepdims=True))
    if extra_col is not None:
        m = jnp.maximum(m, extra_col)
    es = [jnp.exp(p - m) for p in parts]
    den = es[0].sum(axis=1, keepdims=True)
    for e in es[1:]:
        den = den + e.sum(axis=1, keepdims=True)
    if extra_col is not None:
        den = den + jnp.exp(extra_col - m)
    inv = 1.0 / den
    return [e * inv for e in es]


WIN_UNROLL = 4


def _win_body(sink_ref, q_ref, k_ref, v_ref, o_ref, *, seq):
    kv = pl.program_id(1)
    nb = seq // A_BLOCK
    scale = HEAD_DIM ** -0.5
    rows = A_GROUP * A_BLOCK
    qi = lax.broadcasted_iota(jnp.int32, (rows, A_BLOCK), 0) % A_BLOCK
    ci = lax.broadcasted_iota(jnp.int32, (rows, A_BLOCK), 1)
    sink_b = jnp.concatenate(
        [jnp.full((A_BLOCK, HEAD_DIM), sink_ref[kv * A_GROUP + g], F32) for g in range(A_GROUP)], axis=0)

    def scores(n):
        r0 = pl.multiple_of(n * A_BLOCK, A_BLOCK)
        rp = pl.multiple_of(jnp.maximum(n - 1, 0) * A_BLOCK, A_BLOCK)
        rn = pl.multiple_of(jnp.minimum(n + 1, nb - 1) * A_BLOCK, A_BLOCK)
        off_p = jnp.where(n > 0, 0, 2 * A_BLOCK)
        off_n = jnp.where(n < nb - 1, 0, 2 * A_BLOCK)
        q = jnp.concatenate(
            [q_ref[pl.ds(r0, A_BLOCK), g * HEAD_DIM:(g + 1) * HEAD_DIM] for g in range(A_GROUP)],
            axis=0)
        sp = lax.dot_general(q, k_ref[pl.ds(rp, A_BLOCK), :], _NT, preferred_element_type=F32) * scale
        sc = lax.dot_general(q, k_ref[pl.ds(r0, A_BLOCK), :], _NT, preferred_element_type=F32) * scale
        sn = lax.dot_general(q, k_ref[pl.ds(rn, A_BLOCK), :], _NT, preferred_element_type=F32) * scale
        sp = jnp.where(ci >= qi + off_p, sp, NEG)
        sn = jnp.where(ci <= qi - off_n, sn, NEG)
        return (rp, r0, rn), (sp, sc, sn)

    def exps(parts):
        m = jnp.maximum(jnp.maximum(parts[0], parts[1]), parts[2]).max(axis=1, keepdims=True)
        m = jnp.maximum(jnp.broadcast_to(m, sink_b.shape), sink_b)
        es = [jnp.exp(p - m) for p in parts]
        den = (es[0] + es[1] + es[2]).sum(axis=1, keepdims=True)
        den = jnp.broadcast_to(den, sink_b.shape) + jnp.exp(sink_b - m)
        return [e.astype(BF16) for e in es], 1.0 / den

    def body(it, carry):
        blocks = [scores(it * WIN_UNROLL + u) for u in range(WIN_UNROLL)]
        probs = [exps(parts) for _, parts in blocks]
        for (rows_kv, _), (es, inv) in zip(blocks, probs):
            o = jnp.dot(es[0], v_ref[pl.ds(rows_kv[0], A_BLOCK), :], preferred_element_type=F32)
            o = o + jnp.dot(es[1], v_ref[pl.ds(rows_kv[1], A_BLOCK), :], preferred_element_type=F32)
            o = o + jnp.dot(es[2], v_ref[pl.ds(rows_kv[2], A_BLOCK), :], preferred_element_type=F32)
            o = o * inv
            for g in range(A_GROUP):
                o_ref[pl.ds(rows_kv[1], A_BLOCK), g * HEAD_DIM:(g + 1) * HEAD_DIM] = (
                    o[g * A_BLOCK:(g + 1) * A_BLOCK].astype(o_ref.dtype))
        return carry

    lax.fori_loop(0, nb // WIN_UNROLL, body, 0)


def window_attention(qk, vqkv, sink, batch, seq):
    gw = A_GROUP * HEAD_DIM
    k_blk0 = QA_W // HEAD_DIM
    return pl.pallas_call(
        functools.partial(_win_body, seq=seq),
        grid=(batch, A_KV_HEADS),
        in_specs=[pl.BlockSpec(memory_space=pltpu.SMEM),
                  pl.BlockSpec((seq, gw), lambda b, h: (b, h)),
                  pl.BlockSpec((seq, HEAD_DIM), lambda b, h: (b, k_blk0 + h)),
                  pl.BlockSpec((seq, HEAD_DIM), lambda b, h: (b, h))],
        out_specs=pl.BlockSpec((seq, gw), lambda b, h: (b, h)),
        out_shape=jax.ShapeDtypeStruct((batch * seq, QA_W), BF16),
        compiler_params=_params(("arbitrary", "arbitrary"), 32),
        name="window_attention",
    )(sink, qk, qk, vqkv)


NBR_HG = 4
NBR_ROWS = 4


def _nbr_body(q_ref, k_ref, v_ref, bias_ref, o_ref, *, seq):
    rows = seq // GRID_W
    kh = min(NA_KH_MAX, rows)
    strip = kh * GRID_W
    scale = HEAD_DIM ** -0.5

    def body(it, carry):
        units = []
        for rr in range(NBR_ROWS):
            r = it * NBR_ROWS + rr
            rs = jnp.clip(r - kh // 2, 0, rows - kh)
            q0 = pl.multiple_of(r * GRID_W, GRID_W)
            k0 = pl.multiple_of(rs * GRID_W, GRID_W)
            for h in range(NBR_HG):
                units.append((q0, k0, r - rs, h, slice(h * HEAD_DIM, (h + 1) * HEAD_DIM)))
        ss = [lax.dot_general(q_ref[pl.ds(q0, GRID_W), cols], k_ref[pl.ds(k0, strip), cols], _NT,
                              preferred_element_type=F32) * scale + bias_ref[h, var]
              for q0, k0, var, h, cols in units]
        ps = []
        for s in ss:
            e = jnp.exp(s - s.max(axis=1, keepdims=True))
            ps.append((e.astype(BF16), 1.0 / e.sum(axis=1, keepdims=True)))
        for (q0, k0, var, h, cols), (e, inv) in zip(units, ps):
            o = jnp.dot(e, v_ref[pl.ds(k0, strip), cols], preferred_element_type=F32) * inv
            o_ref[pl.ds(q0, GRID_W), cols] = o.astype(o_ref.dtype)
        return carry

    lax.fori_loop(0, rows // NBR_ROWS, body, 0)


def neighbourhood_attention(vqkv, bias_tbl, batch, seq):
    gw = NBR_HG * HEAD_DIM
    q0, k0, v0 = KVA_W // gw, (KVA_W + QB_W) // gw, (KVA_W + 2 * QB_W) // gw
    kh = bias_tbl.shape[1]
    return pl.pallas_call(
        functools.partial(_nbr_body, seq=seq),
        grid=(B_HEADS // NBR_HG, batch),
        in_specs=[pl.BlockSpec((seq, gw), lambda g, b: (b, q0 + g)),
                  pl.BlockSpec((seq, gw), lambda g, b: (b, k0 + g)),
                  pl.BlockSpec((seq, gw), lambda g, b: (b, v0 + g)),
                  pl.BlockSpec((NBR_HG, kh, GRID_W, kh * GRID_W), lambda g, b: (g, 0, 0, 0))],
        out_specs=pl.BlockSpec((seq, gw), lambda g, b: (b, g)),
        out_shape=jax.ShapeDtypeStruct((batch * seq, QB_W), BF16),
        compiler_params=_params(("arbitrary", "arbitrary"), 40),
        name="neighbourhood_attention",
    )(vqkv, vqkv, vqkv, bias_tbl)


def _bias_table_body(rpb_ref, o_ref, *, kh):
    h = pl.program_id(0)
    n_dr, n_dc = 2 * NA_KH_MAX - 1, 2 * NA_KW - 1
    c = lax.broadcasted_iota(jnp.int32, (GRID_W, LANES), 0)
    lane = lax.broadcasted_iota(jnp.int32, (GRID_W, LANES), 1)
    kc = lane % GRID_W
    diff = jnp.clip(kc - c + NA_KW - 1, 0, n_dc - 1)
    cs = jnp.clip(c - NA_KW // 2, 0, GRID_W - NA_KW)
    col_ok = (kc >= cs) & (kc < cs + NA_KW)
    slabs = []
    for dr in range(n_dr):
        acc = jnp.zeros((GRID_W, LANES), F32)
        for d in range(n_dc):
            acc = jnp.where(diff == d, rpb_ref[(h * n_dr + dr) * n_dc + d], acc)
        slabs.append(jnp.where(col_ok, acc, NEG))
    left = lane < GRID_W
    for var in range(kh):
        for jp in range(kh * GRID_W // LANES):
            dr0 = 2 * jp - var + NA_KH_MAX - 1
            o_ref[0, var, :, jp * LANES:(jp + 1) * LANES] = jnp.where(left, slabs[dr0], slabs[dr0 + 1])


def _nbr_bias_table(rpb, seq):
    rows = seq // GRID_W
    kh = min(NA_KH_MAX, rows)
    heads = rpb.shape[0]
    assert kh == NA_KH_MAX and 2 * GRID_W == LANES
    return pl.pallas_call(
        functools.partial(_bias_table_body, kh=kh),
        grid=(heads,),
        in_specs=[pl.BlockSpec(memory_space=pltpu.SMEM)],
        out_specs=pl.BlockSpec((1, kh, GRID_W, kh * GRID_W), lambda h: (h, 0, 0, 0)),
        out_shape=jax.ShapeDtypeStruct((heads, kh, GRID_W, kh * GRID_W), F32),
        compiler_params=_params(("arbitrary",), 16),
        name="nbr_bias_table",
    )(rpb.astype(F32).reshape(-1))


def _merge_body(oa_ref, ob_ref, wa_ref, wb_ref, g0_ref, g1_ref, o_ref):
    wa, wb = wa_ref[...].astype(BF16), wb_ref[...].astype(BF16)
    for rows in _row_parts(oa_ref.shape[0]):
        ya = jnp.dot(oa_ref[rows, :], wa, preferred_element_type=F32)
        yb = jnp.dot(ob_ref[rows, :], wb, preferred_element_type=F32)
        o_ref[rows, :] = (g0_ref[rows, :].astype(F32) * ya
                          + g1_ref[rows, :].astype(F32) * yb).astype(o_ref.dtype)


def branch_merge(oa, ob, wa, wb, gates, bm=2048, bn=256):
    m, k = oa.shape
    n = wa.shape[1]
    g1_off = n // bn
    return pl.pallas_call(
        _merge_body,
        grid=(m // bm, n // bn),
        in_specs=[pl.BlockSpec((bm, k), lambda i, j: (i, 0)),
                  pl.BlockSpec((bm, k), lambda i, j: (i, 0)),
                  pl.BlockSpec((k, bn), lambda i, j: (0, j)),
                  pl.BlockSpec((k, bn), lambda i, j: (0, j)),
                  pl.BlockSpec((bm, bn), lambda i, j: (i, j)),
                  pl.BlockSpec((bm, bn), lambda i, j: (i, j + g1_off))],
        out_specs=pl.BlockSpec((bm, bn), lambda i, j: (i, j)),
        out_shape=jax.ShapeDtypeStruct((m, n), BF16),
        compiler_params=_params(("arbitrary", "arbitrary"), 56),
        name="branch_merge",
    )(oa, ob, wa, wb, gates, gates)


def _cast_body(x_ref, o_ref):
    o_ref[...] = x_ref[...].astype(o_ref.dtype)


def cast_bf16(w):
    r, c = w.shape
    return pl.pallas_call(
        _cast_body,
        grid=(1,),
        in_specs=[pl.BlockSpec((r, c), lambda i: (0, 0))],
        out_specs=pl.BlockSpec((r, c), lambda i: (0, 0)),
        out_shape=jax.ShapeDtypeStruct((r, c), BF16),
        compiler_params=_params(("arbitrary",), 40),
        name="cast_bf16",
    )(w)


def _router_probs(hn, wr_ref):
    hi = hn.astype(BF16)
    lo = (hn - hi.astype(F32)).astype(BF16)
    l_hi = jnp.dot(hi, wr_ref[...], preferred_element_type=F32)
    l_lo = jnp.dot(lo, wr_ref[...], preferred_element_type=F32)
    logits = l_hi + pltpu.roll(l_hi, LANES - N_EXPERTS, 1) + l_lo
    lane = lax.broadcasted_iota(jnp.int32, logits.shape, 1)
    logits = jnp.where(lane < N_EXPERTS, logits, NEG)
    (aff,) = _softmax_parts([logits])
    return aff


def _xblock_body(x_ref, gc_ref, wq_ref, k_ref, v_ref, wo_ref, gf_ref, wr_ref,
                 x2_ref, h3_ref, aff_ref):
    scale = HEAD_DIM ** -0.5
    half = x_ref.shape[1] // 2
    parts = [slice(p * XB_ROWS, (p + 1) * XB_ROWS) for p in range(x_ref.shape[0] // XB_ROWS)]
    h2s = [_rmsnorm_rows(x_ref[r, :], gc_ref[...]).astype(BF16) for r in parts]
    qs = [jnp.dot(h2, wq_ref[...], preferred_element_type=F32).astype(BF16) for h2 in h2s]
    os = []
    for q in qs:
        heads = []
        for h in range(X_HEADS):
            cols = slice(h * HEAD_DIM, (h + 1) * HEAD_DIM)
            s = lax.dot_general(q[:, cols], k_ref[:, cols], _NT, preferred_element_type=F32) * scale
            (p,) = _softmax_parts([s])
            heads.append(jnp.dot(p.astype(BF16), v_ref[:, cols],
                                 preferred_element_type=F32).astype(BF16))
        os.append(jnp.concatenate(heads, axis=1))
    for r, o in zip(parts, os):
        x2_ref[r, :] = x_ref[r, :] + jnp.dot(o, wo_ref[...], preferred_element_type=F32)
    for r in parts:
        hn = _rmsnorm_rows(x2_ref[r, :], gf_ref[...])
        packed = pltpu.pack_elementwise([hn[:, :half], hn[:, half:]], packed_dtype=BF16)
        h3_ref[r, :] = pltpu.bitcast(packed, jnp.uint32)
        aff = _router_probs(hn, wr_ref)
        aff_ref[0, :, r] = aff.T[:N_EXPERTS, :]


XB_ROWS = 256


def cross_attention_block(x1, g_cross, wq, kx, vx, wo, g_ffn, w_router, batch, seq, mem_len, bm=512):
    m, d = x1.shape
    nt = seq // bm
    w_hi = w_router.astype(BF16)
    w_lo = (w_router - w_hi.astype(F32)).astype(BF16)
    wr = jnp.concatenate([w_hi, w_lo, jnp.zeros((d, LANES - 2 * N_EXPERTS), BF16)], axis=1)
    const = lambda i: (0, 0)
    once = pl.Buffered(1)
    return pl.pallas_call(
        _xblock_body,
        grid=(m // bm,),
        in_specs=[pl.BlockSpec((bm, d), lambda i: (i, 0)),
                  pl.BlockSpec((1, d), const),
                  pl.BlockSpec((d, X_W), const, pipeline_mode=once),
                  pl.BlockSpec((mem_len, X_W), lambda i: (i // nt, 0)),
                  pl.BlockSpec((mem_len, X_W), lambda i: (i // nt, 0)),
                  pl.BlockSpec((X_W, d), const, pipeline_mode=once),
                  pl.BlockSpec((1, d), const),
                  pl.BlockSpec((d, LANES), const, pipeline_mode=once)],
        out_specs=[pl.BlockSpec((bm, d), lambda i: (i, 0)),
                   pl.BlockSpec((bm, d // 2), lambda i: (i, 0)),
                   pl.BlockSpec((1, N_EXPERTS, bm), lambda i: (i // nt, 0, i % nt))],
        out_shape=[jax.ShapeDtypeStruct((m, d), F32),
                   jax.ShapeDtypeStruct((m, d // 2), jnp.uint32),
                   jax.ShapeDtypeStruct((batch, N_EXPERTS, seq), F32)],
        compiler_params=_params(("arbitrary",), 58),
        name="cross_attention_block",
    )(x1, g_cross.reshape(1, d), cast_bf16(wq), kx, vx, cast_bf16(wo), g_ffn.reshape(1, d), wr)


CUM_CHUNK = 256


def _excl_cumsum_lanes(x01, tri):
    n = x01.shape[1]
    carry = jnp.zeros((x01.shape[0], 1), F32)
    out = []
    for c in range(n // CUM_CHUNK):
        xc = x01[:, c * CUM_CHUNK:(c + 1) * CUM_CHUNK]
        out.append(jnp.dot(xc.astype(BF16), tri, preferred_element_type=F32) + carry)
        carry = carry + xc.sum(axis=1, keepdims=True)
    return jnp.concatenate(out, axis=1)


COMBINE_TS = 256
COMBINE_W = 64


def _topk_body(aff_ref, slot_ref, first_ref, *, cap):
    a = aff_ref[0]
    n_exp, s = a.shape
    capf = jnp.float32(cap)
    lane_s = lax.broadcasted_iota(jnp.int32, (n_exp, s), 1)

    def count(mask):
        return jnp.where(mask, 1.0, 0.0).sum(axis=1, keepdims=True)

    bits = pltpu.bitcast(a, jnp.int32)

    def search(i, t):
        cand = t | jnp.left_shift(jnp.int32(1), 30 - i)
        return jnp.where(count(bits >= cand) >= capf, cand, t)

    t = lax.fori_loop(0, 31, search, jnp.zeros((n_exp, 1), jnp.int32))
    at = jnp.where(bits == t, lane_s, s).astype(F32).min(axis=1, keepdims=True).astype(jnp.int32)
    pivot = jnp.where(lane_s == at, a, 0.0).sum(axis=1, keepdims=True)

    def stats(p):
        return p, count(a > p), count(a >= p)

    def wrong(state):
        _, n_gt, n_ge = state
        return jnp.where(jnp.logical_or(n_gt >= capf, n_ge < capf), 1.0, 0.0).sum() > 0.0

    def step(state):
        p, n_gt, n_ge = state
        up = jnp.where(a > p, a, jnp.inf).min(axis=1, keepdims=True)
        down = jnp.where(a < p, a, -jnp.inf).max(axis=1, keepdims=True)
        return stats(jnp.where(n_gt >= capf, up, jnp.where(n_ge < capf, down, p)))

    pivot, n_gt, _ = lax.while_loop(wrong, step, stats(pivot))
    ri = lax.broadcasted_iota(jnp.int32, (CUM_CHUNK, CUM_CHUNK), 0)
    cj = lax.broadcasted_iota(jnp.int32, (CUM_CHUNK, CUM_CHUNK), 1)
    tri = jnp.where(ri < cj, 1.0, 0.0).astype(BF16)
    eq = jnp.where(a == pivot, 1.0, 0.0)
    sel = jnp.where(a > pivot, 1.0, 0.0) + jnp.where(_excl_cumsum_lanes(eq, tri) < capf - n_gt, eq, 0.0)
    pos = _excl_cumsum_lanes(sel, tri)
    slot_ref[0] = jnp.where(sel > 0.5, pos, -1.0).astype(jnp.int32)
    lane_k = lax.broadcasted_iota(jnp.int32, (n_exp, LANES), 1)
    first = jnp.zeros((n_exp, LANES), F32)
    for k in range(s // COMBINE_TS):
        at_k = jnp.where(lane_s == k * COMBINE_TS, pos, 0.0).sum(axis=1, keepdims=True)
        first = jnp.where(lane_k == k, at_k, first)
    first_ref[0] = first.astype(jnp.int32)


def expert_topk(aff_t, cap):
    b, e, s = aff_t.shape
    return pl.pallas_call(
        functools.partial(_topk_body, cap=cap),
        grid=(b,),
        in_specs=[pl.BlockSpec((1, e, s), lambda i: (i, 0, 0))],
        out_specs=[pl.BlockSpec((1, e, s), lambda i: (i, 0, 0)),
                   pl.BlockSpec((1, e, LANES), lambda i: (i, 0, 0))],
        out_shape=[jax.ShapeDtypeStruct((b, e, s), jnp.int32),
                   jax.ShapeDtypeStruct((b, e, LANES), jnp.int32)],
        compiler_params=_params(("arbitrary",), 32),
        name="expert_topk",
    )(aff_t)


def _slot_index_body(slot_ref, aff_ref, idx_ref, val_ref, *, cap):
    b = pl.program_id(0)
    n_exp, s = slot_ref.shape[1], slot_ref.shape[2]
    ci = lax.broadcasted_iota(jnp.int32, (cap, s), 0)
    tok = lax.broadcasted_iota(jnp.int32, (cap, s), 1).astype(F32)
    base = (b * s).astype(F32)
    for e in range(n_exp):
        hit = slot_ref[0, e:e + 1, :] == ci
        idx = jnp.where(hit, tok, 0.0).sum(axis=1, keepdims=True) + base
        idx_ref[e] = idx.astype(jnp.int32)
        val_ref[e] = jnp.where(hit, aff_ref[0, e:e + 1, :], 0.0).sum(axis=1, keepdims=True)


def slot_index(slot, aff_t, cap):
    b, e, s = slot.shape
    return pl.pallas_call(
        functools.partial(_slot_index_body, cap=cap),
        grid=(b,),
        in_specs=[pl.BlockSpec((1, e, s), lambda bi: (bi, 0, 0)),
                  pl.BlockSpec((1, e, s), lambda bi: (bi, 0, 0))],
        out_specs=[pl.BlockSpec((e, cap, 1), lambda bi: (0, bi, 0)),
                   pl.BlockSpec((e, cap, 1), lambda bi: (0, bi, 0))],
        out_shape=[jax.ShapeDtypeStruct((e, b * cap, 1), jnp.int32),
                   jax.ShapeDtypeStruct((e, b * cap, 1), F32)],
        compiler_params=_params(("arbitrary",), 32),
        name="slot_index",
    )(slot, aff_t)


UNPACK_ROWS = 128


def _row_copy(idx_ref, hp_ref, gbuf, sem, expert, rows, row):
    tok = idx_ref[expert * rows + row]
    return pltpu.make_async_copy(hp_ref.at[pl.ds(tok, 1)], gbuf.at[pl.ds(row, 1)], sem)


def _expert_up_body(idx_ref, hp_ref, wg_ref, wu_ref, o_ref, gbuf, xbf, sem, *, rows, per_step):
    e, f = pl.program_id(0), pl.program_id(1)
    n_e, n_f = pl.num_programs(0), pl.num_programs(1)
    half = gbuf.shape[1]

    def wait_all_rows():
        pltpu.make_async_copy(hp_ref.at[pl.ds(0, rows)], gbuf, sem).wait()

    @pl.when(jnp.logical_and(e == 0, f == 0))
    def _():
        def body(r, carry):
            _row_copy(idx_ref, hp_ref, gbuf, sem, 0, rows, r).start()
            return carry
        lax.fori_loop(0, rows, body, 0)

    @pl.when(f == 0)
    def _():
        wait_all_rows()

        def unpack(k, carry):
            r = pl.multiple_of(k * UNPACK_ROWS, UNPACK_ROWS)
            w = gbuf[pl.ds(r, UNPACK_ROWS), :]
            lo = pltpu.unpack_elementwise(w, index=0, packed_dtype=BF16, unpacked_dtype=F32)
            hi = pltpu.unpack_elementwise(w, index=1, packed_dtype=BF16, unpacked_dtype=F32)
            xbf[pl.ds(r, UNPACK_ROWS), :half] = lo.astype(BF16)
            xbf[pl.ds(r, UNPACK_ROWS), half:] = hi.astype(BF16)
            return carry
        lax.fori_loop(0, rows // UNPACK_ROWS, unpack, 0)

    nxt = jnp.minimum(e + 1, n_e - 1)
    for r in range(per_step):
        _row_copy(idx_ref, hp_ref, gbuf, sem, nxt, rows, f * per_step + r).start()

    wg, wu = wg_ref[0].astype(BF16), wu_ref[0].astype(BF16)
    for part in _row_parts(rows):
        a = jnp.dot(xbf[part, :], wg, preferred_element_type=F32)
        u = jnp.dot(xbf[part, :], wu, preferred_element_type=F32)
        o_ref[0, part, :] = (jax.nn.silu(a) * u).astype(o_ref.dtype)

    @pl.when(jnp.logical_and(e == n_e - 1, f == n_f - 1))
    def _():
        wait_all_rows()


def expert_up(idx, hp, w_gate, w_up, rows, tf=256):
    n_e, d, f = w_gate.shape
    assert hp.shape[1] * 2 == d and rows % (f // tf) == 0
    grid_spec = pltpu.PrefetchScalarGridSpec(
        num_scalar_prefetch=1,
        grid=(n_e, f // tf),
        in_specs=[pl.BlockSpec(memory_space=pl.ANY),
                  pl.BlockSpec((1, d, tf), lambda ei, fi, idx_ref: (ei, 0, fi)),
                  pl.BlockSpec((1, d, tf), lambda ei, fi, idx_ref: (ei, 0, fi))],
        out_specs=pl.BlockSpec((1, rows, tf), lambda ei, fi, idx_ref: (ei, 0, fi)),
        scratch_shapes=[pltpu.VMEM((rows, d // 2), jnp.uint32),
                        pltpu.VMEM((rows, d), BF16),
                        pltpu.SemaphoreType.DMA(())],
    )
    return pl.pallas_call(
        functools.partial(_expert_up_body, rows=rows, per_step=rows // (f // tf)),
        grid_spec=grid_spec,
        out_shape=jax.ShapeDtypeStruct((n_e, rows, f), BF16),
        compiler_params=_params(("arbitrary", "arbitrary"), 56),
        name="expert_up",
    )(idx, hp, w_gate, w_up)


def _expert_down_body(h_ref, wd_ref, val_ref, o_ref):
    wd = wd_ref[0].astype(BF16)
    for part in _row_parts(h_ref.shape[1]):
        y = jnp.dot(h_ref[0, part, :], wd, preferred_element_type=F32)
        o_ref[0, part, :] = (y * val_ref[0, part, :]).astype(o_ref.dtype)


def expert_down(hmid, w_down, valc, tn=1024):
    e, rows, f = hmid.shape
    d = w_down.shape[-1]
    return pl.pallas_call(
        _expert_down_body,
        grid=(e, d // tn),
        in_specs=[pl.BlockSpec((1, rows, f), lambda ei, ni: (ei, 0, 0)),
                  pl.BlockSpec((1, f, tn), lambda ei, ni: (ei, 0, ni)),
                  pl.BlockSpec((1, rows, 1), lambda ei, ni: (ei, 0, 0))],
        out_specs=pl.BlockSpec((1, rows, tn), lambda ei, ni: (ei, 0, ni)),
        out_shape=jax.ShapeDtypeStruct((e, rows, d), BF16),
        compiler_params=_params(("arbitrary", "arbitrary"), 48),
        name="expert_down",
    )(hmid, w_down, valc)


COMBINE_TN = 512
ROW_ALIGN_BF16 = 16


def _combine_windows(first, cap, n_rows):
    b, e, nt = first.shape
    row0 = jnp.arange(b, dtype=jnp.int32)[:, None, None] * cap
    lo = first + row0
    hi = jnp.concatenate([first[:, :, 1:], jnp.full((b, e, 1), cap, jnp.int32)], axis=2) + row0
    start = jnp.minimum(lo // ROW_ALIGN_BF16 * ROW_ALIGN_BF16, n_rows - COMBINE_W)
    fast = jnp.all(hi - start <= COMBINE_W, axis=1)
    return start.transpose(0, 2, 1).reshape(-1), fast.astype(jnp.int32).reshape(-1)


def _combine_body(win_ref, fast_ref, slot_ref, y_hbm, x_ref, g_ref, o_ref,
                  ywin, oht, ybuf, ohs, wsem, ssem, *, cap, n_tiles):
    i = pl.program_id(0)
    n_exp = slot_ref.shape[1]
    ts, d = o_ref.shape
    w = COMBINE_W
    row0 = (i // n_tiles) * cap
    buf = i % 2

    def window_copy(step, e, b):
        start = pl.multiple_of(win_ref[step * n_exp + e], ROW_ALIGN_BF16)
        return pltpu.make_async_copy(y_hbm.at[e, pl.ds(start, w)], ywin.at[b, pl.ds(e * w, w)],
                                     wsem.at[b])

    @pl.when(i == 0)
    def _():
        for e in range(n_exp):
            window_copy(0, e, 0).start()

    @pl.when(i + 1 < pl.num_programs(0))
    def _():
        for e in range(n_exp):
            window_copy(i + 1, e, 1 - buf).start()

    pltpu.make_async_copy(y_hbm.at[0, pl.ds(0, n_exp * w)], ywin.at[buf], wsem.at[buf]).wait()

    @pl.when(fast_ref[i] == 1)
    def _():
        row = lax.broadcasted_iota(jnp.int32, (2 * w, ts), 0)
        upper = row >= w
        j = row % w
        for p in range(n_exp // 2):
            rel_a = slot_ref[0, 2 * p:2 * p + 1, :] + (row0 - win_ref[i * n_exp + 2 * p])
            rel_b = slot_ref[0, 2 * p + 1:2 * p + 2, :] + (row0 - win_ref[i * n_exp + 2 * p + 1])
            hit = jnp.where(upper, rel_b, rel_a) == j
            oht[:, p * 2 * w:(p + 1) * 2 * w] = jnp.where(hit, 1.0, 0.0).T.astype(BF16)
        for c in range(d // COMBINE_TN):
            cols = slice(c * COMBINE_TN, (c + 1) * COMBINE_TN)
            o_ref[:, cols] = x_ref[:, cols] + jnp.dot(oht[...], ywin[buf, :, cols],
                                                      preferred_element_type=F32)

    @pl.when(fast_ref[i] == 0)
    def _():
        o_ref[...] = x_ref[...]
        ci = lax.broadcasted_iota(jnp.int32, (cap, ts), 0)

        def body(e, carry):
            cp = pltpu.make_async_copy(y_hbm.at[e, pl.ds(pl.multiple_of(row0, cap), cap)], ybuf, ssem)
            cp.start()
            cp.wait()
            srow = slot_ref[0, pl.ds(e, 1), :]
            ohs[...] = jnp.where(srow == ci, 1.0, 0.0).T.astype(BF16)
            for c in range(d // COMBINE_TN):
                cols = slice(c * COMBINE_TN, (c + 1) * COMBINE_TN)
                o_ref[:, cols] += jnp.dot(ohs[...], ybuf[:, cols], preferred_element_type=F32)
            return carry
        lax.fori_loop(0, n_exp, body, 0)

    o_ref[...] = _rmsnorm_rows(o_ref[...], g_ref[...])


def expert_combine(slot, first, y, x2d, g, cap):
    b, e, s = slot.shape
    d = x2d.shape[-1]
    ts, w = COMBINE_TS, COMBINE_W
    nt = s // ts
    assert e % 2 == 0 and y.shape[1] >= e * w and cap % ROW_ALIGN_BF16 == 0
    win, fast = _combine_windows(first[:, :, :nt], cap, y.shape[1])
    grid_spec = pltpu.PrefetchScalarGridSpec(
        num_scalar_prefetch=2,
        grid=(b * nt,),
        in_specs=[pl.BlockSpec((1, e, ts), lambda i, win_ref, fast_ref: (i // nt, 0, i % nt)),
                  pl.BlockSpec(memory_space=pl.ANY),
                  pl.BlockSpec((ts, d), lambda i, win_ref, fast_ref: (i, 0)),
                  pl.BlockSpec((1, d), lambda i, win_ref, fast_ref: (0, 0))],
        out_specs=pl.BlockSpec((ts, d), lambda i, win_ref, fast_ref: (i, 0)),
        scratch_shapes=[pltpu.VMEM((2, e * w, d), BF16),
                        pltpu.VMEM((ts, e * w), BF16),
                        pltpu.VMEM((cap, d), BF16),
                        pltpu.VMEM((ts, cap), BF16),
                        pltpu.SemaphoreType.DMA((2,)),
                        pltpu.SemaphoreType.DMA(())],
    )
    return pl.pallas_call(
        functools.partial(_combine_body, cap=cap, n_tiles=nt),
        grid_spec=grid_spec,
        out_shape=jax.ShapeDtypeStruct(x2d.shape, F32),
        compiler_params=_params(("arbitrary",), 48),
        name="expert_combine",
    )(win, fast, slot, y, x2d, g.reshape(1, d))


def _rotary_tables(seq):
    half = ROT_DIM // 2
    inv = ROPE_THETA ** (-jnp.arange(half, dtype=F32) * 2.0 / ROT_DIM)
    ang = jnp.arange(seq).astype(F32)[:, None] * inv[None, :]
    cos, sin = jnp.cos(ang), jnp.sin(ang)
    ones = jnp.ones((seq, HEAD_DIM - ROT_DIM), F32)
    zeros = jnp.zeros((seq, HEAD_DIM - ROT_DIM), F32)
    zh = jnp.zeros((seq, half), F32)
    c = jnp.concatenate([cos, cos, ones], axis=1)
    s1 = jnp.concatenate([-sin, zh, zeros], axis=1)
    s2 = jnp.concatenate([zh, sin, zeros], axis=1)
    return c, s1, s2


def kernel(x, mem, norm_mix, w_in, b_gate, sink, rpb, w_branch_a, w_branch_b, w_out,
           norm_cross, norm_mem, wq_x, wk_x, wv_x, wo_x, norm_ffn, w_router,
           w_gate, w_up, w_down, norm_final):
    batch, seq, d = x.shape
    mem_len = mem.shape[1]
    m = batch * seq
    assert norm_mix.shape[0] == 1, "final RMSNorm is fused into the single layer's last kernel"
    cap = EC_CAPACITY * seq // N_EXPERTS
    bm, bn = 2048, 256
    sb = seq // bm
    x0 = x.reshape(m, d)

    h = rmsnorm(x0, norm_mix[0], BF16)
    rot = _rotary_tables(seq)
    rot_specs = [pl.BlockSpec((bm, HEAD_DIM), lambda i, j: (i % sb, 0))] * 3
    qk = matmul_rows(h, w_in[0], col_off=0, n_cols=QA_W + KVA_W, bm=bm, bn=bn, out_dtype=BF16,
                     epilogue=_ep_rotary, extras=rot, extra_specs=rot_specs, name="in_proj_rotary")
    vqkv = matmul_rows(h, w_in[0], col_off=QA_W + KVA_W, n_cols=KVA_W + 3 * QB_W, bm=bm, bn=bn,
                       out_dtype=BF16, name="in_proj_plain")
    g_off = QA_W + 2 * KVA_W + 3 * QB_W
    gates = matmul_rows(h, w_in[0], col_off=g_off, n_cols=2 * d, bm=bm, bn=bn, out_dtype=BF16,
                        epilogue=_ep_sigmoid, extras=(b_gate[0].reshape(1, 2 * d),),
                        extra_specs=[pl.BlockSpec((1, bn), lambda i, j: (0, j))], name="in_proj_gates")
    oa = window_attention(qk, vqkv, sink[0], batch, seq)
    ob = neighbourhood_attention(vqkv, _nbr_bias_table(rpb[0], seq), batch, seq)
    merged = branch_merge(oa, ob, w_branch_a[0], w_branch_b[0], gates, bm=bm, bn=bn)
    res_spec = [pl.BlockSpec((bm, bn), lambda i, j: (i, j))]
    x1 = matmul_rows(merged, w_out[0], col_off=0, n_cols=d, bm=bm, bn=bn, out_dtype=F32,
                     epilogue=_ep_residual, extras=(x0,), extra_specs=res_spec, name="out_proj")

    mem_rows = batch * mem_len
    mn = rmsnorm(mem.reshape(mem_rows, d), norm_mem[0], BF16)
    kx = matmul_rows(mn, wk_x[0], col_off=0, n_cols=X_W, bm=mem_rows, bn=bn, out_dtype=BF16, name="xattn_k")
    vx = matmul_rows(mn, wv_x[0], col_off=0, n_cols=X_W, bm=mem_rows, bn=bn, out_dtype=BF16, name="xattn_v")
    x2, h3p, aff_t = cross_attention_block(
        x1, norm_cross[0], wq_x[0], kx, vx, wo_x[0], norm_ffn[0], w_router[0], batch, seq, mem_len)

    slot, first = expert_topk(aff_t, cap)
    idx, valc = slot_index(slot, aff_t, cap)
    hmid = expert_up(idx.reshape(-1), h3p, w_gate[0], w_up[0], batch * cap)
    y = expert_down(hmid, w_down[0], valc)
    out = expert_combine(slot, first, y, x2, norm_final, cap)
    return out.reshape(batch, seq, d)
```

```python
import functools

import jax
import jax.numpy as jnp
from jax import lax
from jax.experimental import pallas as pl
from jax.experimental.pallas import tpu as pltpu

F32 = jnp.float32
BF16 = jnp.bfloat16

HEAD_DIM = 128
A_HEADS = 16
A_KV_HEADS = 4
A_GROUP = A_HEADS // A_KV_HEADS
WINDOW = 128
A_BLOCK = 128
ROT_DIM = HEAD_DIM // 4
ROPE_THETA = 500000.0
B_HEADS = 16
GRID_W = 64
NA_KH_MAX = 8
NA_KW = 16
X_HEADS = 4
N_EXPERTS = 16
EC_CAPACITY = 2
EPS = 1e-6
NEG = -1e30
LOG2E = 1.4426950408889634
LANES = 128
MIB = 1024 * 1024

QA_W = A_HEADS * HEAD_DIM
KVA_W = A_KV_HEADS * HEAD_DIM
QB_W = B_HEADS * HEAD_DIM
X_W = X_HEADS * HEAD_DIM

_NT = (((1,), (1,)), ((), ()))
_TN = (((0,), (0,)), ((), ()))


def _params(semantics, vmem_mib):
    return pltpu.CompilerParams(dimension_semantics=semantics,
                                vmem_limit_bytes=vmem_mib * MIB)


def _rmsnorm_rows(x, g):
    ms = jnp.mean(x * x, axis=-1, keepdims=True)
    return x * lax.rsqrt(ms + EPS) * g


def _rmsnorm_body(x_ref, g_ref, o_ref):
    o_ref[...] = _rmsnorm_rows(x_ref[...], g_ref[...]).astype(o_ref.dtype)


def rmsnorm(x2d, g, out_dtype, bm=256):
    m, d = x2d.shape
    return pl.pallas_call(
        _rmsnorm_body,
        grid=(m // bm,),
        in_specs=[pl.BlockSpec((bm, d), lambda i: (i, 0)),
                  pl.BlockSpec((1, d), lambda i: (0, 0))],
        out_specs=pl.BlockSpec((bm, d), lambda i: (i, 0)),
        out_shape=jax.ShapeDtypeStruct((m, d), out_dtype),
        compiler_params=_params(("arbitrary",), 40),
        name="rmsnorm",
    )(x2d, g.reshape(1, d))


def _sigmoid(x):
    return 0.5 * jnp.tanh(0.5 * x) + 0.5


ROW_SPLIT = 4


def _row_parts(n_rows):
    step = n_rows // ROW_SPLIT
    return [slice(p * step, (p + 1) * step) for p in range(ROW_SPLIT)]


def _ep_store(acc, rows, o_ref):
    o_ref[rows, :] = acc.astype(o_ref.dtype)


def _ep_residual(acc, rows, o_ref, r_ref):
    o_ref[rows, :] = (r_ref[rows, :] + acc).astype(o_ref.dtype)


def _ep_sigmoid(acc, rows, o_ref, b_ref):
    o_ref[rows, :] = _sigmoid(acc + b_ref[...]).astype(o_ref.dtype)


def _ep_rotary(acc, rows, o_ref, c_ref, s1_ref, s2_ref):
    c, s1, s2 = c_ref[rows, :], s1_ref[rows, :], s2_ref[rows, :]
    half = ROT_DIM // 2
    for h in range(acc.shape[1] // HEAD_DIM):
        a = acc[:, h * HEAD_DIM:(h + 1) * HEAD_DIM]
        r = a * c + pltpu.roll(a, HEAD_DIM - half, 1) * s1 + pltpu.roll(a, half, 1) * s2
        o_ref[rows, h * HEAD_DIM:(h + 1) * HEAD_DIM] = r.astype(o_ref.dtype)


def _mm_rows_body(*refs, n_extra, epilogue):
    a_ref, w_ref = refs[0], refs[1]
    extra = refs[2:2 + n_extra]
    o_ref = refs[2 + n_extra]
    wb = w_ref[...].astype(BF16)
    for rows in _row_parts(a_ref.shape[0]):
        acc = jnp.dot(a_ref[rows, :], wb, preferred_element_type=F32)
        epilogue(acc, rows, o_ref, *extra)


def matmul_rows(a, w, *, col_off, n_cols, bm, bn, out_dtype, epilogue=_ep_store,
                extras=(), extra_specs=(), vmem_mib=56, name="matmul_rows"):
    m, k = a.shape
    off = col_off // bn
    assert col_off % bn == 0 and n_cols % bn == 0 and m % bm == 0
    body = functools.partial(_mm_rows_body, n_extra=len(extras), epilogue=epilogue)
    return pl.pallas_call(
        body,
        grid=(m // bm, n_cols // bn),
        in_specs=[pl.BlockSpec((bm, k), lambda i, j: (i, 0)),
                  pl.BlockSpec((k, bn), lambda i, j: (0, j + off))] + list(extra_specs),
        out_specs=pl.BlockSpec((bm, bn), lambda i, j: (i, j)),
        out_shape=jax.ShapeDtypeStruct((m, n_cols), out_dtype),
        compiler_params=_params(("arbitrary", "arbitrary"), vmem_mib),
        name=name,
    )(a, w, *extras)


def _softmax_parts(parts, extra_col=None):
    m = parts[0].max(axis=1, keepdims=True)
    for p in parts[1:]:
        m = jnp.maximum(m, p.max(axis=1, keepdims=True))
    if extra_col is not None:
        m = jnp.maximum(m, extra_col)
    es = [jnp.exp(p - m) for p in parts]
    den = es[0].sum(axis=1, keepdims=True)
    for e in es[1:]:
        den = den + e.sum(axis=1, keepdims=True)
    if extra_col is not None:
        den = den + jnp.exp(extra_col - m)
    inv = 1.0 / den
    return [e * inv for e in es]


WIN_UNROLL = 4


def _win_body(sink_ref, q_ref, k_ref, v_ref, o_ref, *, seq):
    kv = pl.program_id(1)
    nb = seq // A_BLOCK
    scale = HEAD_DIM ** -0.5 * LOG2E
    rows = A_GROUP * A_BLOCK
    qi = lax.broadcasted_iota(jnp.int32, (rows, A_BLOCK), 0) % A_BLOCK
    ci = lax.broadcasted_iota(jnp.int32, (rows, A_BLOCK), 1)
    sink_b = jnp.concatenate(
        [jnp.full((A_BLOCK, HEAD_DIM), sink_ref[kv * A_GROUP + g] * LOG2E, F32)
         for g in range(A_GROUP)], axis=0)

    def scores(n):
        r0 = pl.multiple_of(n * A_BLOCK, A_BLOCK)
        rp = pl.multiple_of(jnp.maximum(n - 1, 0) * A_BLOCK, A_BLOCK)
        rn = pl.multiple_of(jnp.minimum(n + 1, nb - 1) * A_BLOCK, A_BLOCK)
        off_p = jnp.where(n > 0, 0, 2 * A_BLOCK)
        off_n = jnp.where(n < nb - 1, 0, 2 * A_BLOCK)
        q = jnp.concatenate(
            [q_ref[pl.ds(r0, A_BLOCK), g * HEAD_DIM:(g + 1) * HEAD_DIM] for g in range(A_GROUP)],
            axis=0)
        sp = lax.dot_general(q, k_ref[pl.ds(rp, A_BLOCK), :], _NT, preferred_element_type=F32) * scale
        sc = lax.dot_general(q, k_ref[pl.ds(r0, A_BLOCK), :], _NT, preferred_element_type=F32) * scale
        sn = lax.dot_general(q, k_ref[pl.ds(rn, A_BLOCK), :], _NT, preferred_element_type=F32) * scale
        sp = jnp.where(ci >= qi + off_p, sp, NEG)
        sn = jnp.where(ci <= qi - off_n, sn, NEG)
        return (rp, r0, rn), (sp, sc, sn)

    def exps(parts):
        m = jnp.maximum(jnp.maximum(parts[0], parts[1]), parts[2]).max(axis=1, keepdims=True)
        m = jnp.maximum(jnp.broadcast_to(m, sink_b.shape), sink_b)
        es = [jnp.exp2(p - m) for p in parts]
        den = (es[0] + es[1] + es[2]).sum(axis=1, keepdims=True)
        den = jnp.broadcast_to(den, sink_b.shape) + jnp.exp2(sink_b - m)
        return [e.astype(BF16) for e in es], 1.0 / den

    def body(it, carry):
        blocks = [scores(it * WIN_UNROLL + u) for u in range(WIN_UNROLL)]
        probs = [exps(parts) for _, parts in blocks]
        for (rows_kv, _), (es, inv) in zip(blocks, probs):
            o = jnp.dot(es[0], v_ref[pl.ds(rows_kv[0], A_BLOCK), :], preferred_element_type=F32)
            o = o + jnp.dot(es[1], v_ref[pl.ds(rows_kv[1], A_BLOCK), :], preferred_element_type=F32)
            o = o + jnp.dot(es[2], v_ref[pl.ds(rows_kv[2], A_BLOCK), :], preferred_element_type=F32)
            o = o * inv
            for g in range(A_GROUP):
                o_ref[pl.ds(rows_kv[1], A_BLOCK), g * HEAD_DIM:(g + 1) * HEAD_DIM] = (
                    o[g * A_BLOCK:(g + 1) * A_BLOCK].astype(o_ref.dtype))
        return carry

    lax.fori_loop(0, nb // WIN_UNROLL, body, 0)


def window_attention(qk, vqkv, sink, batch, seq):
    gw = A_GROUP * HEAD_DIM
    k_blk0 = QA_W // HEAD_DIM
    return pl.pallas_call(
        functools.partial(_win_body, seq=seq),
        grid=(batch, A_KV_HEADS),
        in_specs=[pl.BlockSpec(memory_space=pltpu.SMEM),
                  pl.BlockSpec((seq, gw), lambda b, h: (b, h)),
                  pl.BlockSpec((seq, HEAD_DIM), lambda b, h: (b, k_blk0 + h)),
                  pl.BlockSpec((seq, HEAD_DIM), lambda b, h: (b, h))],
        out_specs=pl.BlockSpec((seq, gw), lambda b, h: (b, h)),
        out_shape=jax.ShapeDtypeStruct((batch * seq, QA_W), BF16),
        compiler_params=_params(("arbitrary", "arbitrary"), 32),
        name="window_attention",
    )(sink, qk, qk, vqkv)


NBR_HG = 4
NBR_ROWS = 4


def _nbr_body(q_ref, k_ref, v_ref, bias_ref, o_ref, *, seq):
    rows = seq // GRID_W
    kh = min(NA_KH_MAX, rows)
    strip = kh * GRID_W
    scale = HEAD_DIM ** -0.5 * LOG2E

    def body(it, carry):
        units = []
        for rr in range(NBR_ROWS):
            r = it * NBR_ROWS + rr
            rs = jnp.clip(r - kh // 2, 0, rows - kh)
            q0 = pl.multiple_of(r * GRID_W, GRID_W)
            k0 = pl.multiple_of(rs * GRID_W, GRID_W)
            for h in range(NBR_HG):
                units.append((q0, k0, r - rs, h, slice(h * HEAD_DIM, (h + 1) * HEAD_DIM)))
        ss = [lax.dot_general(q_ref[pl.ds(q0, GRID_W), cols], k_ref[pl.ds(k0, strip), cols], _NT,
                              preferred_element_type=F32) * scale + bias_ref[h, var]
              for q0, k0, var, h, cols in units]
        ps = []
        for s in ss:
            e = jnp.exp2(s - s.max(axis=1, keepdims=True))
            ps.append((e.astype(BF16), 1.0 / e.sum(axis=1, keepdims=True)))
        for (q0, k0, var, h, cols), (e, inv) in zip(units, ps):
            o = jnp.dot(e, v_ref[pl.ds(k0, strip), cols], preferred_element_type=F32) * inv
            o_ref[pl.ds(q0, GRID_W), cols] = o.astype(o_ref.dtype)
        return carry

    lax.fori_loop(0, rows // NBR_ROWS, body, 0)


def neighbourhood_attention(vqkv, bias_tbl, batch, seq):
    gw = NBR_HG * HEAD_DIM
    q0, k0, v0 = KVA_W // gw, (KVA_W + QB_W) // gw, (KVA_W + 2 * QB_W) // gw
    kh = bias_tbl.shape[1]
    return pl.pallas_call(
        functools.partial(_nbr_body, seq=seq),
        grid=(B_HEADS // NBR_HG, batch),
        in_specs=[pl.BlockSpec((seq, gw), lambda g, b: (b, q0 + g)),
                  pl.BlockSpec((seq, gw), lambda g, b: (b, k0 + g)),
                  pl.BlockSpec((seq, gw), lambda g, b: (b, v0 + g)),
                  pl.BlockSpec((NBR_HG, kh, GRID_W, kh * GRID_W), lambda g, b: (g, 0, 0, 0))],
        out_specs=pl.BlockSpec((seq, gw), lambda g, b: (b, g)),
        out_shape=jax.ShapeDtypeStruct((batch * seq, QB_W), BF16),
        compiler_params=_params(("arbitrary", "arbitrary"), 40),
        name="neighbourhood_attention",
    )(vqkv, vqkv, vqkv, bias_tbl)


def _bias_table_body(rpb_ref, o_ref, *, kh):
    h = pl.program_id(0)
    n_dr, n_dc = 2 * NA_KH_MAX - 1, 2 * NA_KW - 1
    c = lax.broadcasted_iota(jnp.int32, (GRID_W, LANES), 0)
    lane = lax.broadcasted_iota(jnp.int32, (GRID_W, LANES), 1)
    kc = lane % GRID_W
    diff = jnp.clip(kc - c + NA_KW - 1, 0, n_dc - 1)
    cs = jnp.clip(c - NA_KW // 2, 0, GRID_W - NA_KW)
    col_ok = (kc >= cs) & (kc < cs + NA_KW)
    slabs = []
    for dr in range(n_dr):
        acc = jnp.zeros((GRID_W, LANES), F32)
        for d in range(n_dc):
            acc = jnp.where(diff == d, rpb_ref[(h * n_dr + dr) * n_dc + d], acc)
        slabs.append(jnp.where(col_ok, acc * LOG2E, NEG))
    left = lane < GRID_W
    for var in range(kh):
        for jp in range(kh * GRID_W // LANES):
            dr0 = 2 * jp - var + NA_KH_MAX - 1
            o_ref[0, var, :, jp * LANES:(jp + 1) * LANES] = jnp.where(left, slabs[dr0], slabs[dr0 + 1])


def _nbr_bias_table(rpb, seq):
    rows = seq // GRID_W
    kh = min(NA_KH_MAX, rows)
    heads = rpb.shape[0]
    assert kh == NA_KH_MAX and 2 * GRID_W == LANES
    return pl.pallas_call(
        functools.partial(_bias_table_body, kh=kh),
        grid=(heads,),
        in_specs=[pl.BlockSpec(memory_space=pltpu.SMEM)],
        out_specs=pl.BlockSpec((1, kh, GRID_W, kh * GRID_W), lambda h: (h, 0, 0, 0)),
        out_shape=jax.ShapeDtypeStruct((heads, kh, GRID_W, kh * GRID_W), F32),
        compiler_params=_params(("arbitrary",), 16),
        name="nbr_bias_table",
    )(rpb.astype(F32).reshape(-1))


def _merge_body(oa_ref, ob_ref, wa_ref, wb_ref, g0_ref, g1_ref, o_ref):
    wa, wb = wa_ref[...].astype(BF16), wb_ref[...].astype(BF16)
    for rows in _row_parts(oa_ref.shape[0]):
        ya = jnp.dot(oa_ref[rows, :], wa, preferred_element_type=F32)
        yb = jnp.dot(ob_ref[rows, :], wb, preferred_element_type=F32)
        o_ref[rows, :] = (g0_ref[rows, :].astype(F32) * ya
                          + g1_ref[rows, :].astype(F32) * yb).astype(o_ref.dtype)


def branch_merge(oa, ob, wa, wb, gates, bm=2048, bn=256):
    m, k = oa.shape
    n = wa.shape[1]
    g1_off = n // bn
    return pl.pallas_call(
        _merge_body,
        grid=(m // bm, n // bn),
        in_specs=[pl.BlockSpec((bm, k), lambda i, j: (i, 0)),
                  pl.BlockSpec((bm, k), lambda i, j: (i, 0)),
                  pl.BlockSpec((k, bn), lambda i, j: (0, j)),
                  pl.BlockSpec((k, bn), lambda i, j: (0, j)),
                  pl.BlockSpec((bm, bn), lambda i, j: (i, j)),
                  pl.BlockSpec((bm, bn), lambda i, j: (i, j + g1_off))],
        out_specs=pl.BlockSpec((bm, bn), lambda i, j: (i, j)),
        out_shape=jax.ShapeDtypeStruct((m, n), BF16),
        compiler_params=_params(("arbitrary", "arbitrary"), 56),
        name="branch_merge",
    )(oa, ob, wa, wb, gates, gates)


def _cast_body(x_ref, o_ref):
    o_ref[...] = x_ref[...].astype(o_ref.dtype)


def cast_bf16(w):
    r, c = w.shape
    return pl.pallas_call(
        _cast_body,
        grid=(1,),
        in_specs=[pl.BlockSpec((r, c), lambda i: (0, 0))],
        out_specs=pl.BlockSpec((r, c), lambda i: (0, 0)),
        out_shape=jax.ShapeDtypeStruct((r, c), BF16),
        compiler_params=_params(("arbitrary",), 40),
        name="cast_bf16",
    )(w)


def _router_probs(hn, wr_ref):
    hi = hn.astype(BF16)
    lo = (hn - hi.astype(F32)).astype(BF16)
    l_hi = jnp.dot(hi, wr_ref[...], preferred_element_type=F32)
    l_lo = jnp.dot(lo, wr_ref[...], preferred_element_type=F32)
    logits = l_hi + pltpu.roll(l_hi, LANES - N_EXPERTS, 1) + l_lo
    lane = lax.broadcasted_iota(jnp.int32, logits.shape, 1)
    logits = jnp.where(lane < N_EXPERTS, logits, NEG)
    (aff,) = _softmax_parts([logits])
    return aff


def _xblock_body(x_ref, gc_ref, wq_ref, k_ref, v_ref, wo_ref, gf_ref, wr_ref,
                 x2_ref, h3_ref, aff_ref):
    scale = HEAD_DIM ** -0.5
    half = x_ref.shape[1] // 2
    parts = [slice(p * XB_ROWS, (p + 1) * XB_ROWS) for p in range(x_ref.shape[0] // XB_ROWS)]
    h2s = [_rmsnorm_rows(x_ref[r, :], gc_ref[...]).astype(BF16) for r in parts]
    qs = [jnp.dot(h2, wq_ref[...], preferred_element_type=F32).astype(BF16) for h2 in h2s]
    os = []
    for q in qs:
        heads = []
        for h in range(X_HEADS):
            cols = slice(h * HEAD_DIM, (h + 1) * HEAD_DIM)
            s = lax.dot_general(q[:, cols], k_ref[:, cols], _NT, preferred_element_type=F32) * scale
            (p,) = _softmax_parts([s])
            heads.append(jnp.dot(p.astype(BF16), v_ref[:, cols],
                                 preferred_element_type=F32).astype(BF16))
        os.append(jnp.concatenate(heads, axis=1))
    for r, o in zip(parts, os):
        x2_ref[r, :] = x_ref[r, :] + jnp.dot(o, wo_ref[...], preferred_element_type=F32)
    for r in parts:
        hn = _rmsnorm_rows(x2_ref[r, :], gf_ref[...])
        packed = pltpu.pack_elementwise([hn[:, :half], hn[:, half:]], packed_dtype=BF16)
        h3_ref[r, :] = pltpu.bitcast(packed, jnp.uint32)
        aff = _router_probs(hn, wr_ref)
        aff_ref[0, :, r] = aff.T[:N_EXPERTS, :]


XB_ROWS = 256


def cross_attention_block(x1, g_cross, wq, kx, vx, wo, g_ffn, w_router, batch, seq, mem_len, bm=512):
    m, d = x1.shape
    nt = seq // bm
    w_hi = w_router.astype(BF16)
    w_lo = (w_router - w_hi.astype(F32)).astype(BF16)
    wr = jnp.concatenate([w_hi, w_lo, jnp.zeros((d, LANES - 2 * N_EXPERTS), BF16)], axis=1)
    const = lambda i: (0, 0)
    once = pl.Buffered(1)
    return pl.pallas_call(
        _xblock_body,
        grid=(m // bm,),
        in_specs=[pl.BlockSpec((bm, d), lambda i: (i, 0)),
                  pl.BlockSpec((1, d), const),
                  pl.BlockSpec((d, X_W), const, pipeline_mode=once),
                  pl.BlockSpec((mem_len, X_W), lambda i: (i // nt, 0)),
                  pl.BlockSpec((mem_len, X_W), lambda i: (i // nt, 0)),
                  pl.BlockSpec((X_W, d), const, pipeline_mode=once),
                  pl.BlockSpec((1, d), const),
                  pl.BlockSpec((d, LANES), const, pipeline_mode=once)],
        out_specs=[pl.BlockSpec((bm, d), lambda i: (i, 0)),
                   pl.BlockSpec((bm, d // 2), lambda i: (i, 0)),
                   pl.BlockSpec((1, N_EXPERTS, bm), lambda i: (i // nt, 0, i % nt))],
        out_shape=[jax.ShapeDtypeStruct((m, d), F32),
                   jax.ShapeDtypeStruct((m, d // 2), jnp.uint32),
                   jax.ShapeDtypeStruct((batch, N_EXPERTS, seq), F32)],
        compiler_params=_params(("arbitrary",), 58),
        name="cross_attention_block",
    )(x1, g_cross.reshape(1, d), cast_bf16(wq), kx, vx, cast_bf16(wo), g_ffn.reshape(1, d), wr)


CUM_CHUNK = 256


def _excl_cumsum_lanes(x01, tri):
    n = x01.shape[1]
    carry = jnp.zeros((x01.shape[0], 1), F32)
    out = []
    for c in range(n // CUM_CHUNK):
        xc = x01[:, c * CUM_CHUNK:(c + 1) * CUM_CHUNK]
        out.append(jnp.dot(xc.astype(BF16), tri, preferred_element_type=F32) + carry)
        carry = carry + xc.sum(axis=1, keepdims=True)
    return jnp.concatenate(out, axis=1)


COMBINE_TS = 256
COMBINE_W = 64


def _topk_body(aff_ref, slot_ref, first_ref, *, cap):
    a = aff_ref[0]
    n_exp, s = a.shape
    capf = jnp.float32(cap)
    lane_s = lax.broadcasted_iota(jnp.int32, (n_exp, s), 1)

    def count(mask):
        return jnp.where(mask, 1.0, 0.0).sum(axis=1, keepdims=True)

    bits = pltpu.bitcast(a, jnp.int32)

    def search(i, t):
        cand = t | jnp.left_shift(jnp.int32(1), 30 - i)
        return jnp.where(count(bits >= cand) >= capf, cand, t)

    t = lax.fori_loop(0, 31, search, jnp.zeros((n_exp, 1), jnp.int32))
    at = jnp.where(bits == t, lane_s, s).astype(F32).min(axis=1, keepdims=True).astype(jnp.int32)
    pivot = jnp.where(lane_s == at, a, 0.0).sum(axis=1, keepdims=True)

    def stats(p):
        return p, count(a > p), count(a >= p)

    def wrong(state):
        _, n_gt, n_ge = state
        return jnp.where(jnp.logical_or(n_gt >= capf, n_ge < capf), 1.0, 0.0).sum() > 0.0

    def step(state):
        p, n_gt, n_ge = state
        up = jnp.where(a > p, a, jnp.inf).min(axis=1, keepdims=True)
        down = jnp.where(a < p, a, -jnp.inf).max(axis=1, keepdims=True)
        return stats(jnp.where(n_gt >= capf, up, jnp.where(n_ge < capf, down, p)))

    pivot, n_gt, _ = lax.while_loop(wrong, step, stats(pivot))
    ri = lax.broadcasted_iota(jnp.int32, (CUM_CHUNK, CUM_CHUNK), 0)
    cj = lax.broadcasted_iota(jnp.int32, (CUM_CHUNK, CUM_CHUNK), 1)
    tri = jnp.where(ri < cj, 1.0, 0.0).astype(BF16)
    eq = jnp.where(a == pivot, 1.0, 0.0)
    sel = jnp.where(a > pivot, 1.0, 0.0) + jnp.where(_excl_cumsum_lanes(eq, tri) < capf - n_gt, eq, 0.0)
    pos = _excl_cumsum_lanes(sel, tri)
    slot_ref[0] = jnp.where(sel > 0.5, pos, -1.0).astype(jnp.int32)
    lane_k = lax.broadcasted_iota(jnp.int32, (n_exp, LANES), 1)
    first = jnp.zeros((n_exp, LANES), F32)
    for k in range(s // COMBINE_TS):
        at_k = jnp.where(lane_s == k * COMBINE_TS, pos, 0.0).sum(axis=1, keepdims=True)
        first = jnp.where(lane_k == k, at_k, first)
    first_ref[0] = first.astype(jnp.int32)


def expert_topk(aff_t, cap):
    b, e, s = aff_t.shape
    return pl.pallas_call(
        functools.partial(_topk_body, cap=cap),
        grid=(b,),
        in_specs=[pl.BlockSpec((1, e, s), lambda i: (i, 0, 0))],
        out_specs=[pl.BlockSpec((1, e, s), lambda i: (i, 0, 0)),
                   pl.BlockSpec((1, e, LANES), lambda i: (i, 0, 0))],
        out_shape=[jax.ShapeDtypeStruct((b, e, s), jnp.int32),
                   jax.ShapeDtypeStruct((b, e, LANES), jnp.int32)],
        compiler_params=_params(("arbitrary",), 32),
        name="expert_topk",
    )(aff_t)


def _slot_index_body(slot_ref, aff_ref, idx_ref, val_ref, *, cap):
    b = pl.program_id(0)
    n_exp, s = slot_ref.shape[1], slot_ref.shape[2]
    ci = lax.broadcasted_iota(jnp.int32, (cap, s), 0)
    tok = lax.broadcasted_iota(jnp.int32, (cap, s), 1).astype(F32)
    base = (b * s).astype(F32)
    for e in range(n_exp):
        hit = slot_ref[0, e:e + 1, :] == ci
        idx = jnp.where(hit, tok, 0.0).sum(axis=1, keepdims=True) + base
        idx_ref[e] = idx.astype(jnp.int32)
        val_ref[e] = jnp.where(hit, aff_ref[0, e:e + 1, :], 0.0).sum(axis=1, keepdims=True)


def slot_index(slot, aff_t, cap):
    b, e, s = slot.shape
    return pl.pallas_call(
        functools.partial(_slot_index_body, cap=cap),
        grid=(b,),
        in_specs=[pl.BlockSpec((1, e, s), lambda bi: (bi, 0, 0)),
                  pl.BlockSpec((1, e, s), lambda bi: (bi, 0, 0))],
        out_specs=[pl.BlockSpec((e, cap, 1), lambda bi: (0, bi, 0)),
                   pl.BlockSpec((e, cap, 1), lambda bi: (0, bi, 0))],
        out_shape=[jax.ShapeDtypeStruct((e, b * cap, 1), jnp.int32),
                   jax.ShapeDtypeStruct((e, b * cap, 1), F32)],
        compiler_params=_params(("arbitrary",), 32),
        name="slot_index",
    )(slot, aff_t)


UNPACK_ROWS = 128


def _row_copy(idx_ref, hp_ref, gbuf, sem, expert, rows, row):
    tok = idx_ref[expert * rows + row]
    return pltpu.make_async_copy(hp_ref.at[pl.ds(tok, 1)], gbuf.at[pl.ds(row, 1)], sem)


def _expert_up_body(idx_ref, hp_ref, wg_ref, wu_ref, o_ref, gbuf, xbf, sem, *, rows, per_step):
    e, f = pl.program_id(0), pl.program_id(1)
    n_e, n_f = pl.num_programs(0), pl.num_programs(1)
    half = gbuf.shape[1]

    def wait_all_rows():
        pltpu.make_async_copy(hp_ref.at[pl.ds(0, rows)], gbuf, sem).wait()

    @pl.when(jnp.logical_and(e == 0, f == 0))
    def _():
        def body(r, carry):
            _row_copy(idx_ref, hp_ref, gbuf, sem, 0, rows, r).start()
            return carry
        lax.fori_loop(0, rows, body, 0)

    @pl.when(f == 0)
    def _():
        wait_all_rows()

        def unpack(k, carry):
            r = pl.multiple_of(k * UNPACK_ROWS, UNPACK_ROWS)
            w = gbuf[pl.ds(r, UNPACK_ROWS), :]
            lo = pltpu.unpack_elementwise(w, index=0, packed_dtype=BF16, unpacked_dtype=F32)
            hi = pltpu.unpack_elementwise(w, index=1, packed_dtype=BF16, unpacked_dtype=F32)
            xbf[pl.ds(r, UNPACK_ROWS), :half] = lo.astype(BF16)
            xbf[pl.ds(r, UNPACK_ROWS), half:] = hi.astype(BF16)
            return carry
        lax.fori_loop(0, rows // UNPACK_ROWS, unpack, 0)

    nxt = jnp.minimum(e + 1, n_e - 1)
    for r in range(per_step):
        _row_copy(idx_ref, hp_ref, gbuf, sem, nxt, rows, f * per_step + r).start()

    wg, wu = wg_ref[0].astype(BF16), wu_ref[0].astype(BF16)
    for part in _row_parts(rows):
        a = jnp.dot(xbf[part, :], wg, preferred_element_type=F32)
        u = jnp.dot(xbf[part, :], wu, preferred_element_type=F32)
        o_ref[0, part, :] = (a * _sigmoid(a) * u).astype(o_ref.dtype)

    @pl.when(jnp.logical_and(e == n_e - 1, f == n_f - 1))
    def _():
        wait_all_rows()


def expert_up(idx, hp, w_gate, w_up, rows, tf=256):
    n_e, d, f = w_gate.shape
    assert hp.shape[1] * 2 == d and rows % (f // tf) == 0
    grid_spec = pltpu.PrefetchScalarGridSpec(
        num_scalar_prefetch=1,
        grid=(n_e, f // tf),
        in_specs=[pl.BlockSpec(memory_space=pl.ANY),
                  pl.BlockSpec((1, d, tf), lambda ei, fi, idx_ref: (ei, 0, fi)),
                  pl.BlockSpec((1, d, tf), lambda ei, fi, idx_ref: (ei, 0, fi))],
        out_specs=pl.BlockSpec((1, rows, tf), lambda ei, fi, idx_ref: (ei, 0, fi)),
        scratch_shapes=[pltpu.VMEM((rows, d // 2), jnp.uint32),
                        pltpu.VMEM((rows, d), BF16),
                        pltpu.SemaphoreType.DMA(())],
    )
    return pl.pallas_call(
        functools.partial(_expert_up_body, rows=rows, per_step=rows // (f // tf)),
        grid_spec=grid_spec,
        out_shape=jax.ShapeDtypeStruct((n_e, rows, f), BF16),
        compiler_params=_params(("arbitrary", "arbitrary"), 56),
        name="expert_up",
    )(idx, hp, w_gate, w_up)


def _expert_down_body(h_ref, wd_ref, val_ref, o_ref):
    wd = wd_ref[0].astype(BF16)
    for part in _row_parts(h_ref.shape[1]):
        y = jnp.dot(h_ref[0, part, :], wd, preferred_element_type=F32)
        o_ref[0, part, :] = (y * val_ref[0, part, :]).astype(o_ref.dtype)


def expert_down(hmid, w_down, valc, tn=1024):
    e, rows, f = hmid.shape
    d = w_down.shape[-1]
    return pl.pallas_call(
        _expert_down_body,
        grid=(e, d // tn),
        in_specs=[pl.BlockSpec((1, rows, f), lambda ei, ni: (ei, 0, 0)),
                  pl.BlockSpec((1, f, tn), lambda ei, ni: (ei, 0, ni)),
                  pl.BlockSpec((1, rows, 1), lambda ei, ni: (ei, 0, 0))],
        out_specs=pl.BlockSpec((1, rows, tn), lambda ei, ni: (ei, 0, ni)),
        out_shape=jax.ShapeDtypeStruct((e, rows, d), BF16),
        compiler_params=_params(("arbitrary", "arbitrary"), 48),
        name="expert_down",
    )(hmid, w_down, valc)


COMBINE_TN = 512
ROW_ALIGN_BF16 = 16


def _combine_windows(first, cap, n_rows):
    b, e, nt = first.shape
    row0 = jnp.arange(b, dtype=jnp.int32)[:, None, None] * cap
    lo = first + row0
    hi = jnp.concatenate([first[:, :, 1:], jnp.full((b, e, 1), cap, jnp.int32)], axis=2) + row0
    start = jnp.minimum(lo // ROW_ALIGN_BF16 * ROW_ALIGN_BF16, n_rows - COMBINE_W)
    fast = jnp.all(hi - start <= COMBINE_W, axis=1)
    return start.transpose(0, 2, 1).reshape(-1), fast.astype(jnp.int32).reshape(-1)


def _combine_body(win_ref, fast_ref, slot_ref, y_hbm, x_ref, g_ref, o_ref,
                  ywin, oht, ybuf, ohs, wsem, ssem, *, cap, n_tiles):
    i = pl.program_id(0)
    n_exp = slot_ref.shape[1]
    ts, d = o_ref.shape
    w = COMBINE_W
    row0 = (i // n_tiles) * cap
    buf = i % 2

    def window_copy(step, e, b):
        start = pl.multiple_of(win_ref[step * n_exp + e], ROW_ALIGN_BF16)
        return pltpu.make_async_copy(y_hbm.at[e, pl.ds(start, w)], ywin.at[b, pl.ds(e * w, w)],
                                     wsem.at[b])

    @pl.when(i == 0)
    def _():
        for e in range(n_exp):
            window_copy(0, e, 0).start()

    @pl.when(i + 1 < pl.num_programs(0))
    def _():
        for e in range(n_exp):
            window_copy(i + 1, e, 1 - buf).start()

    pltpu.make_async_copy(y_hbm.at[0, pl.ds(0, n_exp * w)], ywin.at[buf], wsem.at[buf]).wait()

    @pl.when(fast_ref[i] == 1)
    def _():
        row = lax.broadcasted_iota(jnp.int32, (2 * w, ts), 0)
        upper = row >= w
        j = row % w
        for p in range(n_exp // 2):
            rel_a = slot_ref[0, 2 * p:2 * p + 1, :] + (row0 - win_ref[i * n_exp + 2 * p])
            rel_b = slot_ref[0, 2 * p + 1:2 * p + 2, :] + (row0 - win_ref[i * n_exp + 2 * p + 1])
            hit = jnp.where(upper, rel_b, rel_a) == j
            oht[:, p * 2 * w:(p + 1) * 2 * w] = jnp.where(hit, 1.0, 0.0).T.astype(BF16)
        for c in range(d // COMBINE_TN):
            cols = slice(c * COMBINE_TN, (c + 1) * COMBINE_TN)
            o_ref[:, cols] = x_ref[:, cols] + jnp.dot(oht[...], ywin[buf, :, cols],
                                                      preferred_element_type=F32)

    @pl.when(fast_ref[i] == 0)
    def _():
        o_ref[...] = x_ref[...]
        ci = lax.broadcasted_iota(jnp.int32, (cap, ts), 0)

        def body(e, carry):
            cp = pltpu.make_async_copy(y_hbm.at[e, pl.ds(pl.multiple_of(row0, cap), cap)], ybuf, ssem)
            cp.start()
            cp.wait()
            srow = slot_ref[0, pl.ds(e, 1), :]
            ohs[...] = jnp.where(srow == ci, 1.0, 0.0).T.astype(BF16)
            for c in range(d // COMBINE_TN):
                cols = slice(c * COMBINE_TN, (c + 1) * COMBINE_TN)
                o_ref[:, cols] += jnp.dot(ohs[...], ybuf[:, cols], preferred_element_type=F32)
            return carry
        lax.fori_loop(0, n_exp, body, 0)

    o_ref[...] = _rmsnorm_rows(o_ref[...], g_ref[...])


def expert_combine(slot, first, y, x2d, g, cap):
    b, e, s = slot.shape
    d = x2d.shape[-1]
    ts, w = COMBINE_TS, COMBINE_W
    nt = s // ts
    assert e % 2 == 0 and y.shape[1] >= e * w and cap % ROW_ALIGN_BF16 == 0
    win, fast = _combine_windows(first[:, :, :nt], cap, y.shape[1])
    grid_spec = pltpu.PrefetchScalarGridSpec(
        num_scalar_prefetch=2,
        grid=(b * nt,),
        in_specs=[pl.BlockSpec((1, e, ts), lambda i, win_ref, fast_ref: (i // nt, 0, i % nt)),
                  pl.BlockSpec(memory_space=pl.ANY),
                  pl.BlockSpec((ts, d), lambda i, win_ref, fast_ref: (i, 0)),
                  pl.BlockSpec((1, d), lambda i, win_ref, fast_ref: (0, 0))],
        out_specs=pl.BlockSpec((ts, d), lambda i, win_ref, fast_ref: (i, 0)),
        scratch_shapes=[pltpu.VMEM((2, e * w, d), BF16),
                        pltpu.VMEM((ts, e * w), BF16),
                        pltpu.VMEM((cap, d), BF16),
                        pltpu.VMEM((ts, cap), BF16),
                        pltpu.SemaphoreType.DMA((2,)),
                        pltpu.SemaphoreType.DMA(())],
    )
    return pl.pallas_call(
        functools.partial(_combine_body, cap=cap, n_tiles=nt),
        grid_spec=grid_spec,
        out_shape=jax.ShapeDtypeStruct(x2d.shape, F32),
        compiler_params=_params(("arbitrary",), 48),
        name="expert_combine",
    )(win, fast, slot, y, x2d, g.reshape(1, d))


def _rotary_tables(seq):
    half = ROT_DIM // 2
    inv = ROPE_THETA ** (-jnp.arange(half, dtype=F32) * 2.0 / ROT_DIM)
    ang = jnp.arange(seq).astype(F32)[:, None] * inv[None, :]
    cos, sin = jnp.cos(ang), jnp.sin(ang)
    ones = jnp.ones((seq, HEAD_DIM - ROT_DIM), F32)
    zeros = jnp.zeros((seq, HEAD_DIM - ROT_DIM), F32)
    zh = jnp.zeros((seq, half), F32)
    c = jnp.concatenate([cos, cos, ones], axis=1)
    s1 = jnp.concatenate([-sin, zh, zeros], axis=1)
    s2 = jnp.concatenate([zh, sin, zeros], axis=1)
    return c, s1, s2


def kernel(x, mem, norm_mix, w_in, b_gate, sink, rpb, w_branch_a, w_branch_b, w_out,
           norm_cross, norm_mem, wq_x, wk_x, wv_x, wo_x, norm_ffn, w_router,
           w_gate, w_up, w_down, norm_final):
    batch, seq, d = x.shape
    mem_len = mem.shape[1]
    m = batch * seq
    assert norm_mix.shape[0] == 1, "final RMSNorm is fused into the single layer's last kernel"
    cap = EC_CAPACITY * seq // N_EXPERTS
    bm, bn = 2048, 256
    sb = seq // bm
    x0 = x.reshape(m, d)

    h = rmsnorm(x0, norm_mix[0], BF16)
    rot = _rotary_tables(seq)
    rot_specs = [pl.BlockSpec((bm, HEAD_DIM), lambda i, j: (i % sb, 0))] * 3
    qk = matmul_rows(h, w_in[0], col_off=0, n_cols=QA_W + KVA_W, bm=bm, bn=bn, out_dtype=BF16,
                     epilogue=_ep_rotary, extras=rot, extra_specs=rot_specs, name="in_proj_rotary")
    vqkv = matmul_rows(h, w_in[0], col_off=QA_W + KVA_W, n_cols=KVA_W + 3 * QB_W, bm=bm, bn=bn,
                       out_dtype=BF16, name="in_proj_plain")
    g_off = QA_W + 2 * KVA_W + 3 * QB_W
    gates = matmul_rows(h, w_in[0], col_off=g_off, n_cols=2 * d, bm=bm, bn=bn, out_dtype=BF16,
                        epilogue=_ep_sigmoid, extras=(b_gate[0].reshape(1, 2 * d),),
                        extra_specs=[pl.BlockSpec((1, bn), lambda i, j: (0, j))], name="in_proj_gates")
    oa = window_attention(qk, vqkv, sink[0], batch, seq)
    ob = neighbourhood_attention(vqkv, _nbr_bias_table(rpb[0], seq), batch, seq)
    merged = branch_merge(oa, ob, w_branch_a[0], w_branch_b[0], gates, bm=bm, bn=bn)
    res_spec = [pl.BlockSpec((bm, bn), lambda i, j: (i, j))]
    x1 = matmul_rows(merged, w_out[0], col_off=0, n_cols=d, bm=bm, bn=bn, out_dtype=F32,
                     epilogue=_ep_residual, extras=(x0,), extra_specs=res_spec, name="out_proj")

    mem_rows = batch * mem_len
    mn = rmsnorm(mem.reshape(mem_rows, d), norm_mem[0], BF16)
    kx = matmul_rows(mn, wk_x[0], col_off=0, n_cols=X_W, bm=mem_rows, bn=bn, out_dtype=BF16, name="xattn_k")
    vx = matmul_rows(mn, wv_x[0], col_off=0, n_cols=X_W, bm=mem_rows, bn=bn, out_dtype=BF16, name="xattn_v")
    x2, h3p, aff_t = cross_attention_block(
        x1, norm_cross[0], wq_x[0], kx, vx, wo_x[0], norm_ffn[0], w_router[0], batch, seq, mem_len)

    slot, first = expert_topk(aff_t, cap)
    idx, valc = slot_index(slot, aff_t, cap)
    hmid = expert_up(idx.reshape(-1), h3p, w_gate[0], w_up[0], batch * cap)
    y = expert_down(hmid, w_down[0], valc)
    out = expert_combine(slot, first, y, x2, norm_final, cap)
    return out.reshape(batch, seq, d)
```

```python
import functools

import jax
import jax.numpy as jnp
from jax import lax
from jax.experimental import pallas as pl
from jax.experimental.pallas import tpu as pltpu

F32 = jnp.float32
BF16 = jnp.bfloat16

HEAD_DIM = 128
A_HEADS = 16
A_KV_HEADS = 4
A_GROUP = A_HEADS // A_KV_HEADS
WINDOW = 128
A_BLOCK = 128
ROT_DIM = HEAD_DIM // 4
ROPE_THETA = 500000.0
B_HEADS = 16
GRID_W = 64
NA_KH_MAX = 8
NA_KW = 16
X_HEADS = 4
N_EXPERTS = 16
EC_CAPACITY = 2
EPS = 1e-6
NEG = -1e30
LOG2E = 1.4426950408889634
LANES = 128
MIB = 1024 * 1024

QA_W = A_HEADS * HEAD_DIM
KVA_W = A_KV_HEADS * HEAD_DIM
QB_W = B_HEADS * HEAD_DIM
X_W = X_HEADS * HEAD_DIM

_NT = (((1,), (1,)), ((), ()))
_TN = (((0,), (0,)), ((), ()))


def _params(semantics, vmem_mib):
    return pltpu.CompilerParams(dimension_semantics=semantics,
                                vmem_limit_bytes=vmem_mib * MIB)


def _rmsnorm_rows(x, g):
    ms = jnp.mean(x * x, axis=-1, keepdims=True)
    return x * lax.rsqrt(ms + EPS) * g


def _rmsnorm_body(x_ref, g_ref, o_ref):
    o_ref[...] = _rmsnorm_rows(x_ref[...], g_ref[...]).astype(o_ref.dtype)


def rmsnorm(x2d, g, out_dtype, bm=512):
    m, d = x2d.shape
    return pl.pallas_call(
        _rmsnorm_body,
        grid=(m // bm,),
        in_specs=[pl.BlockSpec((bm, d), lambda i: (i, 0)),
                  pl.BlockSpec((1, d), lambda i: (0, 0))],
        out_specs=pl.BlockSpec((bm, d), lambda i: (i, 0)),
        out_shape=jax.ShapeDtypeStruct((m, d), out_dtype),
        compiler_params=_params(("arbitrary",), 40),
        name="rmsnorm",
    )(x2d, g.reshape(1, d))


def _sigmoid(x):
    return 0.5 * jnp.tanh(0.5 * x) + 0.5


ROW_SPLIT = 4


def _row_parts(n_rows):
    step = n_rows // ROW_SPLIT
    return [slice(p * step, (p + 1) * step) for p in range(ROW_SPLIT)]


def _ep_store(acc, rows, o_ref):
    o_ref[rows, :] = acc.astype(o_ref.dtype)


def _ep_residual(acc, rows, o_ref, r_ref):
    o_ref[rows, :] = (r_ref[rows, :] + acc).astype(o_ref.dtype)


def _ep_sigmoid(acc, rows, o_ref, b_ref):
    o_ref[rows, :] = _sigmoid(acc + b_ref[...]).astype(o_ref.dtype)


def _ep_rotary(acc, rows, o_ref, c_ref, s1_ref, s2_ref):
    c, s1, s2 = c_ref[rows, :], s1_ref[rows, :], s2_ref[rows, :]
    half = ROT_DIM // 2
    for h in range(acc.shape[1] // HEAD_DIM):
        a = acc[:, h * HEAD_DIM:(h + 1) * HEAD_DIM]
        r = a * c + pltpu.roll(a, HEAD_DIM - half, 1) * s1 + pltpu.roll(a, half, 1) * s2
        o_ref[rows, h * HEAD_DIM:(h + 1) * HEAD_DIM] = r.astype(o_ref.dtype)


def _mm_rows_body(*refs, n_extra, epilogue):
    a_ref, w_ref = refs[0], refs[1]
    extra = refs[2:2 + n_extra]
    o_ref = refs[2 + n_extra]
    wb = w_ref[...].astype(BF16)
    for rows in _row_parts(a_ref.shape[0]):
        acc = jnp.dot(a_ref[rows, :], wb, preferred_element_type=F32)
        epilogue(acc, rows, o_ref, *extra)


def matmul_rows(a, w, *, col_off, n_cols, bm, bn, out_dtype, epilogue=_ep_store,
                extras=(), extra_specs=(), vmem_mib=56, name="matmul_rows"):
    m, k = a.shape
    off = col_off // bn
    assert col_off % bn == 0 and n_cols % bn == 0 and m % bm == 0
    body = functools.partial(_mm_rows_body, n_extra=len(extras), epilogue=epilogue)
    return pl.pallas_call(
        body,
        grid=(m // bm, n_cols // bn),
        in_specs=[pl.BlockSpec((bm, k), lambda i, j: (i, 0)),
                  pl.BlockSpec((k, bn), lambda i, j: (0, j + off))] + list(extra_specs),
        out_specs=pl.BlockSpec((bm, bn), lambda i, j: (i, j)),
        out_shape=jax.ShapeDtypeStruct((m, n_cols), out_dtype),
        compiler_params=_params(("arbitrary", "arbitrary"), vmem_mib),
        name=name,
    )(a, w, *extras)


def _softmax_rows(s):
    e = jnp.exp(s - s.max(axis=1, keepdims=True))
    return e * (1.0 / e.sum(axis=1, keepdims=True))


WIN_UNROLL = 4


def _win_body(sink_ref, q_ref, k_ref, v_ref, o_ref, *, seq):
    kv = pl.program_id(1)
    nb = seq // A_BLOCK
    scale = HEAD_DIM ** -0.5 * LOG2E
    rows = A_GROUP * A_BLOCK
    qi = lax.broadcasted_iota(jnp.int32, (rows, A_BLOCK), 0) % A_BLOCK
    ci = lax.broadcasted_iota(jnp.int32, (rows, A_BLOCK), 1)
    sink_b = jnp.concatenate(
        [jnp.full((A_BLOCK, HEAD_DIM), sink_ref[kv * A_GROUP + g] * LOG2E, F32)
         for g in range(A_GROUP)], axis=0)

    def scores(n):
        r0 = pl.multiple_of(n * A_BLOCK, A_BLOCK)
        rp = pl.multiple_of(jnp.maximum(n - 1, 0) * A_BLOCK, A_BLOCK)
        rn = pl.multiple_of(jnp.minimum(n + 1, nb - 1) * A_BLOCK, A_BLOCK)
        off_p = jnp.where(n > 0, 0, 2 * A_BLOCK)
        off_n = jnp.where(n < nb - 1, 0, 2 * A_BLOCK)
        q = jnp.concatenate(
            [q_ref[pl.ds(r0, A_BLOCK), g * HEAD_DIM:(g + 1) * HEAD_DIM] for g in range(A_GROUP)],
            axis=0)
        sp = lax.dot_general(q, k_ref[pl.ds(rp, A_BLOCK), :], _NT, preferred_element_type=F32) * scale
        sc = lax.dot_general(q, k_ref[pl.ds(r0, A_BLOCK), :], _NT, preferred_element_type=F32) * scale
        sn = lax.dot_general(q, k_ref[pl.ds(rn, A_BLOCK), :], _NT, preferred_element_type=F32) * scale
        sp = jnp.where(ci >= qi + off_p, sp, NEG)
        sn = jnp.where(ci <= qi - off_n, sn, NEG)
        return (rp, r0, rn), (sp, sc, sn)

    def exps(parts):
        m = jnp.maximum(jnp.maximum(parts[0], parts[1]), parts[2]).max(axis=1, keepdims=True)
        m = jnp.maximum(jnp.broadcast_to(m, sink_b.shape), sink_b)
        es = [jnp.exp2(p - m) for p in parts]
        den = (es[0] + es[1] + es[2]).sum(axis=1, keepdims=True)
        den = jnp.broadcast_to(den, sink_b.shape) + jnp.exp2(sink_b - m)
        return [e.astype(BF16) for e in es], 1.0 / den

    def body(it, carry):
        blocks = [scores(it * WIN_UNROLL + u) for u in range(WIN_UNROLL)]
        probs = [exps(parts) for _, parts in blocks]
        for (rows_kv, _), (es, inv) in zip(blocks, probs):
            o = jnp.dot(es[0], v_ref[pl.ds(rows_kv[0], A_BLOCK), :], preferred_element_type=F32)
            o = o + jnp.dot(es[1], v_ref[pl.ds(rows_kv[1], A_BLOCK), :], preferred_element_type=F32)
            o = o + jnp.dot(es[2], v_ref[pl.ds(rows_kv[2], A_BLOCK), :], preferred_element_type=F32)
            o = o * inv
            for g in range(A_GROUP):
                o_ref[pl.ds(rows_kv[1], A_BLOCK), g * HEAD_DIM:(g + 1) * HEAD_DIM] = (
                    o[g * A_BLOCK:(g + 1) * A_BLOCK].astype(o_ref.dtype))
        return carry

    lax.fori_loop(0, nb // WIN_UNROLL, body, 0)


def window_attention(qk, vqkv, sink, batch, seq):
    gw = A_GROUP * HEAD_DIM
    k_blk0 = QA_W // HEAD_DIM
    return pl.pallas_call(
        functools.partial(_win_body, seq=seq),
        grid=(batch, A_KV_HEADS),
        in_specs=[pl.BlockSpec(memory_space=pltpu.SMEM),
                  pl.BlockSpec((seq, gw), lambda b, h: (b, h)),
                  pl.BlockSpec((seq, HEAD_DIM), lambda b, h: (b, k_blk0 + h)),
                  pl.BlockSpec((seq, HEAD_DIM), lambda b, h: (b, h))],
        out_specs=pl.BlockSpec((seq, gw), lambda b, h: (b, h)),
        out_shape=jax.ShapeDtypeStruct((batch * seq, QA_W), BF16),
        compiler_params=_params(("arbitrary", "arbitrary"), 32),
        name="window_attention",
    )(sink, qk, qk, vqkv)


NBR_HG = 4
NBR_ROWS = 4


def _nbr_body(q_ref, k_ref, v_ref, bias_ref, o_ref, *, seq):
    rows = seq // GRID_W
    kh = min(NA_KH_MAX, rows)
    strip = kh * GRID_W
    scale = HEAD_DIM ** -0.5 * LOG2E

    def body(it, carry):
        units = []
        for rr in range(NBR_ROWS):
            r = it * NBR_ROWS + rr
            rs = jnp.clip(r - kh // 2, 0, rows - kh)
            q0 = pl.multiple_of(r * GRID_W, GRID_W)
            k0 = pl.multiple_of(rs * GRID_W, GRID_W)
            for h in range(NBR_HG):
                units.append((q0, k0, r - rs, h, slice(h * HEAD_DIM, (h + 1) * HEAD_DIM)))
        ss = [lax.dot_general(q_ref[pl.ds(q0, GRID_W), cols], k_ref[pl.ds(k0, strip), cols], _NT,
                              preferred_element_type=F32) * scale + bias_ref[h, var]
              for q0, k0, var, h, cols in units]
        ps = []
        for s in ss:
            e = jnp.exp2(s - s.max(axis=1, keepdims=True))
            ps.append((e.astype(BF16), 1.0 / e.sum(axis=1, keepdims=True)))
        for (q0, k0, var, h, cols), (e, inv) in zip(units, ps):
            o = jnp.dot(e, v_ref[pl.ds(k0, strip), cols], preferred_element_type=F32) * inv
            o_ref[pl.ds(q0, GRID_W), cols] = o.astype(o_ref.dtype)
        return carry

    lax.fori_loop(0, rows // NBR_ROWS, body, 0)


def neighbourhood_attention(vqkv, bias_tbl, batch, seq):
    gw = NBR_HG * HEAD_DIM
    q0, k0, v0 = KVA_W // gw, (KVA_W + QB_W) // gw, (KVA_W + 2 * QB_W) // gw
    kh = bias_tbl.shape[1]
    return pl.pallas_call(
        functools.partial(_nbr_body, seq=seq),
        grid=(B_HEADS // NBR_HG, batch),
        in_specs=[pl.BlockSpec((seq, gw), lambda g, b: (b, q0 + g)),
                  pl.BlockSpec((seq, gw), lambda g, b: (b, k0 + g)),
                  pl.BlockSpec((seq, gw), lambda g, b: (b, v0 + g)),
                  pl.BlockSpec((NBR_HG, kh, GRID_W, kh * GRID_W), lambda g, b: (g, 0, 0, 0))],
        out_specs=pl.BlockSpec((seq, gw), lambda g, b: (b, g)),
        out_shape=jax.ShapeDtypeStruct((batch * seq, QB_W), BF16),
        compiler_params=_params(("arbitrary", "arbitrary"), 40),
        name="neighbourhood_attention",
    )(vqkv, vqkv, vqkv, bias_tbl)


def _bias_table_body(rpb_ref, o_ref, *, kh):
    h = pl.program_id(0)
    n_dr, n_dc = 2 * NA_KH_MAX - 1, 2 * NA_KW - 1
    c = lax.broadcasted_iota(jnp.int32, (GRID_W, LANES), 0)
    lane = lax.broadcasted_iota(jnp.int32, (GRID_W, LANES), 1)
    kc = lane % GRID_W
    diff = jnp.clip(kc - c + NA_KW - 1, 0, n_dc - 1)
    cs = jnp.clip(c - NA_KW // 2, 0, GRID_W - NA_KW)
    col_ok = (kc >= cs) & (kc < cs + NA_KW)
    slabs = []
    for dr in range(n_dr):
        acc = jnp.zeros((GRID_W, LANES), F32)
        for d in range(n_dc):
            acc = jnp.where(diff == d, rpb_ref[(h * n_dr + dr) * n_dc + d], acc)
        slabs.append(jnp.where(col_ok, acc * LOG2E, NEG))
    left = lane < GRID_W
    for var in range(kh):
        for jp in range(kh * GRID_W // LANES):
            dr0 = 2 * jp - var + NA_KH_MAX - 1
            o_ref[0, var, :, jp * LANES:(jp + 1) * LANES] = jnp.where(left, slabs[dr0], slabs[dr0 + 1])


def _nbr_bias_table(rpb, seq):
    rows = seq // GRID_W
    kh = min(NA_KH_MAX, rows)
    heads = rpb.shape[0]
    assert kh == NA_KH_MAX and 2 * GRID_W == LANES
    return pl.pallas_call(
        functools.partial(_bias_table_body, kh=kh),
        grid=(heads,),
        in_specs=[pl.BlockSpec(memory_space=pltpu.SMEM)],
        out_specs=pl.BlockSpec((1, kh, GRID_W, kh * GRID_W), lambda h: (h, 0, 0, 0)),
        out_shape=jax.ShapeDtypeStruct((heads, kh, GRID_W, kh * GRID_W), F32),
        compiler_params=_params(("arbitrary",), 16),
        name="nbr_bias_table",
    )(rpb.astype(F32).reshape(-1))


def _merge_body(oa_ref, ob_ref, wa_ref, wb_ref, g0_ref, g1_ref, o_ref):
    wa, wb = wa_ref[...].astype(BF16), wb_ref[...].astype(BF16)
    for rows in _row_parts(oa_ref.shape[0]):
        ya = jnp.dot(oa_ref[rows, :], wa, preferred_element_type=F32)
        yb = jnp.dot(ob_ref[rows, :], wb, preferred_element_type=F32)
        o_ref[rows, :] = (g0_ref[rows, :].astype(F32) * ya
                          + g1_ref[rows, :].astype(F32) * yb).astype(o_ref.dtype)


def branch_merge(oa, ob, wa, wb, gates, bm=2048, bn=256):
    m, k = oa.shape
    n = wa.shape[1]
    g1_off = n // bn
    return pl.pallas_call(
        _merge_body,
        grid=(m // bm, n // bn),
        in_specs=[pl.BlockSpec((bm, k), lambda i, j: (i, 0)),
                  pl.BlockSpec((bm, k), lambda i, j: (i, 0)),
                  pl.BlockSpec((k, bn), lambda i, j: (0, j)),
                  pl.BlockSpec((k, bn), lambda i, j: (0, j)),
                  pl.BlockSpec((bm, bn), lambda i, j: (i, j)),
                  pl.BlockSpec((bm, bn), lambda i, j: (i, j + g1_off))],
        out_specs=pl.BlockSpec((bm, bn), lambda i, j: (i, j)),
        out_shape=jax.ShapeDtypeStruct((m, n), BF16),
        compiler_params=_params(("arbitrary", "arbitrary"), 56),
        name="branch_merge",
    )(oa, ob, wa, wb, gates, gates)


def _cast_body(x_ref, o_ref):
    o_ref[...] = x_ref[...].astype(o_ref.dtype)


def cast_bf16(w):
    r, c = w.shape
    return pl.pallas_call(
        _cast_body,
        grid=(1,),
        in_specs=[pl.BlockSpec((r, c), lambda i: (0, 0))],
        out_specs=pl.BlockSpec((r, c), lambda i: (0, 0)),
        out_shape=jax.ShapeDtypeStruct((r, c), BF16),
        compiler_params=_params(("arbitrary",), 40),
        name="cast_bf16",
    )(w)


def _router_probs(hn, wr_ref):
    hi = hn.astype(BF16)
    lo = (hn - hi.astype(F32)).astype(BF16)
    l_hi = jnp.dot(hi, wr_ref[...], preferred_element_type=F32)
    l_lo = jnp.dot(lo, wr_ref[...], preferred_element_type=F32)
    logits = l_hi + pltpu.roll(l_hi, LANES - N_EXPERTS, 1) + l_lo
    lane = lax.broadcasted_iota(jnp.int32, logits.shape, 1)
    logits = jnp.where(lane < N_EXPERTS, logits, NEG)
    return _softmax_rows(logits)


def _xblock_body(x_ref, gc_ref, wq_ref, k_ref, v_ref, wo_ref, gf_ref, wr_ref,
                 x2_ref, h3_ref, aff_ref):
    scale = HEAD_DIM ** -0.5
    half = x_ref.shape[1] // 2
    parts = [slice(p * XB_ROWS, (p + 1) * XB_ROWS) for p in range(x_ref.shape[0] // XB_ROWS)]
    h2s = [_rmsnorm_rows(x_ref[r, :], gc_ref[...]).astype(BF16) for r in parts]
    qs = [jnp.dot(h2, wq_ref[...], preferred_element_type=F32).astype(BF16) for h2 in h2s]
    os = []
    for q in qs:
        heads = []
        for h in range(X_HEADS):
            cols = slice(h * HEAD_DIM, (h + 1) * HEAD_DIM)
            s = lax.dot_general(q[:, cols], k_ref[:, cols], _NT, preferred_element_type=F32) * scale
            p = _softmax_rows(s)
            heads.append(jnp.dot(p.astype(BF16), v_ref[:, cols],
                                 preferred_element_type=F32).astype(BF16))
        os.append(jnp.concatenate(heads, axis=1))
    for r, o in zip(parts, os):
        x2_ref[r, :] = x_ref[r, :] + jnp.dot(o, wo_ref[...], preferred_element_type=F32)
    for r in parts:
        hn = _rmsnorm_rows(x2_ref[r, :], gf_ref[...])
        packed = pltpu.pack_elementwise([hn[:, :half], hn[:, half:]], packed_dtype=BF16)
        h3_ref[r, :] = pltpu.bitcast(packed, jnp.uint32)
        aff = _router_probs(hn, wr_ref)
        aff_ref[0, :, r] = aff.T[:N_EXPERTS, :]


XB_ROWS = 256


def cross_attention_block(x1, g_cross, wq, kx, vx, wo, g_ffn, w_router, batch, seq, mem_len, bm=512):
    m, d = x1.shape
    nt = seq // bm
    w_hi = w_router.astype(BF16)
    w_lo = (w_router - w_hi.astype(F32)).astype(BF16)
    wr = jnp.concatenate([w_hi, w_lo, jnp.zeros((d, LANES - 2 * N_EXPERTS), BF16)], axis=1)
    const = lambda i: (0, 0)
    once = pl.Buffered(1)
    return pl.pallas_call(
        _xblock_body,
        grid=(m // bm,),
        in_specs=[pl.BlockSpec((bm, d), lambda i: (i, 0)),
                  pl.BlockSpec((1, d), const),
                  pl.BlockSpec((d, X_W), const, pipeline_mode=once),
                  pl.BlockSpec((mem_len, X_W), lambda i: (i // nt, 0)),
                  pl.BlockSpec((mem_len, X_W), lambda i: (i // nt, 0)),
                  pl.BlockSpec((X_W, d), const, pipeline_mode=once),
                  pl.BlockSpec((1, d), const),
                  pl.BlockSpec((d, LANES), const, pipeline_mode=once)],
        out_specs=[pl.BlockSpec((bm, d), lambda i: (i, 0)),
                   pl.BlockSpec((bm, d // 2), lambda i: (i, 0)),
                   pl.BlockSpec((1, N_EXPERTS, bm), lambda i: (i // nt, 0, i % nt))],
        out_shape=[jax.ShapeDtypeStruct((m, d), F32),
                   jax.ShapeDtypeStruct((m, d // 2), jnp.uint32),
                   jax.ShapeDtypeStruct((batch, N_EXPERTS, seq), F32)],
        compiler_params=_params(("arbitrary",), 58),
        name="cross_attention_block",
    )(x1, g_cross.reshape(1, d), cast_bf16(wq), kx, vx, cast_bf16(wo), g_ffn.reshape(1, d), wr)


CUM_CHUNK = 256


def _excl_cumsum_lanes(x01, tri):
    n = x01.shape[1]
    carry = jnp.zeros((x01.shape[0], 1), F32)
    out = []
    for c in range(n // CUM_CHUNK):
        xc = x01[:, c * CUM_CHUNK:(c + 1) * CUM_CHUNK]
        out.append(jnp.dot(xc.astype(BF16), tri, preferred_element_type=F32) + carry)
        carry = carry + xc.sum(axis=1, keepdims=True)
    return jnp.concatenate(out, axis=1)


COMBINE_TS = 256
COMBINE_W = 64


def _topk_body(aff_ref, slot_ref, first_ref, *, cap):
    a = aff_ref[0]
    n_exp, s = a.shape
    capf = jnp.float32(cap)
    lane_s = lax.broadcasted_iota(jnp.int32, (n_exp, s), 1)

    def count(mask):
        return jnp.where(mask, 1.0, 0.0).sum(axis=1, keepdims=True)

    bits = pltpu.bitcast(a, jnp.int32)

    def search(i, t):
        cand = t | jnp.left_shift(jnp.int32(1), 30 - i)
        return jnp.where(count(bits >= cand) >= capf, cand, t)

    t = lax.fori_loop(0, 31, search, jnp.zeros((n_exp, 1), jnp.int32))
    at = jnp.where(bits == t, lane_s, s).astype(F32).min(axis=1, keepdims=True).astype(jnp.int32)
    pivot = jnp.where(lane_s == at, a, 0.0).sum(axis=1, keepdims=True)

    def stats(p):
        return p, count(a > p), count(a >= p)

    def wrong(state):
        _, n_gt, n_ge = state
        return jnp.where(jnp.logical_or(n_gt >= capf, n_ge < capf), 1.0, 0.0).sum() > 0.0

    def step(state):
        p, n_gt, n_ge = state
        up = jnp.where(a > p, a, jnp.inf).min(axis=1, keepdims=True)
        down = jnp.where(a < p, a, -jnp.inf).max(axis=1, keepdims=True)
        return stats(jnp.where(n_gt >= capf, up, jnp.where(n_ge < capf, down, p)))

    pivot, n_gt, _ = lax.while_loop(wrong, step, stats(pivot))
    ri = lax.broadcasted_iota(jnp.int32, (CUM_CHUNK, CUM_CHUNK), 0)
    cj = lax.broadcasted_iota(jnp.int32, (CUM_CHUNK, CUM_CHUNK), 1)
    tri = jnp.where(ri < cj, 1.0, 0.0).astype(BF16)
    eq = jnp.where(a == pivot, 1.0, 0.0)
    sel = jnp.where(a > pivot, 1.0, 0.0) + jnp.where(_excl_cumsum_lanes(eq, tri) < capf - n_gt, eq, 0.0)
    pos = _excl_cumsum_lanes(sel, tri)
    slot_ref[0] = jnp.where(sel > 0.5, pos, -1.0).astype(jnp.int32)
    lane_k = lax.broadcasted_iota(jnp.int32, (n_exp, LANES), 1)
    first = jnp.zeros((n_exp, LANES), F32)
    for k in range(s // COMBINE_TS):
        at_k = jnp.where(lane_s == k * COMBINE_TS, pos, 0.0).sum(axis=1, keepdims=True)
        first = jnp.where(lane_k == k, at_k, first)
    first_ref[0] = first.astype(jnp.int32)


def expert_topk(aff_t, cap):
    b, e, s = aff_t.shape
    return pl.pallas_call(
        functools.partial(_topk_body, cap=cap),
        grid=(b,),
        in_specs=[pl.BlockSpec((1, e, s), lambda i: (i, 0, 0))],
        out_specs=[pl.BlockSpec((1, e, s), lambda i: (i, 0, 0)),
                   pl.BlockSpec((1, e, LANES), lambda i: (i, 0, 0))],
        out_shape=[jax.ShapeDtypeStruct((b, e, s), jnp.int32),
                   jax.ShapeDtypeStruct((b, e, LANES), jnp.int32)],
        compiler_params=_params(("arbitrary",), 32),
        name="expert_topk",
    )(aff_t)


def _slot_index_body(slot_ref, aff_ref, idx_ref, val_ref, *, cap):
    b = pl.program_id(0)
    n_exp, s = slot_ref.shape[1], slot_ref.shape[2]
    ci = lax.broadcasted_iota(jnp.int32, (cap, s), 0)
    tok = lax.broadcasted_iota(jnp.int32, (cap, s), 1).astype(F32)
    base = (b * s).astype(F32)
    for e in range(n_exp):
        hit = slot_ref[0, e:e + 1, :] == ci
        idx = jnp.where(hit, tok, 0.0).sum(axis=1, keepdims=True) + base
        idx_ref[e] = idx.astype(jnp.int32)
        val_ref[e] = jnp.where(hit, aff_ref[0, e:e + 1, :], 0.0).sum(axis=1, keepdims=True)


def slot_index(slot, aff_t, cap):
    b, e, s = slot.shape
    return pl.pallas_call(
        functools.partial(_slot_index_body, cap=cap),
        grid=(b,),
        in_specs=[pl.BlockSpec((1, e, s), lambda bi: (bi, 0, 0)),
                  pl.BlockSpec((1, e, s), lambda bi: (bi, 0, 0))],
        out_specs=[pl.BlockSpec((e, cap, 1), lambda bi: (0, bi, 0)),
                   pl.BlockSpec((e, cap, 1), lambda bi: (0, bi, 0))],
        out_shape=[jax.ShapeDtypeStruct((e, b * cap, 1), jnp.int32),
                   jax.ShapeDtypeStruct((e, b * cap, 1), F32)],
        compiler_params=_params(("arbitrary",), 32),
        name="slot_index",
    )(slot, aff_t)


UNPACK_ROWS = 128


def _row_copy(idx_ref, hp_ref, gbuf, sem, expert, rows, row):
    tok = idx_ref[expert * rows + row]
    return pltpu.make_async_copy(hp_ref.at[pl.ds(tok, 1)], gbuf.at[pl.ds(row, 1)], sem)


def _expert_up_body(idx_ref, hp_ref, wg_ref, wu_ref, o_ref, gbuf, xbf, sem, *, rows, per_step):
    e, f = pl.program_id(0), pl.program_id(1)
    n_e, n_f = pl.num_programs(0), pl.num_programs(1)
    half = gbuf.shape[1]

    def wait_all_rows():
        pltpu.make_async_copy(hp_ref.at[pl.ds(0, rows)], gbuf, sem).wait()

    @pl.when(jnp.logical_and(e == 0, f == 0))
    def _():
        def body(r, carry):
            _row_copy(idx_ref, hp_ref, gbuf, sem, 0, rows, r).start()
            return carry
        lax.fori_loop(0, rows, body, 0)

    @pl.when(f == 0)
    def _():
        wait_all_rows()

        def unpack(k, carry):
            r = pl.multiple_of(k * UNPACK_ROWS, UNPACK_ROWS)
            w = gbuf[pl.ds(r, UNPACK_ROWS), :]
            lo = pltpu.unpack_elementwise(w, index=0, packed_dtype=BF16, unpacked_dtype=F32)
            hi = pltpu.unpack_elementwise(w, index=1, packed_dtype=BF16, unpacked_dtype=F32)
            xbf[pl.ds(r, UNPACK_ROWS), :half] = lo.astype(BF16)
            xbf[pl.ds(r, UNPACK_ROWS), half:] = hi.astype(BF16)
            return carry
        lax.fori_loop(0, rows // UNPACK_ROWS, unpack, 0)

    nxt = jnp.minimum(e + 1, n_e - 1)
    for r in range(per_step):
        _row_copy(idx_ref, hp_ref, gbuf, sem, nxt, rows, f * per_step + r).start()

    wg, wu = wg_ref[0].astype(BF16), wu_ref[0].astype(BF16)
    for part in _row_parts(rows):
        a = jnp.dot(xbf[part, :], wg, preferred_element_type=F32)
        u = jnp.dot(xbf[part, :], wu, preferred_element_type=F32)
        o_ref[0, part, :] = (a * _sigmoid(a) * u).astype(o_ref.dtype)

    @pl.when(jnp.logical_and(e == n_e - 1, f == n_f - 1))
    def _():
        wait_all_rows()


def expert_up(idx, hp, w_gate, w_up, rows, tf=256):
    n_e, d, f = w_gate.shape
    assert hp.shape[1] * 2 == d and rows % (f // tf) == 0
    grid_spec = pltpu.PrefetchScalarGridSpec(
        num_scalar_prefetch=1,
        grid=(n_e, f // tf),
        in_specs=[pl.BlockSpec(memory_space=pl.ANY),
                  pl.BlockSpec((1, d, tf), lambda ei, fi, idx_ref: (ei, 0, fi)),
                  pl.BlockSpec((1, d, tf), lambda ei, fi, idx_ref: (ei, 0, fi))],
        out_specs=pl.BlockSpec((1, rows, tf), lambda ei, fi, idx_ref: (ei, 0, fi)),
        scratch_shapes=[pltpu.VMEM((rows, d // 2), jnp.uint32),
                        pltpu.VMEM((rows, d), BF16),
                        pltpu.SemaphoreType.DMA(())],
    )
    return pl.pallas_call(
        functools.partial(_expert_up_body, rows=rows, per_step=rows // (f // tf)),
        grid_spec=grid_spec,
        out_shape=jax.ShapeDtypeStruct((n_e, rows, f), BF16),
        compiler_params=_params(("arbitrary", "arbitrary"), 56),
        name="expert_up",
    )(idx, hp, w_gate, w_up)


def _expert_down_body(h_ref, wd_ref, val_ref, o_ref):
    wd = wd_ref[0].astype(BF16)
    for part in _row_parts(h_ref.shape[1]):
        y = jnp.dot(h_ref[0, part, :], wd, preferred_element_type=F32)
        o_ref[0, part, :] = (y * val_ref[0, part, :]).astype(o_ref.dtype)


def expert_down(hmid, w_down, valc, tn=1024):
    e, rows, f = hmid.shape
    d = w_down.shape[-1]
    return pl.pallas_call(
        _expert_down_body,
        grid=(e, d // tn),
        in_specs=[pl.BlockSpec((1, rows, f), lambda ei, ni: (ei, 0, 0)),
                  pl.BlockSpec((1, f, tn), lambda ei, ni: (ei, 0, ni)),
                  pl.BlockSpec((1, rows, 1), lambda ei, ni: (ei, 0, 0))],
        out_specs=pl.BlockSpec((1, rows, tn), lambda ei, ni: (ei, 0, ni)),
        out_shape=jax.ShapeDtypeStruct((e, rows, d), BF16),
        compiler_params=_params(("arbitrary", "arbitrary"), 48),
        name="expert_down",
    )(hmid, w_down, valc)


COMBINE_TN = 512
ROW_ALIGN_BF16 = 16


def _combine_windows(first, cap, n_rows):
    b, e, nt = first.shape
    row0 = jnp.arange(b, dtype=jnp.int32)[:, None, None] * cap
    lo = first + row0
    hi = jnp.concatenate([first[:, :, 1:], jnp.full((b, e, 1), cap, jnp.int32)], axis=2) + row0
    start = jnp.minimum(lo // ROW_ALIGN_BF16 * ROW_ALIGN_BF16, n_rows - COMBINE_W)
    fast = jnp.all(hi - start <= COMBINE_W, axis=1)
    return start.transpose(0, 2, 1).reshape(-1), fast.astype(jnp.int32).reshape(-1)


def _combine_body(win_ref, fast_ref, slot_ref, y_hbm, x_ref, g_ref, o_ref,
                  ywin, oht, ybuf, ohs, wsem, ssem, *, cap, n_tiles):
    i = pl.program_id(0)
    n_exp = slot_ref.shape[1]
    ts, d = o_ref.shape
    w = COMBINE_W
    row0 = (i // n_tiles) * cap
    buf = i % 2

    def window_copy(step, e, b):
        start = pl.multiple_of(win_ref[step * n_exp + e], ROW_ALIGN_BF16)
        return pltpu.make_async_copy(y_hbm.at[e, pl.ds(start, w)], ywin.at[b, pl.ds(e * w, w)],
                                     wsem.at[b])

    @pl.when(i == 0)
    def _():
        for e in range(n_exp):
            window_copy(0, e, 0).start()

    @pl.when(i + 1 < pl.num_programs(0))
    def _():
        for e in range(n_exp):
            window_copy(i + 1, e, 1 - buf).start()

    pltpu.make_async_copy(y_hbm.at[0, pl.ds(0, n_exp * w)], ywin.at[buf], wsem.at[buf]).wait()

    @pl.when(fast_ref[i] == 1)
    def _():
        row = lax.broadcasted_iota(jnp.int32, (2 * w, ts), 0)
        upper = row >= w
        j = row % w
        for p in range(n_exp // 2):
            rel_a = slot_ref[0, 2 * p:2 * p + 1, :] + (row0 - win_ref[i * n_exp + 2 * p])
            rel_b = slot_ref[0, 2 * p + 1:2 * p + 2, :] + (row0 - win_ref[i * n_exp + 2 * p + 1])
            hit = jnp.where(upper, rel_b, rel_a) == j
            oht[:, p * 2 * w:(p + 1) * 2 * w] = jnp.where(hit, 1.0, 0.0).T.astype(BF16)
        for c in range(d // COMBINE_TN):
            cols = slice(c * COMBINE_TN, (c + 1) * COMBINE_TN)
            o_ref[:, cols] = x_ref[:, cols] + jnp.dot(oht[...], ywin[buf, :, cols],
                                                      preferred_element_type=F32)

    @pl.when(fast_ref[i] == 0)
    def _():
        o_ref[...] = x_ref[...]
        ci = lax.broadcasted_iota(jnp.int32, (cap, ts), 0)

        def body(e, carry):
            cp = pltpu.make_async_copy(y_hbm.at[e, pl.ds(pl.multiple_of(row0, cap), cap)], ybuf, ssem)
            cp.start()
            cp.wait()
            srow = slot_ref[0, pl.ds(e, 1), :]
            ohs[...] = jnp.where(srow == ci, 1.0, 0.0).T.astype(BF16)
            for c in range(d // COMBINE_TN):
                cols = slice(c * COMBINE_TN, (c + 1) * COMBINE_TN)
                o_ref[:, cols] += jnp.dot(ohs[...], ybuf[:, cols], preferred_element_type=F32)
            return carry
        lax.fori_loop(0, n_exp, body, 0)

    o_ref[...] = _rmsnorm_rows(o_ref[...], g_ref[...])


def expert_combine(slot, first, y, x2d, g, cap):
    b, e, s = slot.shape
    d = x2d.shape[-1]
    ts, w = COMBINE_TS, COMBINE_W
    nt = s // ts
    assert e % 2 == 0 and y.shape[1] >= e * w and cap % ROW_ALIGN_BF16 == 0
    win, fast = _combine_windows(first[:, :, :nt], cap, y.shape[1])
    grid_spec = pltpu.PrefetchScalarGridSpec(
        num_scalar_prefetch=2,
        grid=(b * nt,),
        in_specs=[pl.BlockSpec((1, e, ts), lambda i, win_ref, fast_ref: (i // nt, 0, i % nt)),
                  pl.BlockSpec(memory_space=pl.ANY),
                  pl.BlockSpec((ts, d), lambda i, win_ref, fast_ref: (i, 0)),
                  pl.BlockSpec((1, d), lambda i, win_ref, fast_ref: (0, 0))],
        out_specs=pl.BlockSpec((ts, d), lambda i, win_ref, fast_ref: (i, 0)),
        scratch_shapes=[pltpu.VMEM((2, e * w, d), BF16),
                        pltpu.VMEM((ts, e * w), BF16),
                        pltpu.VMEM((cap, d), BF16),
                        pltpu.VMEM((ts, cap), BF16),
                        pltpu.SemaphoreType.DMA((2,)),
                        pltpu.SemaphoreType.DMA(())],
    )
    return pl.pallas_call(
        functools.partial(_combine_body, cap=cap, n_tiles=nt),
        grid_spec=grid_spec,
        out_shape=jax.ShapeDtypeStruct(x2d.shape, F32),
        compiler_params=_params(("arbitrary",), 48),
        name="expert_combine",
    )(win, fast, slot, y, x2d, g.reshape(1, d))


def _rotary_tables(seq):
    half = ROT_DIM // 2
    inv = ROPE_THETA ** (-jnp.arange(half, dtype=F32) * 2.0 / ROT_DIM)
    ang = jnp.arange(seq).astype(F32)[:, None] * inv[None, :]
    cos, sin = jnp.cos(ang), jnp.sin(ang)
    ones = jnp.ones((seq, HEAD_DIM - ROT_DIM), F32)
    zeros = jnp.zeros((seq, HEAD_DIM - ROT_DIM), F32)
    zh = jnp.zeros((seq, half), F32)
    c = jnp.concatenate([cos, cos, ones], axis=1)
    s1 = jnp.concatenate([-sin, zh, zeros], axis=1)
    s2 = jnp.concatenate([zh, sin, zeros], axis=1)
    return c, s1, s2


def kernel(x, mem, norm_mix, w_in, b_gate, sink, rpb, w_branch_a, w_branch_b, w_out,
           norm_cross, norm_mem, wq_x, wk_x, wv_x, wo_x, norm_ffn, w_router,
           w_gate, w_up, w_down, norm_final):
    batch, seq, d = x.shape
    mem_len = mem.shape[1]
    m = batch * seq
    assert norm_mix.shape[0] == 1, "final RMSNorm is fused into the single layer's last kernel"
    cap = EC_CAPACITY * seq // N_EXPERTS
    bm, bn = 2048, 256
    sb = seq // bm
    x0 = x.reshape(m, d)

    h = rmsnorm(x0, norm_mix[0], BF16)
    rot = _rotary_tables(seq)
    rot_specs = [pl.BlockSpec((bm, HEAD_DIM), lambda i, j: (i % sb, 0))] * 3
    qk = matmul_rows(h, w_in[0], col_off=0, n_cols=QA_W + KVA_W, bm=bm, bn=bn, out_dtype=BF16,
                     epilogue=_ep_rotary, extras=rot, extra_specs=rot_specs, name="in_proj_rotary")
    vqkv = matmul_rows(h, w_in[0], col_off=QA_W + KVA_W, n_cols=KVA_W + 3 * QB_W, bm=bm, bn=bn,
                       out_dtype=BF16, name="in_proj_plain")
    g_off = QA_W + 2 * KVA_W + 3 * QB_W
    gates = matmul_rows(h, w_in[0], col_off=g_off, n_cols=2 * d, bm=bm, bn=bn, out_dtype=BF16,
                        epilogue=_ep_sigmoid, extras=(b_gate[0].reshape(1, 2 * d),),
                        extra_specs=[pl.BlockSpec((1, bn), lambda i, j: (0, j))], name="in_proj_gates")
    oa = window_attention(qk, vqkv, sink[0], batch, seq)
    ob = neighbourhood_attention(vqkv, _nbr_bias_table(rpb[0], seq), batch, seq)
    merged = branch_merge(oa, ob, w_branch_a[0], w_branch_b[0], gates, bm=bm, bn=bn)
    res_spec = [pl.BlockSpec((bm, bn), lambda i, j: (i, j))]
    x1 = matmul_rows(merged, w_out[0], col_off=0, n_cols=d, bm=bm, bn=bn, out_dtype=F32,
                     epilogue=_ep_residual, extras=(x0,), extra_specs=res_spec, name="out_proj")

    mem_rows = batch * mem_len
    mn = rmsnorm(mem.reshape(mem_rows, d), norm_mem[0], BF16)
    kx = matmul_rows(mn, wk_x[0], col_off=0, n_cols=X_W, bm=mem_rows, bn=bn, out_dtype=BF16, name="xattn_k")
    vx = matmul_rows(mn, wv_x[0], col_off=0, n_cols=X_W, bm=mem_rows, bn=bn, out_dtype=BF16, name="xattn_v")
    x2, h3p, aff_t = cross_attention_block(
        x1, norm_cross[0], wq_x[0], kx, vx, wo_x[0], norm_ffn[0], w_router[0], batch, seq, mem_len)

    slot, first = expert_topk(aff_t, cap)
    idx, valc = slot_index(slot, aff_t, cap)
    hmid = expert_up(idx.reshape(-1), h3p, w_gate[0], w_up[0], batch * cap)
    y = expert_down(hmid, w_down[0], valc)
    out = expert_combine(slot, first, y, x2, norm_final, cap)
    return out.reshape(batch, seq, d)
```

```python
import functools

import jax
import jax.numpy as jnp
from jax import lax
from jax.experimental import pallas as pl
from jax.experimental.pallas import tpu as pltpu

F32 = jnp.float32
BF16 = jnp.bfloat16

HEAD_DIM = 128
A_HEADS = 16
A_KV_HEADS = 4
A_GROUP = A_HEADS // A_KV_HEADS
WINDOW = 128
A_BLOCK = 128
ROT_DIM = HEAD_DIM // 4
ROPE_THETA = 500000.0
B_HEADS = 16
GRID_W = 64
NA_KH_MAX = 8
NA_KW = 16
X_HEADS = 4
N_EXPERTS = 16
EC_CAPACITY = 2
EPS = 1e-6
NEG = -1e30
LOG2E = 1.4426950408889634
LANES = 128
MIB = 1024 * 1024

QA_W = A_HEADS * HEAD_DIM
KVA_W = A_KV_HEADS * HEAD_DIM
QB_W = B_HEADS * HEAD_DIM
X_W = X_HEADS * HEAD_DIM

_NT = (((1,), (1,)), ((), ()))
_TN = (((0,), (0,)), ((), ()))


def _params(semantics, vmem_mib):
    return pltpu.CompilerParams(dimension_semantics=semantics,
                                vmem_limit_bytes=vmem_mib * MIB)


def _rmsnorm_rows(x, g):
    ms = jnp.mean(x * x, axis=-1, keepdims=True)
    return x * lax.rsqrt(ms + EPS) * g


def _rmsnorm_body(x_ref, g_ref, o_ref):
    o_ref[...] = _rmsnorm_rows(x_ref[...], g_ref[...]).astype(o_ref.dtype)


def rmsnorm(x2d, g, out_dtype, bm=512):
    m, d = x2d.shape
    return pl.pallas_call(
        _rmsnorm_body,
        grid=(m // bm,),
        in_specs=[pl.BlockSpec((bm, d), lambda i: (i, 0)),
                  pl.BlockSpec((1, d), lambda i: (0, 0))],
        out_specs=pl.BlockSpec((bm, d), lambda i: (i, 0)),
        out_shape=jax.ShapeDtypeStruct((m, d), out_dtype),
        compiler_params=_params(("arbitrary",), 40),
        name="rmsnorm",
    )(x2d, g.reshape(1, d))


def _sigmoid(x):
    return 0.5 * jnp.tanh(0.5 * x) + 0.5


ROW_SPLIT = 4


def _row_parts(n_rows, split=ROW_SPLIT):
    step = n_rows // split
    return [slice(p * step, (p + 1) * step) for p in range(split)]


def _ep_store(acc, rows, o_ref):
    o_ref[rows, :] = acc.astype(o_ref.dtype)


def _ep_residual(acc, rows, o_ref, r_ref):
    o_ref[rows, :] = (r_ref[rows, :] + acc).astype(o_ref.dtype)


def _ep_sigmoid(acc, rows, o_ref, b_ref):
    o_ref[rows, :] = _sigmoid((acc + b_ref[...]).astype(o_ref.dtype))


def _ep_rotary(acc, rows, o_ref, c_ref, s1_ref, s2_ref):
    c, s1, s2 = c_ref[rows, :], s1_ref[rows, :], s2_ref[rows, :]
    half = ROT_DIM // 2
    for h in range(acc.shape[1] // HEAD_DIM):
        a = acc[:, h * HEAD_DIM:(h + 1) * HEAD_DIM]
        r = a * c + pltpu.roll(a, HEAD_DIM - half, 1) * s1 + pltpu.roll(a, half, 1) * s2
        o_ref[rows, h * HEAD_DIM:(h + 1) * HEAD_DIM] = r.astype(o_ref.dtype)


def _mm_rows_body(*refs, n_extra, epilogue, row_split):
    a_ref, w_ref = refs[0], refs[1]
    extra = refs[2:2 + n_extra]
    o_ref = refs[2 + n_extra]
    wb = w_ref[...].astype(BF16)
    for rows in _row_parts(a_ref.shape[0], row_split):
        acc = jnp.dot(a_ref[rows, :], wb, preferred_element_type=F32)
        epilogue(acc, rows, o_ref, *extra)


def matmul_rows(a, w, *, col_off, n_cols, bm, bn, out_dtype, epilogue=_ep_store,
                extras=(), extra_specs=(), row_split=ROW_SPLIT, vmem_mib=56, name="matmul_rows"):
    m, k = a.shape
    off = col_off // bn
    assert col_off % bn == 0 and n_cols % bn == 0 and m % bm == 0
    body = functools.partial(_mm_rows_body, n_extra=len(extras), epilogue=epilogue, row_split=row_split)
    return pl.pallas_call(
        body,
        grid=(m // bm, n_cols // bn),
        in_specs=[pl.BlockSpec((bm, k), lambda i, j: (i, 0)),
                  pl.BlockSpec((k, bn), lambda i, j: (0, j + off))] + list(extra_specs),
        out_specs=pl.BlockSpec((bm, bn), lambda i, j: (i, j)),
        out_shape=jax.ShapeDtypeStruct((m, n_cols), out_dtype),
        compiler_params=_params(("arbitrary", "arbitrary"), vmem_mib),
        name=name,
    )(a, w, *extras)


def _softmax_rows(s):
    e = jnp.exp(s - s.max(axis=1, keepdims=True))
    return e * (1.0 / e.sum(axis=1, keepdims=True))


WIN_UNROLL = 8


def _win_body(sink_ref, q_ref, k_ref, v_ref, o_ref, *, seq):
    kv = pl.program_id(1)
    nb = seq // A_BLOCK
    scale = HEAD_DIM ** -0.5 * LOG2E
    rows = A_GROUP * A_BLOCK
    qi = lax.broadcasted_iota(jnp.int32, (rows, A_BLOCK), 0) % A_BLOCK
    ci = lax.broadcasted_iota(jnp.int32, (rows, A_BLOCK), 1)
    sink_b = jnp.concatenate(
        [jnp.full((A_BLOCK, HEAD_DIM), sink_ref[kv * A_GROUP + g] * LOG2E, F32)
         for g in range(A_GROUP)], axis=0)

    def scores(n):
        r0 = pl.multiple_of(n * A_BLOCK, A_BLOCK)
        rp = pl.multiple_of(jnp.maximum(n - 1, 0) * A_BLOCK, A_BLOCK)
        rn = pl.multiple_of(jnp.minimum(n + 1, nb - 1) * A_BLOCK, A_BLOCK)
        off_p = jnp.where(n > 0, 0, 2 * A_BLOCK)
        off_n = jnp.where(n < nb - 1, 0, 2 * A_BLOCK)
        q = jnp.concatenate(
            [q_ref[pl.ds(r0, A_BLOCK), g * HEAD_DIM:(g + 1) * HEAD_DIM] for g in range(A_GROUP)],
            axis=0)
        sp = lax.dot_general(q, k_ref[pl.ds(rp, A_BLOCK), :], _NT, preferred_element_type=F32) * scale
        sc = lax.dot_general(q, k_ref[pl.ds(r0, A_BLOCK), :], _NT, preferred_element_type=F32) * scale
        sn = lax.dot_general(q, k_ref[pl.ds(rn, A_BLOCK), :], _NT, preferred_element_type=F32) * scale
        sp = jnp.where(ci >= qi + off_p, sp, NEG)
        sn = jnp.where(ci <= qi - off_n, sn, NEG)
        return (rp, r0, rn), (sp, sc, sn)

    def exps(parts):
        m = jnp.maximum(jnp.maximum(parts[0], parts[1]), parts[2]).max(axis=1, keepdims=True)
        m = jnp.maximum(jnp.broadcast_to(m, sink_b.shape), sink_b)
        es = [jnp.exp2(p - m) for p in parts]
        den = (es[0] + es[1] + es[2]).sum(axis=1, keepdims=True)
        den = jnp.broadcast_to(den, sink_b.shape) + jnp.exp2(sink_b - m)
        return [e.astype(BF16) for e in es], 1.0 / den

    def body(it, carry):
        blocks = [scores(it * WIN_UNROLL + u) for u in range(WIN_UNROLL)]
        probs = [exps(parts) for _, parts in blocks]
        for (rows_kv, _), (es, inv) in zip(blocks, probs):
            o = jnp.dot(es[0], v_ref[pl.ds(rows_kv[0], A_BLOCK), :], preferred_element_type=F32)
            o = o + jnp.dot(es[1], v_ref[pl.ds(rows_kv[1], A_BLOCK), :], preferred_element_type=F32)
            o = o + jnp.dot(es[2], v_ref[pl.ds(rows_kv[2], A_BLOCK), :], preferred_element_type=F32)
            o = o * inv
            for g in range(A_GROUP):
                o_ref[pl.ds(rows_kv[1], A_BLOCK), g * HEAD_DIM:(g + 1) * HEAD_DIM] = (
                    o[g * A_BLOCK:(g + 1) * A_BLOCK].astype(o_ref.dtype))
        return carry

    lax.fori_loop(0, nb // WIN_UNROLL, body, 0)


def window_attention(qk, vqkv, sink, batch, seq):
    gw = A_GROUP * HEAD_DIM
    k_blk0 = QA_W // HEAD_DIM
    return pl.pallas_call(
        functools.partial(_win_body, seq=seq),
        grid=(batch, A_KV_HEADS),
        in_specs=[pl.BlockSpec(memory_space=pltpu.SMEM),
                  pl.BlockSpec((seq, gw), lambda b, h: (b, h)),
                  pl.BlockSpec((seq, HEAD_DIM), lambda b, h: (b, k_blk0 + h)),
                  pl.BlockSpec((seq, HEAD_DIM), lambda b, h: (b, h))],
        out_specs=pl.BlockSpec((seq, gw), lambda b, h: (b, h)),
        out_shape=jax.ShapeDtypeStruct((batch * seq, QA_W), BF16),
        compiler_params=_params(("arbitrary", "arbitrary"), 32),
        name="window_attention",
    )(sink, qk, qk, vqkv)


NBR_HG = 4
NBR_ROWS = 4


def _nbr_body(q_ref, k_ref, v_ref, bias_ref, o_ref, *, seq):
    rows = seq // GRID_W
    kh = min(NA_KH_MAX, rows)
    strip = kh * GRID_W
    scale = HEAD_DIM ** -0.5 * LOG2E

    def body(it, carry):
        units = []
        for rr in range(NBR_ROWS):
            r = it * NBR_ROWS + rr
            rs = jnp.clip(r - kh // 2, 0, rows - kh)
            q0 = pl.multiple_of(r * GRID_W, GRID_W)
            k0 = pl.multiple_of(rs * GRID_W, GRID_W)
            for h in range(NBR_HG):
                units.append((q0, k0, r - rs, h, slice(h * HEAD_DIM, (h + 1) * HEAD_DIM)))
        ss = [lax.dot_general(q_ref[pl.ds(q0, GRID_W), cols], k_ref[pl.ds(k0, strip), cols], _NT,
                              preferred_element_type=F32) * scale + bias_ref[h, var]
              for q0, k0, var, h, cols in units]
        ps = []
        for s in ss:
            e = jnp.exp2(s - s.max(axis=1, keepdims=True))
            ps.append((e.astype(BF16), 1.0 / e.sum(axis=1, keepdims=True)))
        for (q0, k0, var, h, cols), (e, inv) in zip(units, ps):
            o = jnp.dot(e, v_ref[pl.ds(k0, strip), cols], preferred_element_type=F32) * inv
            o_ref[pl.ds(q0, GRID_W), cols] = o.astype(o_ref.dtype)
        return carry

    lax.fori_loop(0, rows // NBR_ROWS, body, 0)


def neighbourhood_attention(vqkv, bias_tbl, batch, seq):
    gw = NBR_HG * HEAD_DIM
    q0, k0, v0 = KVA_W // gw, (KVA_W + QB_W) // gw, (KVA_W + 2 * QB_W) // gw
    kh = bias_tbl.shape[1]
    return pl.pallas_call(
        functools.partial(_nbr_body, seq=seq),
        grid=(B_HEADS // NBR_HG, batch),
        in_specs=[pl.BlockSpec((seq, gw), lambda g, b: (b, q0 + g)),
                  pl.BlockSpec((seq, gw), lambda g, b: (b, k0 + g)),
                  pl.BlockSpec((seq, gw), lambda g, b: (b, v0 + g)),
                  pl.BlockSpec((NBR_HG, kh, GRID_W, kh * GRID_W), lambda g, b: (g, 0, 0, 0))],
        out_specs=pl.BlockSpec((seq, gw), lambda g, b: (b, g)),
        out_shape=jax.ShapeDtypeStruct((batch * seq, QB_W), BF16),
        compiler_params=_params(("arbitrary", "arbitrary"), 40),
        name="neighbourhood_attention",
    )(vqkv, vqkv, vqkv, bias_tbl)


def _bias_table_body(rpb_ref, o_ref, *, kh):
    h = pl.program_id(0)
    n_dr, n_dc = 2 * NA_KH_MAX - 1, 2 * NA_KW - 1
    c = lax.broadcasted_iota(jnp.int32, (GRID_W, LANES), 0)
    lane = lax.broadcasted_iota(jnp.int32, (GRID_W, LANES), 1)
    kc = lane % GRID_W
    diff = jnp.clip(kc - c + NA_KW - 1, 0, n_dc - 1)
    cs = jnp.clip(c - NA_KW // 2, 0, GRID_W - NA_KW)
    col_ok = (kc >= cs) & (kc < cs + NA_KW)
    slabs = []
    for dr in range(n_dr):
        acc = jnp.zeros((GRID_W, LANES), F32)
        for d in range(n_dc):
            acc = jnp.where(diff == d, rpb_ref[(h * n_dr + dr) * n_dc + d], acc)
        slabs.append(jnp.where(col_ok, acc * LOG2E, NEG))
    left = lane < GRID_W
    for var in range(kh):
        for jp in range(kh * GRID_W // LANES):
            dr0 = 2 * jp - var + NA_KH_MAX - 1
            o_ref[0, var, :, jp * LANES:(jp + 1) * LANES] = jnp.where(left, slabs[dr0], slabs[dr0 + 1])


def _nbr_bias_table(rpb, seq):
    rows = seq // GRID_W
    kh = min(NA_KH_MAX, rows)
    heads = rpb.shape[0]
    assert kh == NA_KH_MAX and 2 * GRID_W == LANES
    return pl.pallas_call(
        functools.partial(_bias_table_body, kh=kh),
        grid=(heads,),
        in_specs=[pl.BlockSpec(memory_space=pltpu.SMEM)],
        out_specs=pl.BlockSpec((1, kh, GRID_W, kh * GRID_W), lambda h: (h, 0, 0, 0)),
        out_shape=jax.ShapeDtypeStruct((heads, kh, GRID_W, kh * GRID_W), F32),
        compiler_params=_params(("arbitrary",), 16),
        name="nbr_bias_table",
    )(rpb.astype(F32).reshape(-1))


def _merge_body(oa_ref, ob_ref, wa_ref, wb_ref, g0_ref, g1_ref, o_ref):
    wa, wb = wa_ref[...].astype(BF16), wb_ref[...].astype(BF16)
    for rows in _row_parts(oa_ref.shape[0]):
        ya = jnp.dot(oa_ref[rows, :], wa, preferred_element_type=F32)
        yb = jnp.dot(ob_ref[rows, :], wb, preferred_element_type=F32)
        o_ref[rows, :] = (g0_ref[rows, :].astype(F32) * ya
                          + g1_ref[rows, :].astype(F32) * yb).astype(o_ref.dtype)


def branch_merge(oa, ob, wa, wb, gates, bm=2048, bn=256):
    m, k = oa.shape
    n = wa.shape[1]
    g1_off = n // bn
    return pl.pallas_call(
        _merge_body,
        grid=(m // bm, n // bn),
        in_specs=[pl.BlockSpec((bm, k), lambda i, j: (i, 0)),
                  pl.BlockSpec((bm, k), lambda i, j: (i, 0)),
                  pl.BlockSpec((k, bn), lambda i, j: (0, j)),
                  pl.BlockSpec((k, bn), lambda i, j: (0, j)),
                  pl.BlockSpec((bm, bn), lambda i, j: (i, j)),
                  pl.BlockSpec((bm, bn), lambda i, j: (i, j + g1_off))],
        out_specs=pl.BlockSpec((bm, bn), lambda i, j: (i, j)),
        out_shape=jax.ShapeDtypeStruct((m, n), BF16),
        compiler_params=_params(("arbitrary", "arbitrary"), 56),
        name="branch_merge",
    )(oa, ob, wa, wb, gates, gates)


def _cast_body(x_ref, o_ref):
    o_ref[...] = x_ref[...].astype(o_ref.dtype)


def cast_bf16(w):
    r, c = w.shape
    return pl.pallas_call(
        _cast_body,
        grid=(1,),
        in_specs=[pl.BlockSpec((r, c), lambda i: (0, 0))],
        out_specs=pl.BlockSpec((r, c), lambda i: (0, 0)),
        out_shape=jax.ShapeDtypeStruct((r, c), BF16),
        compiler_params=_params(("arbitrary",), 40),
        name="cast_bf16",
    )(w)


def _router_probs(hn, wr_ref):
    hi = hn.astype(BF16)
    lo = (hn - hi.astype(F32)).astype(BF16)
    l_hi = jnp.dot(hi, wr_ref[...], preferred_element_type=F32)
    l_lo = jnp.dot(lo, wr_ref[...], preferred_element_type=F32)
    logits = l_hi + pltpu.roll(l_hi, LANES - N_EXPERTS, 1) + l_lo
    lane = lax.broadcasted_iota(jnp.int32, logits.shape, 1)
    logits = jnp.where(lane < N_EXPERTS, logits, NEG)
    return _softmax_rows(logits)


def _xblock_body(x_ref, gc_ref, wq_ref, k_ref, v_ref, wo_ref, gf_ref, wr_ref,
                 x2_ref, h3_ref, aff_ref):
    scale = HEAD_DIM ** -0.5
    half = x_ref.shape[1] // 2
    parts = [slice(p * XB_ROWS, (p + 1) * XB_ROWS) for p in range(x_ref.shape[0] // XB_ROWS)]
    h2s = [_rmsnorm_rows(x_ref[r, :], gc_ref[...]).astype(BF16) for r in parts]
    qs = [jnp.dot(h2, wq_ref[...], preferred_element_type=F32).astype(BF16) for h2 in h2s]
    os = []
    for q in qs:
        heads = []
        for h in range(X_HEADS):
            cols = slice(h * HEAD_DIM, (h + 1) * HEAD_DIM)
            s = lax.dot_general(q[:, cols], k_ref[:, cols], _NT, preferred_element_type=F32) * scale
            p = _softmax_rows(s)
            heads.append(jnp.dot(p.astype(BF16), v_ref[:, cols],
                                 preferred_element_type=F32).astype(BF16))
        os.append(jnp.concatenate(heads, axis=1))
    for r, o in zip(parts, os):
        x2_ref[r, :] = x_ref[r, :] + jnp.dot(o, wo_ref[...], preferred_element_type=F32)
    for r in parts:
        hn = _rmsnorm_rows(x2_ref[r, :], gf_ref[...])
        packed = pltpu.pack_elementwise([hn[:, :half], hn[:, half:]], packed_dtype=BF16)
        h3_ref[r, :] = pltpu.bitcast(packed, jnp.uint32)
        aff = _router_probs(hn, wr_ref)
        aff_ref[0, :, r] = aff.T[:N_EXPERTS, :]


XB_ROWS = 256


def cross_attention_block(x1, g_cross, wq, kx, vx, wo, g_ffn, w_router, batch, seq, mem_len, bm=512):
    m, d = x1.shape
    nt = seq // bm
    w_hi = w_router.astype(BF16)
    w_lo = (w_router - w_hi.astype(F32)).astype(BF16)
    wr = jnp.concatenate([w_hi, w_lo, jnp.zeros((d, LANES - 2 * N_EXPERTS), BF16)], axis=1)
    const = lambda i: (0, 0)
    once = pl.Buffered(1)
    return pl.pallas_call(
        _xblock_body,
        grid=(m // bm,),
        in_specs=[pl.BlockSpec((bm, d), lambda i: (i, 0)),
                  pl.BlockSpec((1, d), const),
                  pl.BlockSpec((d, X_W), const, pipeline_mode=once),
                  pl.BlockSpec((mem_len, X_W), lambda i: (i // nt, 0)),
                  pl.BlockSpec((mem_len, X_W), lambda i: (i // nt, 0)),
                  pl.BlockSpec((X_W, d), const, pipeline_mode=once),
                  pl.BlockSpec((1, d), const),
                  pl.BlockSpec((d, LANES), const, pipeline_mode=once)],
        out_specs=[pl.BlockSpec((bm, d), lambda i: (i, 0)),
                   pl.BlockSpec((bm, d // 2), lambda i: (i, 0)),
                   pl.BlockSpec((1, N_EXPERTS, bm), lambda i: (i // nt, 0, i % nt))],
        out_shape=[jax.ShapeDtypeStruct((m, d), F32),
                   jax.ShapeDtypeStruct((m, d // 2), jnp.uint32),
                   jax.ShapeDtypeStruct((batch, N_EXPERTS, seq), F32)],
        compiler_params=_params(("arbitrary",), 58),
        name="cross_attention_block",
    )(x1, g_cross.reshape(1, d), cast_bf16(wq), kx, vx, cast_bf16(wo), g_ffn.reshape(1, d), wr)


CUM_CHUNK = 256


def _excl_cumsum_lanes(x01, tri):
    n = x01.shape[1]
    carry = jnp.zeros((x01.shape[0], 1), F32)
    out = []
    for c in range(n // CUM_CHUNK):
        xc = x01[:, c * CUM_CHUNK:(c + 1) * CUM_CHUNK]
        out.append(jnp.dot(xc.astype(BF16), tri, preferred_element_type=F32) + carry)
        carry = carry + xc.sum(axis=1, keepdims=True)
    return jnp.concatenate(out, axis=1)


COMBINE_TS = 256
COMBINE_W = 64


def _topk_body(aff_ref, slot_ref, first_ref, *, cap):
    a = aff_ref[0]
    n_exp, s = a.shape
    capf = jnp.float32(cap)
    lane_s = lax.broadcasted_iota(jnp.int32, (n_exp, s), 1)

    def count(mask):
        return jnp.where(mask, 1.0, 0.0).sum(axis=1, keepdims=True)

    bits = pltpu.bitcast(a, jnp.int32)

    def search(i, t):
        cand = t | jnp.left_shift(jnp.int32(1), 30 - i)
        return jnp.where(count(bits >= cand) >= capf, cand, t)

    t = lax.fori_loop(0, 31, search, jnp.zeros((n_exp, 1), jnp.int32))
    at = jnp.where(bits == t, lane_s, s).astype(F32).min(axis=1, keepdims=True).astype(jnp.int32)
    pivot = jnp.where(lane_s == at, a, 0.0).sum(axis=1, keepdims=True)

    def stats(p):
        return p, count(a > p), count(a >= p)

    def wrong(state):
        _, n_gt, n_ge = state
        return jnp.where(jnp.logical_or(n_gt >= capf, n_ge < capf), 1.0, 0.0).sum() > 0.0

    def step(state):
        p, n_gt, n_ge = state
        up = jnp.where(a > p, a, jnp.inf).min(axis=1, keepdims=True)
        down = jnp.where(a < p, a, -jnp.inf).max(axis=1, keepdims=True)
        return stats(jnp.where(n_gt >= capf, up, jnp.where(n_ge < capf, down, p)))

    pivot, n_gt, _ = lax.while_loop(wrong, step, stats(pivot))
    ri = lax.broadcasted_iota(jnp.int32, (CUM_CHUNK, CUM_CHUNK), 0)
    cj = lax.broadcasted_iota(jnp.int32, (CUM_CHUNK, CUM_CHUNK), 1)
    tri = jnp.where(ri < cj, 1.0, 0.0).astype(BF16)
    eq = jnp.where(a == pivot, 1.0, 0.0)
    sel = jnp.where(a > pivot, 1.0, 0.0) + jnp.where(_excl_cumsum_lanes(eq, tri) < capf - n_gt, eq, 0.0)
    pos = _excl_cumsum_lanes(sel, tri)
    slot_ref[0] = jnp.where(sel > 0.5, pos, -1.0).astype(jnp.int32)
    lane_k = lax.broadcasted_iota(jnp.int32, (n_exp, LANES), 1)
    first = jnp.zeros((n_exp, LANES), F32)
    for k in range(s // COMBINE_TS):
        at_k = jnp.where(lane_s == k * COMBINE_TS, pos, 0.0).sum(axis=1, keepdims=True)
        first = jnp.where(lane_k == k, at_k, first)
    first_ref[0] = first.astype(jnp.int32)


def expert_topk(aff_t, cap):
    b, e, s = aff_t.shape
    return pl.pallas_call(
        functools.partial(_topk_body, cap=cap),
        grid=(b,),
        in_specs=[pl.BlockSpec((1, e, s), lambda i: (i, 0, 0))],
        out_specs=[pl.BlockSpec((1, e, s), lambda i: (i, 0, 0)),
                   pl.BlockSpec((1, e, LANES), lambda i: (i, 0, 0))],
        out_shape=[jax.ShapeDtypeStruct((b, e, s), jnp.int32),
                   jax.ShapeDtypeStruct((b, e, LANES), jnp.int32)],
        compiler_params=_params(("arbitrary",), 32),
        name="expert_topk",
    )(aff_t)


def _slot_index_body(slot_ref, aff_ref, idx_ref, val_ref, *, cap):
    b = pl.program_id(0)
    n_exp, s = slot_ref.shape[1], slot_ref.shape[2]
    ci = lax.broadcasted_iota(jnp.int32, (cap, s), 0)
    tok = lax.broadcasted_iota(jnp.int32, (cap, s), 1).astype(F32)
    base = (b * s).astype(F32)
    for e in range(n_exp):
        hit = slot_ref[0, e:e + 1, :] == ci
        idx = jnp.where(hit, tok, 0.0).sum(axis=1, keepdims=True) + base
        idx_ref[e] = idx.astype(jnp.int32)
        val_ref[e] = jnp.where(hit, aff_ref[0, e:e + 1, :], 0.0).sum(axis=1, keepdims=True)


def slot_index(slot, aff_t, cap):
    b, e, s = slot.shape
    return pl.pallas_call(
        functools.partial(_slot_index_body, cap=cap),
        grid=(b,),
        in_specs=[pl.BlockSpec((1, e, s), lambda bi: (bi, 0, 0)),
                  pl.BlockSpec((1, e, s), lambda bi: (bi, 0, 0))],
        out_specs=[pl.BlockSpec((e, cap, 1), lambda bi: (0, bi, 0)),
                   pl.BlockSpec((e, cap, 1), lambda bi: (0, bi, 0))],
        out_shape=[jax.ShapeDtypeStruct((e, b * cap, 1), jnp.int32),
                   jax.ShapeDtypeStruct((e, b * cap, 1), F32)],
        compiler_params=_params(("arbitrary",), 32),
        name="slot_index",
    )(slot, aff_t)


UNPACK_ROWS = 128


def _row_copy(idx_ref, hp_ref, gbuf, sem, expert, rows, row):
    tok = idx_ref[expert * rows + row]
    return pltpu.make_async_copy(hp_ref.at[pl.ds(tok, 1)], gbuf.at[pl.ds(row, 1)], sem)


def _expert_up_body(idx_ref, hp_ref, wg_ref, wu_ref, o_ref, gbuf, xbf, sem, *, rows, per_step):
    e, f = pl.program_id(0), pl.program_id(1)
    n_e, n_f = pl.num_programs(0), pl.num_programs(1)
    half = gbuf.shape[1]

    def wait_all_rows():
        pltpu.make_async_copy(hp_ref.at[pl.ds(0, rows)], gbuf, sem).wait()

    @pl.when(jnp.logical_and(e == 0, f == 0))
    def _():
        def body(r, carry):
            _row_copy(idx_ref, hp_ref, gbuf, sem, 0, rows, r).start()
            return carry
        lax.fori_loop(0, rows, body, 0)

    @pl.when(f == 0)
    def _():
        wait_all_rows()

        def unpack(k, carry):
            r = pl.multiple_of(k * UNPACK_ROWS, UNPACK_ROWS)
            w = gbuf[pl.ds(r, UNPACK_ROWS), :]
            lo = pltpu.unpack_elementwise(w, index=0, packed_dtype=BF16, unpacked_dtype=F32)
            hi = pltpu.unpack_elementwise(w, index=1, packed_dtype=BF16, unpacked_dtype=F32)
            xbf[pl.ds(r, UNPACK_ROWS), :half] = lo.astype(BF16)
            xbf[pl.ds(r, UNPACK_ROWS), half:] = hi.astype(BF16)
            return carry
        lax.fori_loop(0, rows // UNPACK_ROWS, unpack, 0)

    nxt = jnp.minimum(e + 1, n_e - 1)
    for r in range(per_step):
        _row_copy(idx_ref, hp_ref, gbuf, sem, nxt, rows, f * per_step + r).start()

    wg, wu = wg_ref[0].astype(BF16), wu_ref[0].astype(BF16)
    for part in _row_parts(rows):
        a = jnp.dot(xbf[part, :], wg, preferred_element_type=F32)
        u = jnp.dot(xbf[part, :], wu, preferred_element_type=F32)
        o_ref[0, part, :] = (a * _sigmoid(a) * u).astype(o_ref.dtype)

    @pl.when(jnp.logical_and(e == n_e - 1, f == n_f - 1))
    def _():
        wait_all_rows()


def expert_up(idx, hp, w_gate, w_up, rows, tf=256):
    n_e, d, f = w_gate.shape
    assert hp.shape[1] * 2 == d and rows % (f // tf) == 0
    grid_spec = pltpu.PrefetchScalarGridSpec(
        num_scalar_prefetch=1,
        grid=(n_e, f // tf),
        in_specs=[pl.BlockSpec(memory_space=pl.ANY),
                  pl.BlockSpec((1, d, tf), lambda ei, fi, idx_ref: (ei, 0, fi)),
                  pl.BlockSpec((1, d, tf), lambda ei, fi, idx_ref: (ei, 0, fi))],
        out_specs=pl.BlockSpec((1, rows, tf), lambda ei, fi, idx_ref: (ei, 0, fi)),
        scratch_shapes=[pltpu.VMEM((rows, d // 2), jnp.uint32),
                        pltpu.VMEM((rows, d), BF16),
                        pltpu.SemaphoreType.DMA(())],
    )
    return pl.pallas_call(
        functools.partial(_expert_up_body, rows=rows, per_step=rows // (f // tf)),
        grid_spec=grid_spec,
        out_shape=jax.ShapeDtypeStruct((n_e, rows, f), BF16),
        compiler_params=_params(("arbitrary", "arbitrary"), 56),
        name="expert_up",
    )(idx, hp, w_gate, w_up)


def _expert_down_body(h_ref, wd_ref, val_ref, o_ref):
    wd = wd_ref[0].astype(BF16)
    for part in _row_parts(h_ref.shape[1]):
        y = jnp.dot(h_ref[0, part, :], wd, preferred_element_type=F32)
        o_ref[0, part, :] = (y * val_ref[0, part, :]).astype(o_ref.dtype)


def expert_down(hmid, w_down, valc, tn=1024):
    e, rows, f = hmid.shape
    d = w_down.shape[-1]
    return pl.pallas_call(
        _expert_down_body,
        grid=(e, d // tn),
        in_specs=[pl.BlockSpec((1, rows, f), lambda ei, ni: (ei, 0, 0)),
                  pl.BlockSpec((1, f, tn), lambda ei, ni: (ei, 0, ni)),
                  pl.BlockSpec((1, rows, 1), lambda ei, ni: (ei, 0, 0))],
        out_specs=pl.BlockSpec((1, rows, tn), lambda ei, ni: (ei, 0, ni)),
        out_shape=jax.ShapeDtypeStruct((e, rows, d), BF16),
        compiler_params=_params(("arbitrary", "arbitrary"), 48),
        name="expert_down",
    )(hmid, w_down, valc)


COMBINE_TN = 512
ROW_ALIGN_BF16 = 16


def _combine_windows(first, cap, n_rows):
    b, e, nt = first.shape
    row0 = jnp.arange(b, dtype=jnp.int32)[:, None, None] * cap
    lo = first + row0
    hi = jnp.concatenate([first[:, :, 1:], jnp.full((b, e, 1), cap, jnp.int32)], axis=2) + row0
    start = jnp.minimum(lo // ROW_ALIGN_BF16 * ROW_ALIGN_BF16, n_rows - COMBINE_W)
    fast = jnp.all(hi - start <= COMBINE_W, axis=1)
    return start.transpose(0, 2, 1).reshape(-1), fast.astype(jnp.int32).reshape(-1)


def _combine_body(win_ref, fast_ref, slot_ref, y_hbm, x_ref, g_ref, o_ref,
                  ywin, oht, ybuf, ohs, wsem, ssem, *, cap, n_tiles):
    i = pl.program_id(0)
    n_exp = slot_ref.shape[1]
    ts, d = o_ref.shape
    w = COMBINE_W
    row0 = (i // n_tiles) * cap
    buf = i % 2

    def window_copy(step, e, b):
        start = pl.multiple_of(win_ref[step * n_exp + e], ROW_ALIGN_BF16)
        return pltpu.make_async_copy(y_hbm.at[e, pl.ds(start, w)], ywin.at[b, pl.ds(e * w, w)],
                                     wsem.at[b])

    @pl.when(i == 0)
    def _():
        for e in range(n_exp):
            window_copy(0, e, 0).start()

    @pl.when(i + 1 < pl.num_programs(0))
    def _():
        for e in range(n_exp):
            window_copy(i + 1, e, 1 - buf).start()

    pltpu.make_async_copy(y_hbm.at[0, pl.ds(0, n_exp * w)], ywin.at[buf], wsem.at[buf]).wait()

    @pl.when(fast_ref[i] == 1)
    def _():
        row = lax.broadcasted_iota(jnp.int32, (2 * w, ts), 0)
        upper = row >= w
        j = row % w
        for p in range(n_exp // 2):
            rel_a = slot_ref[0, 2 * p:2 * p + 1, :] + (row0 - win_ref[i * n_exp + 2 * p])
            rel_b = slot_ref[0, 2 * p + 1:2 * p + 2, :] + (row0 - win_ref[i * n_exp + 2 * p + 1])
            hit = jnp.where(upper, rel_b, rel_a) == j
            oht[:, p * 2 * w:(p + 1) * 2 * w] = jnp.where(hit, 1.0, 0.0).T.astype(BF16)
        for c in range(d // COMBINE_TN):
            cols = slice(c * COMBINE_TN, (c + 1) * COMBINE_TN)
            o_ref[:, cols] = x_ref[:, cols] + jnp.dot(oht[...], ywin[buf, :, cols],
                                                      preferred_element_type=F32)

    @pl.when(fast_ref[i] == 0)
    def _():
        o_ref[...] = x_ref[...]
        ci = lax.broadcasted_iota(jnp.int32, (cap, ts), 0)

        def body(e, carry):
            cp = pltpu.make_async_copy(y_hbm.at[e, pl.ds(pl.multiple_of(row0, cap), cap)], ybuf, ssem)
            cp.start()
            cp.wait()
            srow = slot_ref[0, pl.ds(e, 1), :]
            ohs[...] = jnp.where(srow == ci, 1.0, 0.0).T.astype(BF16)
            for c in range(d // COMBINE_TN):
                cols = slice(c * COMBINE_TN, (c + 1) * COMBINE_TN)
                o_ref[:, cols] += jnp.dot(ohs[...], ybuf[:, cols], preferred_element_type=F32)
            return carry
        lax.fori_loop(0, n_exp, body, 0)

    o_ref[...] = _rmsnorm_rows(o_ref[...], g_ref[...])


def expert_combine(slot, first, y, x2d, g, cap):
    b, e, s = slot.shape
    d = x2d.shape[-1]
    ts, w = COMBINE_TS, COMBINE_W
    nt = s // ts
    assert e % 2 == 0 and y.shape[1] >= e * w and cap % ROW_ALIGN_BF16 == 0
    win, fast = _combine_windows(first[:, :, :nt], cap, y.shape[1])
    grid_spec = pltpu.PrefetchScalarGridSpec(
        num_scalar_prefetch=2,
        grid=(b * nt,),
        in_specs=[pl.BlockSpec((1, e, ts), lambda i, win_ref, fast_ref: (i // nt, 0, i % nt)),
                  pl.BlockSpec(memory_space=pl.ANY),
                  pl.BlockSpec((ts, d), lambda i, win_ref, fast_ref: (i, 0)),
                  pl.BlockSpec((1, d), lambda i, win_ref, fast_ref: (0, 0))],
        out_specs=pl.BlockSpec((ts, d), lambda i, win_ref, fast_ref: (i, 0)),
        scratch_shapes=[pltpu.VMEM((2, e * w, d), BF16),
                        pltpu.VMEM((ts, e * w), BF16),
                        pltpu.VMEM((cap, d), BF16),
                        pltpu.VMEM((ts, cap), BF16),
                        pltpu.SemaphoreType.DMA((2,)),
                        pltpu.SemaphoreType.DMA(())],
    )
    return pl.pallas_call(
        functools.partial(_combine_body, cap=cap, n_tiles=nt),
        grid_spec=grid_spec,
        out_shape=jax.ShapeDtypeStruct(x2d.shape, F32),
        compiler_params=_params(("arbitrary",), 48),
        name="expert_combine",
    )(win, fast, slot, y, x2d, g.reshape(1, d))


def _rotary_tables(seq):
    half = ROT_DIM // 2
    inv = ROPE_THETA ** (-jnp.arange(half, dtype=F32) * 2.0 / ROT_DIM)
    ang = jnp.arange(seq).astype(F32)[:, None] * inv[None, :]
    cos, sin = jnp.cos(ang), jnp.sin(ang)
    ones = jnp.ones((seq, HEAD_DIM - ROT_DIM), F32)
    zeros = jnp.zeros((seq, HEAD_DIM - ROT_DIM), F32)
    zh = jnp.zeros((seq, half), F32)
    c = jnp.concatenate([cos, cos, ones], axis=1)
    s1 = jnp.concatenate([-sin, zh, zeros], axis=1)
    s2 = jnp.concatenate([zh, sin, zeros], axis=1)
    return c, s1, s2


def kernel(x, mem, norm_mix, w_in, b_gate, sink, rpb, w_branch_a, w_branch_b, w_out,
           norm_cross, norm_mem, wq_x, wk_x, wv_x, wo_x, norm_ffn, w_router,
           w_gate, w_up, w_down, norm_final):
    batch, seq, d = x.shape
    mem_len = mem.shape[1]
    m = batch * seq
    assert norm_mix.shape[0] == 1, "final RMSNorm is fused into the single layer's last kernel"
    cap = EC_CAPACITY * seq // N_EXPERTS
    bm, bn = 2048, 256
    sb = seq // bm
    x0 = x.reshape(m, d)

    h = rmsnorm(x0, norm_mix[0], BF16)
    rot = _rotary_tables(seq)
    rot_specs = [pl.BlockSpec((bm, HEAD_DIM), lambda i, j: (i % sb, 0))] * 3
    qk = matmul_rows(h, w_in[0], col_off=0, n_cols=QA_W + KVA_W, bm=bm, bn=bn, out_dtype=BF16,
                     epilogue=_ep_rotary, extras=rot, extra_specs=rot_specs, row_split=2 * ROW_SPLIT,
                     name="in_proj_rotary")
    vqkv = matmul_rows(h, w_in[0], col_off=QA_W + KVA_W, n_cols=KVA_W + 3 * QB_W, bm=bm, bn=bn,
                       out_dtype=BF16, name="in_proj_plain")
    g_off = QA_W + 2 * KVA_W + 3 * QB_W
    gates = matmul_rows(h, w_in[0], col_off=g_off, n_cols=2 * d, bm=bm, bn=bn, out_dtype=BF16,
                        epilogue=_ep_sigmoid, extras=(b_gate[0].reshape(1, 2 * d),),
                        extra_specs=[pl.BlockSpec((1, bn), lambda i, j: (0, j))], name="in_proj_gates")
    oa = window_attention(qk, vqkv, sink[0], batch, seq)
    ob = neighbourhood_attention(vqkv, _nbr_bias_table(rpb[0], seq), batch, seq)
    merged = branch_merge(oa, ob, w_branch_a[0], w_branch_b[0], gates, bm=bm, bn=bn)
    res_spec = [pl.BlockSpec((bm, bn), lambda i, j: (i, j))]
    x1 = matmul_rows(merged, w_out[0], col_off=0, n_cols=d, bm=bm, bn=bn, out_dtype=F32,
                     epilogue=_ep_residual, extras=(x0,), extra_specs=res_spec, name="out_proj")

    mem_rows = batch * mem_len
    mn = rmsnorm(mem.reshape(mem_rows, d), norm_mem[0], BF16)
    kx = matmul_rows(mn, wk_x[0], col_off=0, n_cols=X_W, bm=mem_rows, bn=bn, out_dtype=BF16, name="xattn_k")
    vx = matmul_rows(mn, wv_x[0], col_off=0, n_cols=X_W, bm=mem_rows, bn=bn, out_dtype=BF16, name="xattn_v")
    x2, h3p, aff_t = cross_attention_block(
        x1, norm_cross[0], wq_x[0], kx, vx, wo_x[0], norm_ffn[0], w_router[0], batch, seq, mem_len)

    slot, first = expert_topk(aff_t, cap)
    idx, valc = slot_index(slot, aff_t, cap)
    hmid = expert_up(idx.reshape(-1), h3p, w_gate[0], w_up[0], batch * cap)
    y = expert_down(hmid, w_down[0], valc)
    out = expert_combine(slot, first, y, x2, norm_final, cap)
    return out.reshape(batch, seq, d)
```

```python
import functools

import jax
import jax.numpy as jnp
from jax import lax
from jax.experimental import pallas as pl
from jax.experimental.pallas import tpu as pltpu

F32 = jnp.float32
BF16 = jnp.bfloat16

HEAD_DIM = 128
A_HEADS = 16
A_KV_HEADS = 4
A_GROUP = A_HEADS // A_KV_HEADS
WINDOW = 128
A_BLOCK = 128
ROT_DIM = HEAD_DIM // 4
ROPE_THETA = 500000.0
B_HEADS = 16
GRID_W = 64
NA_KH_MAX = 8
NA_KW = 16
X_HEADS = 4
N_EXPERTS = 16
EC_CAPACITY = 2
EPS = 1e-6
NEG = -1e30
LOG2E = 1.4426950408889634
LANES = 128
MIB = 1024 * 1024

QA_W = A_HEADS * HEAD_DIM
KVA_W = A_KV_HEADS * HEAD_DIM
QB_W = B_HEADS * HEAD_DIM
X_W = X_HEADS * HEAD_DIM

_NT = (((1,), (1,)), ((), ()))
_TN = (((0,), (0,)), ((), ()))


def _params(semantics, vmem_mib):
    return pltpu.CompilerParams(dimension_semantics=semantics,
                                vmem_limit_bytes=vmem_mib * MIB)


def _rmsnorm_rows(x, g):
    ms = jnp.mean(x * x, axis=-1, keepdims=True)
    return x * lax.rsqrt(ms + EPS) * g


def _rmsnorm_body(x_ref, g_ref, o_ref):
    o_ref[...] = _rmsnorm_rows(x_ref[...], g_ref[...]).astype(o_ref.dtype)


def rmsnorm(x2d, g, out_dtype, bm=512):
    m, d = x2d.shape
    return pl.pallas_call(
        _rmsnorm_body,
        grid=(m // bm,),
        in_specs=[pl.BlockSpec((bm, d), lambda i: (i, 0)),
                  pl.BlockSpec((1, d), lambda i: (0, 0))],
        out_specs=pl.BlockSpec((bm, d), lambda i: (i, 0)),
        out_shape=jax.ShapeDtypeStruct((m, d), out_dtype),
        compiler_params=_params(("arbitrary",), 40),
        name="rmsnorm",
    )(x2d, g.reshape(1, d))


def _sigmoid(x):
    return 0.5 * jnp.tanh(0.5 * x) + 0.5


ROW_SPLIT = 4


def _row_parts(n_rows, split=ROW_SPLIT):
    step = n_rows // split
    return [slice(p * step, (p + 1) * step) for p in range(split)]


def _ep_store(acc, rows, o_ref):
    o_ref[rows, :] = acc.astype(o_ref.dtype)


def _ep_residual(acc, rows, o_ref, r_ref):
    o_ref[rows, :] = (r_ref[rows, :] + acc).astype(o_ref.dtype)


def _ep_sigmoid(acc, rows, o_ref, b_ref):
    o_ref[rows, :] = _sigmoid((acc + b_ref[...]).astype(o_ref.dtype))


def _ep_rotary(acc, rows, o_ref, c_ref, s1_ref, s2_ref):
    c, s1, s2 = c_ref[rows, :], s1_ref[rows, :], s2_ref[rows, :]
    half = ROT_DIM // 2
    for h in range(acc.shape[1] // HEAD_DIM):
        a = acc[:, h * HEAD_DIM:(h + 1) * HEAD_DIM]
        r = a * c + pltpu.roll(a, HEAD_DIM - half, 1) * s1 + pltpu.roll(a, half, 1) * s2
        o_ref[rows, h * HEAD_DIM:(h + 1) * HEAD_DIM] = r.astype(o_ref.dtype)


def _mm_rows_body(*refs, n_extra, epilogue, row_split):
    a_ref, w_ref = refs[0], refs[1]
    extra = refs[2:2 + n_extra]
    o_ref = refs[2 + n_extra]
    wb = w_ref[...].astype(BF16)
    for rows in _row_parts(a_ref.shape[0], row_split):
        acc = jnp.dot(a_ref[rows, :], wb, preferred_element_type=F32)
        epilogue(acc, rows, o_ref, *extra)


def matmul_rows(a, w, *, col_off, n_cols, bm, bn, out_dtype, epilogue=_ep_store,
                extras=(), extra_specs=(), row_split=ROW_SPLIT, vmem_mib=56, name="matmul_rows"):
    m, k = a.shape
    off = col_off // bn
    assert col_off % bn == 0 and n_cols % bn == 0 and m % bm == 0
    body = functools.partial(_mm_rows_body, n_extra=len(extras), epilogue=epilogue, row_split=row_split)
    return pl.pallas_call(
        body,
        grid=(m // bm, n_cols // bn),
        in_specs=[pl.BlockSpec((bm, k), lambda i, j: (i, 0)),
                  pl.BlockSpec((k, bn), lambda i, j: (0, j + off))] + list(extra_specs),
        out_specs=pl.BlockSpec((bm, bn), lambda i, j: (i, j)),
        out_shape=jax.ShapeDtypeStruct((m, n_cols), out_dtype),
        compiler_params=_params(("arbitrary", "arbitrary"), vmem_mib),
        name=name,
    )(a, w, *extras)


def _softmax_rows(s):
    e = jnp.exp(s - s.max(axis=1, keepdims=True))
    return e * (1.0 / e.sum(axis=1, keepdims=True))


WIN_UNROLL = 8


def _win_body(sink_ref, q_ref, k_ref, v_ref, o_ref, *, seq):
    kv = pl.program_id(1)
    nb = seq // A_BLOCK
    scale = HEAD_DIM ** -0.5 * LOG2E
    rows = A_GROUP * A_BLOCK
    qi = lax.broadcasted_iota(jnp.int32, (rows, A_BLOCK), 0) % A_BLOCK
    ci = lax.broadcasted_iota(jnp.int32, (rows, A_BLOCK), 1)
    sink_b = jnp.concatenate(
        [jnp.full((A_BLOCK, HEAD_DIM), sink_ref[kv * A_GROUP + g] * LOG2E, F32)
         for g in range(A_GROUP)], axis=0)

    def scores(n):
        r0 = pl.multiple_of(n * A_BLOCK, A_BLOCK)
        rp = pl.multiple_of(jnp.maximum(n - 1, 0) * A_BLOCK, A_BLOCK)
        rn = pl.multiple_of(jnp.minimum(n + 1, nb - 1) * A_BLOCK, A_BLOCK)
        off_p = jnp.where(n > 0, 0, 2 * A_BLOCK)
        off_n = jnp.where(n < nb - 1, 0, 2 * A_BLOCK)
        q = jnp.concatenate(
            [q_ref[pl.ds(r0, A_BLOCK), g * HEAD_DIM:(g + 1) * HEAD_DIM] for g in range(A_GROUP)],
            axis=0)
        sp = lax.dot_general(q, k_ref[pl.ds(rp, A_BLOCK), :], _NT, preferred_element_type=F32) * scale
        sc = lax.dot_general(q, k_ref[pl.ds(r0, A_BLOCK), :], _NT, preferred_element_type=F32) * scale
        sn = lax.dot_general(q, k_ref[pl.ds(rn, A_BLOCK), :], _NT, preferred_element_type=F32) * scale
        sp = jnp.where(ci >= qi + off_p, sp, NEG)
        sn = jnp.where(ci <= qi - off_n, sn, NEG)
        return (rp, r0, rn), (sp, sc, sn)

    def exps(parts):
        m = jnp.maximum(jnp.maximum(parts[0], parts[1]), parts[2]).max(axis=1, keepdims=True)
        m = jnp.maximum(jnp.broadcast_to(m, sink_b.shape), sink_b)
        es = [jnp.exp2(p - m) for p in parts]
        den = (es[0] + es[1] + es[2]).sum(axis=1, keepdims=True)
        den = jnp.broadcast_to(den, sink_b.shape) + jnp.exp2(sink_b - m)
        return [e.astype(BF16) for e in es], 1.0 / den

    def body(it, carry):
        blocks = [scores(it * WIN_UNROLL + u) for u in range(WIN_UNROLL)]
        probs = [exps(parts) for _, parts in blocks]
        for (rows_kv, _), (es, inv) in zip(blocks, probs):
            o = jnp.dot(es[0], v_ref[pl.ds(rows_kv[0], A_BLOCK), :], preferred_element_type=F32)
            o = o + jnp.dot(es[1], v_ref[pl.ds(rows_kv[1], A_BLOCK), :], preferred_element_type=F32)
            o = o + jnp.dot(es[2], v_ref[pl.ds(rows_kv[2], A_BLOCK), :], preferred_element_type=F32)
            o = o * inv
            for g in range(A_GROUP):
                o_ref[pl.ds(rows_kv[1], A_BLOCK), g * HEAD_DIM:(g + 1) * HEAD_DIM] = (
                    o[g * A_BLOCK:(g + 1) * A_BLOCK].astype(o_ref.dtype))
        return carry

    lax.fori_loop(0, nb // WIN_UNROLL, body, 0)


def window_attention(qk, vqkv, sink, batch, seq):
    gw = A_GROUP * HEAD_DIM
    k_blk0 = QA_W // HEAD_DIM
    return pl.pallas_call(
        functools.partial(_win_body, seq=seq),
        grid=(batch, A_KV_HEADS),
        in_specs=[pl.BlockSpec(memory_space=pltpu.SMEM),
                  pl.BlockSpec((seq, gw), lambda b, h: (b, h)),
                  pl.BlockSpec((seq, HEAD_DIM), lambda b, h: (b, k_blk0 + h)),
                  pl.BlockSpec((seq, HEAD_DIM), lambda b, h: (b, h))],
        out_specs=pl.BlockSpec((seq, gw), lambda b, h: (b, h)),
        out_shape=jax.ShapeDtypeStruct((batch * seq, QA_W), BF16),
        compiler_params=_params(("arbitrary", "arbitrary"), 32),
        name="window_attention",
    )(sink, qk, qk, vqkv)


NBR_HG = 4
NBR_ROWS = 8


def _nbr_body(q_ref, k_ref, v_ref, bias_ref, o_ref, *, seq):
    rows = seq // GRID_W
    kh = min(NA_KH_MAX, rows)
    strip = kh * GRID_W
    scale = HEAD_DIM ** -0.5 * LOG2E

    def body(it, carry):
        units = []
        for rr in range(NBR_ROWS):
            r = it * NBR_ROWS + rr
            rs = jnp.clip(r - kh // 2, 0, rows - kh)
            q0 = pl.multiple_of(r * GRID_W, GRID_W)
            k0 = pl.multiple_of(rs * GRID_W, GRID_W)
            for h in range(NBR_HG):
                units.append((q0, k0, r - rs, h, slice(h * HEAD_DIM, (h + 1) * HEAD_DIM)))
        ss = [lax.dot_general(q_ref[pl.ds(q0, GRID_W), cols], k_ref[pl.ds(k0, strip), cols], _NT,
                              preferred_element_type=F32) * scale + bias_ref[h, var]
              for q0, k0, var, h, cols in units]
        ps = []
        for s in ss:
            e = jnp.exp2(s - s.max(axis=1, keepdims=True))
            ps.append((e.astype(BF16), 1.0 / e.sum(axis=1, keepdims=True)))
        for (q0, k0, var, h, cols), (e, inv) in zip(units, ps):
            o = jnp.dot(e, v_ref[pl.ds(k0, strip), cols], preferred_element_type=F32) * inv
            o_ref[pl.ds(q0, GRID_W), cols] = o.astype(o_ref.dtype)
        return carry

    lax.fori_loop(0, rows // NBR_ROWS, body, 0)


def neighbourhood_attention(vqkv, bias_tbl, batch, seq):
    gw = NBR_HG * HEAD_DIM
    q0, k0, v0 = KVA_W // gw, (KVA_W + QB_W) // gw, (KVA_W + 2 * QB_W) // gw
    kh = bias_tbl.shape[1]
    return pl.pallas_call(
        functools.partial(_nbr_body, seq=seq),
        grid=(B_HEADS // NBR_HG, batch),
        in_specs=[pl.BlockSpec((seq, gw), lambda g, b: (b, q0 + g)),
                  pl.BlockSpec((seq, gw), lambda g, b: (b, k0 + g)),
                  pl.BlockSpec((seq, gw), lambda g, b: (b, v0 + g)),
                  pl.BlockSpec((NBR_HG, kh, GRID_W, kh * GRID_W), lambda g, b: (g, 0, 0, 0))],
        out_specs=pl.BlockSpec((seq, gw), lambda g, b: (b, g)),
        out_shape=jax.ShapeDtypeStruct((batch * seq, QB_W), BF16),
        compiler_params=_params(("arbitrary", "arbitrary"), 40),
        name="neighbourhood_attention",
    )(vqkv, vqkv, vqkv, bias_tbl)


def _bias_table_body(rpb_ref, o_ref, *, kh):
    h = pl.program_id(0)
    n_dr, n_dc = 2 * NA_KH_MAX - 1, 2 * NA_KW - 1
    c = lax.broadcasted_iota(jnp.int32, (GRID_W, LANES), 0)
    lane = lax.broadcasted_iota(jnp.int32, (GRID_W, LANES), 1)
    kc = lane % GRID_W
    diff = jnp.clip(kc - c + NA_KW - 1, 0, n_dc - 1)
    cs = jnp.clip(c - NA_KW // 2, 0, GRID_W - NA_KW)
    col_ok = (kc >= cs) & (kc < cs + NA_KW)
    slabs = []
    for dr in range(n_dr):
        acc = jnp.zeros((GRID_W, LANES), F32)
        for d in range(n_dc):
            acc = jnp.where(diff == d, rpb_ref[(h * n_dr + dr) * n_dc + d], acc)
        slabs.append(jnp.where(col_ok, acc * LOG2E, NEG))
    left = lane < GRID_W
    for var in range(kh):
        for jp in range(kh * GRID_W // LANES):
            dr0 = 2 * jp - var + NA_KH_MAX - 1
            o_ref[0, var, :, jp * LANES:(jp + 1) * LANES] = jnp.where(left, slabs[dr0], slabs[dr0 + 1])


def _nbr_bias_table(rpb, seq):
    rows = seq // GRID_W
    kh = min(NA_KH_MAX, rows)
    heads = rpb.shape[0]
    assert kh == NA_KH_MAX and 2 * GRID_W == LANES
    return pl.pallas_call(
        functools.partial(_bias_table_body, kh=kh),
        grid=(heads,),
        in_specs=[pl.BlockSpec(memory_space=pltpu.SMEM)],
        out_specs=pl.BlockSpec((1, kh, GRID_W, kh * GRID_W), lambda h: (h, 0, 0, 0)),
        out_shape=jax.ShapeDtypeStruct((heads, kh, GRID_W, kh * GRID_W), F32),
        compiler_params=_params(("arbitrary",), 16),
        name="nbr_bias_table",
    )(rpb.astype(F32).reshape(-1))


def _merge_body(oa_ref, ob_ref, wa_ref, wb_ref, g0_ref, g1_ref, o_ref):
    wa, wb = wa_ref[...].astype(BF16), wb_ref[...].astype(BF16)
    for rows in _row_parts(oa_ref.shape[0]):
        ya = jnp.dot(oa_ref[rows, :], wa, preferred_element_type=F32)
        yb = jnp.dot(ob_ref[rows, :], wb, preferred_element_type=F32)
        o_ref[rows, :] = (g0_ref[rows, :].astype(F32) * ya
                          + g1_ref[rows, :].astype(F32) * yb).astype(o_ref.dtype)


def branch_merge(oa, ob, wa, wb, gates, bm=2048, bn=256):
    m, k = oa.shape
    n = wa.shape[1]
    g1_off = n // bn
    return pl.pallas_call(
        _merge_body,
        grid=(m // bm, n // bn),
        in_specs=[pl.BlockSpec((bm, k), lambda i, j: (i, 0)),
                  pl.BlockSpec((bm, k), lambda i, j: (i, 0)),
                  pl.BlockSpec((k, bn), lambda i, j: (0, j)),
                  pl.BlockSpec((k, bn), lambda i, j: (0, j)),
                  pl.BlockSpec((bm, bn), lambda i, j: (i, j)),
                  pl.BlockSpec((bm, bn), lambda i, j: (i, j + g1_off))],
        out_specs=pl.BlockSpec((bm, bn), lambda i, j: (i, j)),
        out_shape=jax.ShapeDtypeStruct((m, n), BF16),
        compiler_params=_params(("arbitrary", "arbitrary"), 56),
        name="branch_merge",
    )(oa, ob, wa, wb, gates, gates)


def _cast_body(x_ref, o_ref):
    o_ref[...] = x_ref[...].astype(o_ref.dtype)


def cast_bf16(w):
    r, c = w.shape
    return pl.pallas_call(
        _cast_body,
        grid=(1,),
        in_specs=[pl.BlockSpec((r, c), lambda i: (0, 0))],
        out_specs=pl.BlockSpec((r, c), lambda i: (0, 0)),
        out_shape=jax.ShapeDtypeStruct((r, c), BF16),
        compiler_params=_params(("arbitrary",), 40),
        name="cast_bf16",
    )(w)


def _router_probs(hn, wr_ref):
    hi = hn.astype(BF16)
    lo = (hn - hi.astype(F32)).astype(BF16)
    l_hi = jnp.dot(hi, wr_ref[...], preferred_element_type=F32)
    l_lo = jnp.dot(lo, wr_ref[...], preferred_element_type=F32)
    logits = l_hi + pltpu.roll(l_hi, LANES - N_EXPERTS, 1) + l_lo
    lane = lax.broadcasted_iota(jnp.int32, logits.shape, 1)
    logits = jnp.where(lane < N_EXPERTS, logits, NEG)
    return _softmax_rows(logits)


def _xblock_body(x_ref, gc_ref, wq_ref, k_ref, v_ref, wo_ref, gf_ref, wr_ref,
                 x2_ref, h3_ref, aff_ref):
    scale = HEAD_DIM ** -0.5
    half = x_ref.shape[1] // 2
    parts = [slice(p * XB_ROWS, (p + 1) * XB_ROWS) for p in range(x_ref.shape[0] // XB_ROWS)]
    h2s = [_rmsnorm_rows(x_ref[r, :], gc_ref[...]).astype(BF16) for r in parts]
    qs = [jnp.dot(h2, wq_ref[...], preferred_element_type=F32).astype(BF16) for h2 in h2s]
    head_cols = [slice(h * HEAD_DIM, (h + 1) * HEAD_DIM) for h in range(X_HEADS)]
    ss = [[lax.dot_general(q[:, cols], k_ref[:, cols], _NT, preferred_element_type=F32) * scale
           for cols in head_cols] for q in qs]
    ps = [[_softmax_rows(s).astype(BF16) for s in row] for row in ss]
    os = [jnp.concatenate([jnp.dot(p, v_ref[:, cols], preferred_element_type=F32).astype(BF16)
                           for p, cols in zip(row, head_cols)], axis=1) for row in ps]
    for r, o in zip(parts, os):
        x2_ref[r, :] = x_ref[r, :] + jnp.dot(o, wo_ref[...], preferred_element_type=F32)
    for r in parts:
        hn = _rmsnorm_rows(x2_ref[r, :], gf_ref[...])
        packed = pltpu.pack_elementwise([hn[:, :half], hn[:, half:]], packed_dtype=BF16)
        h3_ref[r, :] = pltpu.bitcast(packed, jnp.uint32)
        aff = _router_probs(hn, wr_ref)
        aff_ref[0, :, r] = aff.T[:N_EXPERTS, :]


XB_ROWS = 256


def cross_attention_block(x1, g_cross, wq, kx, vx, wo, g_ffn, w_router, batch, seq, mem_len, bm=512):
    m, d = x1.shape
    nt = seq // bm
    w_hi = w_router.astype(BF16)
    w_lo = (w_router - w_hi.astype(F32)).astype(BF16)
    wr = jnp.concatenate([w_hi, w_lo, jnp.zeros((d, LANES - 2 * N_EXPERTS), BF16)], axis=1)
    const = lambda i: (0, 0)
    once = pl.Buffered(1)
    return pl.pallas_call(
        _xblock_body,
        grid=(m // bm,),
        in_specs=[pl.BlockSpec((bm, d), lambda i: (i, 0)),
                  pl.BlockSpec((1, d), const),
                  pl.BlockSpec((d, X_W), const, pipeline_mode=once),
                  pl.BlockSpec((mem_len, X_W), lambda i: (i // nt, 0)),
                  pl.BlockSpec((mem_len, X_W), lambda i: (i // nt, 0)),
                  pl.BlockSpec((X_W, d), const, pipeline_mode=once),
                  pl.BlockSpec((1, d), const),
                  pl.BlockSpec((d, LANES), const, pipeline_mode=once)],
        out_specs=[pl.BlockSpec((bm, d), lambda i: (i, 0)),
                   pl.BlockSpec((bm, d // 2), lambda i: (i, 0)),
                   pl.BlockSpec((1, N_EXPERTS, bm), lambda i: (i // nt, 0, i % nt))],
        out_shape=[jax.ShapeDtypeStruct((m, d), F32),
                   jax.ShapeDtypeStruct((m, d // 2), jnp.uint32),
                   jax.ShapeDtypeStruct((batch, N_EXPERTS, seq), F32)],
        compiler_params=_params(("arbitrary",), 58),
        name="cross_attention_block",
    )(x1, g_cross.reshape(1, d), cast_bf16(wq), kx, vx, cast_bf16(wo), g_ffn.reshape(1, d), wr)


CUM_CHUNK = 256


def _excl_cumsum_lanes(x01, tri):
    n = x01.shape[1]
    carry = jnp.zeros((x01.shape[0], 1), F32)
    out = []
    for c in range(n // CUM_CHUNK):
        xc = x01[:, c * CUM_CHUNK:(c + 1) * CUM_CHUNK]
        out.append(jnp.dot(xc.astype(BF16), tri, preferred_element_type=F32) + carry)
        carry = carry + xc.sum(axis=1, keepdims=True)
    return jnp.concatenate(out, axis=1)


COMBINE_TS = 256
COMBINE_W = 64


def _topk_body(aff_ref, slot_ref, first_ref, *, cap):
    a = aff_ref[0]
    n_exp, s = a.shape
    capf = jnp.float32(cap)
    lane_s = lax.broadcasted_iota(jnp.int32, (n_exp, s), 1)

    def count(mask):
        return jnp.where(mask, 1.0, 0.0).sum(axis=1, keepdims=True)

    bits = pltpu.bitcast(a, jnp.int32)

    def search(i, t):
        cand = t | jnp.left_shift(jnp.int32(1), 30 - i)
        return jnp.where(count(bits >= cand) >= capf, cand, t)

    t = lax.fori_loop(0, 31, search, jnp.zeros((n_exp, 1), jnp.int32))
    at = jnp.where(bits == t, lane_s, s).astype(F32).min(axis=1, keepdims=True).astype(jnp.int32)
    pivot = jnp.where(lane_s == at, a, 0.0).sum(axis=1, keepdims=True)

    def stats(p):
        return p, count(a > p), count(a >= p)

    def wrong(state):
        _, n_gt, n_ge = state
        return jnp.where(jnp.logical_or(n_gt >= capf, n_ge < capf), 1.0, 0.0).sum() > 0.0

    def step(state):
        p, n_gt, n_ge = state
        up = jnp.where(a > p, a, jnp.inf).min(axis=1, keepdims=True)
        down = jnp.where(a < p, a, -jnp.inf).max(axis=1, keepdims=True)
        return stats(jnp.where(n_gt >= capf, up, jnp.where(n_ge < capf, down, p)))

    pivot, n_gt, _ = lax.while_loop(wrong, step, stats(pivot))
    ri = lax.broadcasted_iota(jnp.int32, (CUM_CHUNK, CUM_CHUNK), 0)
    cj = lax.broadcasted_iota(jnp.int32, (CUM_CHUNK, CUM_CHUNK), 1)
    tri = jnp.where(ri < cj, 1.0, 0.0).astype(BF16)
    eq = jnp.where(a == pivot, 1.0, 0.0)
    sel = jnp.where(a > pivot, 1.0, 0.0) + jnp.where(_excl_cumsum_lanes(eq, tri) < capf - n_gt, eq, 0.0)
    pos = _excl_cumsum_lanes(sel, tri)
    slot_ref[0] = jnp.where(sel > 0.5, pos, -1.0).astype(jnp.int32)
    lane_k = lax.broadcasted_iota(jnp.int32, (n_exp, LANES), 1)
    first = jnp.zeros((n_exp, LANES), F32)
    for k in range(s // COMBINE_TS):
        at_k = jnp.where(lane_s == k * COMBINE_TS, pos, 0.0).sum(axis=1, keepdims=True)
        first = jnp.where(lane_k == k, at_k, first)
    first_ref[0] = first.astype(jnp.int32)


def expert_topk(aff_t, cap):
    b, e, s = aff_t.shape
    return pl.pallas_call(
        functools.partial(_topk_body, cap=cap),
        grid=(b,),
        in_specs=[pl.BlockSpec((1, e, s), lambda i: (i, 0, 0))],
        out_specs=[pl.BlockSpec((1, e, s), lambda i: (i, 0, 0)),
                   pl.BlockSpec((1, e, LANES), lambda i: (i, 0, 0))],
        out_shape=[jax.ShapeDtypeStruct((b, e, s), jnp.int32),
                   jax.ShapeDtypeStruct((b, e, LANES), jnp.int32)],
        compiler_params=_params(("arbitrary",), 32),
        name="expert_topk",
    )(aff_t)


def _slot_index_body(slot_ref, aff_ref, idx_ref, val_ref, *, cap):
    b = pl.program_id(0)
    n_exp, s = slot_ref.shape[1], slot_ref.shape[2]
    ci = lax.broadcasted_iota(jnp.int32, (cap, s), 0)
    tok = lax.broadcasted_iota(jnp.int32, (cap, s), 1).astype(F32)
    base = (b * s).astype(F32)
    for e in range(n_exp):
        hit = slot_ref[0, e:e + 1, :] == ci
        idx = jnp.where(hit, tok, 0.0).sum(axis=1, keepdims=True) + base
        idx_ref[e] = idx.astype(jnp.int32)
        val_ref[e] = jnp.where(hit, aff_ref[0, e:e + 1, :], 0.0).sum(axis=1, keepdims=True)


def slot_index(slot, aff_t, cap):
    b, e, s = slot.shape
    return pl.pallas_call(
        functools.partial(_slot_index_body, cap=cap),
        grid=(b,),
        in_specs=[pl.BlockSpec((1, e, s), lambda bi: (bi, 0, 0)),
                  pl.BlockSpec((1, e, s), lambda bi: (bi, 0, 0))],
        out_specs=[pl.BlockSpec((e, cap, 1), lambda bi: (0, bi, 0)),
                   pl.BlockSpec((e, cap, 1), lambda bi: (0, bi, 0))],
        out_shape=[jax.ShapeDtypeStruct((e, b * cap, 1), jnp.int32),
                   jax.ShapeDtypeStruct((e, b * cap, 1), F32)],
        compiler_params=_params(("arbitrary",), 32),
        name="slot_index",
    )(slot, aff_t)


UNPACK_ROWS = 128


def _row_copy(idx_ref, hp_ref, gbuf, sem, expert, rows, row):
    tok = idx_ref[expert * rows + row]
    return pltpu.make_async_copy(hp_ref.at[pl.ds(tok, 1)], gbuf.at[pl.ds(row, 1)], sem)


def _expert_up_body(idx_ref, hp_ref, wg_ref, wu_ref, o_ref, gbuf, xbf, sem, *, rows, per_step):
    e, f = pl.program_id(0), pl.program_id(1)
    n_e, n_f = pl.num_programs(0), pl.num_programs(1)
    half = gbuf.shape[1]

    def wait_all_rows():
        pltpu.make_async_copy(hp_ref.at[pl.ds(0, rows)], gbuf, sem).wait()

    @pl.when(jnp.logical_and(e == 0, f == 0))
    def _():
        def body(r, carry):
            _row_copy(idx_ref, hp_ref, gbuf, sem, 0, rows, r).start()
            return carry
        lax.fori_loop(0, rows, body, 0)

    @pl.when(f == 0)
    def _():
        wait_all_rows()

        def unpack(k, carry):
            r = pl.multiple_of(k * UNPACK_ROWS, UNPACK_ROWS)
            w = gbuf[pl.ds(r, UNPACK_ROWS), :]
            lo = pltpu.unpack_elementwise(w, index=0, packed_dtype=BF16, unpacked_dtype=F32)
            hi = pltpu.unpack_elementwise(w, index=1, packed_dtype=BF16, unpacked_dtype=F32)
            xbf[pl.ds(r, UNPACK_ROWS), :half] = lo.astype(BF16)
            xbf[pl.ds(r, UNPACK_ROWS), half:] = hi.astype(BF16)
            return carry
        lax.fori_loop(0, rows // UNPACK_ROWS, unpack, 0)

    nxt = jnp.minimum(e + 1, n_e - 1)
    for r in range(per_step):
        _row_copy(idx_ref, hp_ref, gbuf, sem, nxt, rows, f * per_step + r).start()

    wg, wu = wg_ref[0].astype(BF16), wu_ref[0].astype(BF16)
    for part in _row_parts(rows):
        a = jnp.dot(xbf[part, :], wg, preferred_element_type=F32)
        u = jnp.dot(xbf[part, :], wu, preferred_element_type=F32)
        o_ref[0, part, :] = (a * _sigmoid(a) * u).astype(o_ref.dtype)

    @pl.when(jnp.logical_and(e == n_e - 1, f == n_f - 1))
    def _():
        wait_all_rows()


def expert_up(idx, hp, w_gate, w_up, rows, tf=256):
    n_e, d, f = w_gate.shape
    assert hp.shape[1] * 2 == d and rows % (f // tf) == 0
    grid_spec = pltpu.PrefetchScalarGridSpec(
        num_scalar_prefetch=1,
        grid=(n_e, f // tf),
        in_specs=[pl.BlockSpec(memory_space=pl.ANY),
                  pl.BlockSpec((1, d, tf), lambda ei, fi, idx_ref: (ei, 0, fi)),
                  pl.BlockSpec((1, d, tf), lambda ei, fi, idx_ref: (ei, 0, fi))],
        out_specs=pl.BlockSpec((1, rows, tf), lambda ei, fi, idx_ref: (ei, 0, fi)),
        scratch_shapes=[pltpu.VMEM((rows, d // 2), jnp.uint32),
                        pltpu.VMEM((rows, d), BF16),
                        pltpu.SemaphoreType.DMA(())],
    )
    return pl.pallas_call(
        functools.partial(_expert_up_body, rows=rows, per_step=rows // (f // tf)),
        grid_spec=grid_spec,
        out_shape=jax.ShapeDtypeStruct((n_e, rows, f), BF16),
        compiler_params=_params(("arbitrary", "arbitrary"), 56),
        name="expert_up",
    )(idx, hp, w_gate, w_up)


def _expert_down_body(h_ref, wd_ref, val_ref, o_ref):
    wd = wd_ref[0].astype(BF16)
    for part in _row_parts(h_ref.shape[1]):
        y = jnp.dot(h_ref[0, part, :], wd, preferred_element_type=F32)
        o_ref[0, part, :] = (y * val_ref[0, part, :]).astype(o_ref.dtype)


def expert_down(hmid, w_down, valc, tn=1024):
    e, rows, f = hmid.shape
    d = w_down.shape[-1]
    return pl.pallas_call(
        _expert_down_body,
        grid=(e, d // tn),
        in_specs=[pl.BlockSpec((1, rows, f), lambda ei, ni: (ei, 0, 0)),
                  pl.BlockSpec((1, f, tn), lambda ei, ni: (ei, 0, ni)),
                  pl.BlockSpec((1, rows, 1), lambda ei, ni: (ei, 0, 0))],
        out_specs=pl.BlockSpec((1, rows, tn), lambda ei, ni: (ei, 0, ni)),
        out_shape=jax.ShapeDtypeStruct((e, rows, d), BF16),
        compiler_params=_params(("arbitrary", "arbitrary"), 48),
        name="expert_down",
    )(hmid, w_down, valc)


COMBINE_TN = 512
ROW_ALIGN_BF16 = 16


def _combine_windows(first, cap, n_rows):
    b, e, nt = first.shape
    row0 = jnp.arange(b, dtype=jnp.int32)[:, None, None] * cap
    lo = first + row0
    hi = jnp.concatenate([first[:, :, 1:], jnp.full((b, e, 1), cap, jnp.int32)], axis=2) + row0
    start = jnp.minimum(lo // ROW_ALIGN_BF16 * ROW_ALIGN_BF16, n_rows - COMBINE_W)
    fast = jnp.all(hi - start <= COMBINE_W, axis=1)
    return start.transpose(0, 2, 1).reshape(-1), fast.astype(jnp.int32).reshape(-1)


def _combine_body(win_ref, fast_ref, slot_ref, y_hbm, x_ref, g_ref, o_ref,
                  ywin, oht, ybuf, ohs, wsem, ssem, *, cap, n_tiles):
    i = pl.program_id(0)
    n_exp = slot_ref.shape[1]
    ts, d = o_ref.shape
    w = COMBINE_W
    row0 = (i // n_tiles) * cap
    buf = i % 2

    def window_copy(step, e, b):
        start = pl.multiple_of(win_ref[step * n_exp + e], ROW_ALIGN_BF16)
        return pltpu.make_async_copy(y_hbm.at[e, pl.ds(start, w)], ywin.at[b, pl.ds(e * w, w)],
                                     wsem.at[b])

    @pl.when(i == 0)
    def _():
        for e in range(n_exp):
            window_copy(0, e, 0).start()

    @pl.when(i + 1 < pl.num_programs(0))
    def _():
        for e in range(n_exp):
            window_copy(i + 1, e, 1 - buf).start()

    pltpu.make_async_copy(y_hbm.at[0, pl.ds(0, n_exp * w)], ywin.at[buf], wsem.at[buf]).wait()

    @pl.when(fast_ref[i] == 1)
    def _():
        row = lax.broadcasted_iota(jnp.int32, (2 * w, ts), 0)
        upper = row >= w
        j = row % w
        for p in range(n_exp // 2):
            rel_a = slot_ref[0, 2 * p:2 * p + 1, :] + (row0 - win_ref[i * n_exp + 2 * p])
            rel_b = slot_ref[0, 2 * p + 1:2 * p + 2, :] + (row0 - win_ref[i * n_exp + 2 * p + 1])
            hit = jnp.where(upper, rel_b, rel_a) == j
            oht[:, p * 2 * w:(p + 1) * 2 * w] = jnp.where(hit, 1.0, 0.0).T.astype(BF16)
        for c in range(d // COMBINE_TN):
            cols = slice(c * COMBINE_TN, (c + 1) * COMBINE_TN)
            o_ref[:, cols] = x_ref[:, cols] + jnp.dot(oht[...], ywin[buf, :, cols],
                                                      preferred_element_type=F32)

    @pl.when(fast_ref[i] == 0)
    def _():
        o_ref[...] = x_ref[...]
        ci = lax.broadcasted_iota(jnp.int32, (cap, ts), 0)

        def body(e, carry):
            cp = pltpu.make_async_copy(y_hbm.at[e, pl.ds(pl.multiple_of(row0, cap), cap)], ybuf, ssem)
            cp.start()
            cp.wait()
            srow = slot_ref[0, pl.ds(e, 1), :]
            ohs[...] = jnp.where(srow == ci, 1.0, 0.0).T.astype(BF16)
            for c in range(d // COMBINE_TN):
                cols = slice(c * COMBINE_TN, (c + 1) * COMBINE_TN)
                o_ref[:, cols] += jnp.dot(ohs[...], ybuf[:, cols], preferred_element_type=F32)
            return carry
        lax.fori_loop(0, n_exp, body, 0)

    o_ref[...] = _rmsnorm_rows(o_ref[...], g_ref[...])


def expert_combine(slot, first, y, x2d, g, cap):
    b, e, s = slot.shape
    d = x2d.shape[-1]
    ts, w = COMBINE_TS, COMBINE_W
    nt = s // ts
    assert e % 2 == 0 and y.shape[1] >= e * w and cap % ROW_ALIGN_BF16 == 0
    win, fast = _combine_windows(first[:, :, :nt], cap, y.shape[1])
    grid_spec = pltpu.PrefetchScalarGridSpec(
        num_scalar_prefetch=2,
        grid=(b * nt,),
        in_specs=[pl.BlockSpec((1, e, ts), lambda i, win_ref, fast_ref: (i // nt, 0, i % nt)),
                  pl.BlockSpec(memory_space=pl.ANY),
                  pl.BlockSpec((ts, d), lambda i, win_ref, fast_ref: (i, 0)),
                  pl.BlockSpec((1, d), lambda i, win_ref, fast_ref: (0, 0))],
        out_specs=pl.BlockSpec((ts, d), lambda i, win_ref, fast_ref: (i, 0)),
        scratch_shapes=[pltpu.VMEM((2, e * w, d), BF16),
                        pltpu.VMEM((ts, e * w), BF16),
                        pltpu.VMEM((cap, d), BF16),
                        pltpu.VMEM((ts, cap), BF16),
                        pltpu.SemaphoreType.DMA((2,)),
                        pltpu.SemaphoreType.DMA(())],
    )
    return pl.pallas_call(
        functools.partial(_combine_body, cap=cap, n_tiles=nt),
        grid_spec=grid_spec,
        out_shape=jax.ShapeDtypeStruct(x2d.shape, F32),
        compiler_params=_params(("arbitrary",), 48),
        name="expert_combine",
    )(win, fast, slot, y, x2d, g.reshape(1, d))


def _rotary_tables(seq):
    half = ROT_DIM // 2
    inv = ROPE_THETA ** (-jnp.arange(half, dtype=F32) * 2.0 / ROT_DIM)
    ang = jnp.arange(seq).astype(F32)[:, None] * inv[None, :]
    cos, sin = jnp.cos(ang), jnp.sin(ang)
    ones = jnp.ones((seq, HEAD_DIM - ROT_DIM), F32)
    zeros = jnp.zeros((seq, HEAD_DIM - ROT_DIM), F32)
    zh = jnp.zeros((seq, half), F32)
    c = jnp.concatenate([cos, cos, ones], axis=1)
    s1 = jnp.concatenate([-sin, zh, zeros], axis=1)
    s2 = jnp.concatenate([zh, sin, zeros], axis=1)
    return c, s1, s2


def kernel(x, mem, norm_mix, w_in, b_gate, sink, rpb, w_branch_a, w_branch_b, w_out,
           norm_cross, norm_mem, wq_x, wk_x, wv_x, wo_x, norm_ffn, w_router,
           w_gate, w_up, w_down, norm_final):
    batch, seq, d = x.shape
    mem_len = mem.shape[1]
    m = batch * seq
    assert norm_mix.shape[0] == 1, "final RMSNorm is fused into the single layer's last kernel"
    cap = EC_CAPACITY * seq // N_EXPERTS
    bm, bn = 2048, 256
    sb = seq // bm
    x0 = x.reshape(m, d)

    h = rmsnorm(x0, norm_mix[0], BF16)
    rot = _rotary_tables(seq)
    rot_specs = [pl.BlockSpec((bm, HEAD_DIM), lambda i, j: (i % sb, 0))] * 3
    qk = matmul_rows(h, w_in[0], col_off=0, n_cols=QA_W + KVA_W, bm=bm, bn=bn, out_dtype=BF16,
                     epilogue=_ep_rotary, extras=rot, extra_specs=rot_specs, row_split=2 * ROW_SPLIT,
                     name="in_proj_rotary")
    vqkv = matmul_rows(h, w_in[0], col_off=QA_W + KVA_W, n_cols=KVA_W + 3 * QB_W, bm=bm, bn=bn,
                       out_dtype=BF16, name="in_proj_plain")
    g_off = QA_W + 2 * KVA_W + 3 * QB_W
    gates = matmul_rows(h, w_in[0], col_off=g_off, n_cols=2 * d, bm=bm, bn=bn, out_dtype=BF16,
                        epilogue=_ep_sigmoid, extras=(b_gate[0].reshape(1, 2 * d),),
                        extra_specs=[pl.BlockSpec((1, bn), lambda i, j: (0, j))], name="in_proj_gates")
    oa = window_attention(qk, vqkv, sink[0], batch, seq)
    ob = neighbourhood_attention(vqkv, _nbr_bias_table(rpb[0], seq), batch, seq)
    merged = branch_merge(oa, ob, w_branch_a[0], w_branch_b[0], gates, bm=bm, bn=bn)
    res_spec = [pl.BlockSpec((bm, bn), lambda i, j: (i, j))]
    x1 = matmul_rows(merged, w_out[0], col_off=0, n_cols=d, bm=bm, bn=bn, out_dtype=F32,
                     epilogue=_ep_residual, extras=(x0,), extra_specs=res_spec, name="out_proj")

    mem_rows = batch * mem_len
    mn = rmsnorm(mem.reshape(mem_rows, d), norm_mem[0], BF16)
    kx = matmul_rows(mn, wk_x[0], col_off=0, n_cols=X_W, bm=mem_rows, bn=bn, out_dtype=BF16, name="xattn_k")
    vx = matmul_rows(mn, wv_x[0], col_off=0, n_cols=X_W, bm=mem_rows, bn=bn, out_dtype=BF16, name="xattn_v")
    x2, h3p, aff_t = cross_attention_block(
        x1, norm_cross[0], wq_x[0], kx, vx, wo_x[0], norm_ffn[0], w_router[0], batch, seq, mem_len)

    slot, first = expert_topk(aff_t, cap)
    idx, valc = slot_index(slot, aff_t, cap)
    hmid = expert_up(idx.reshape(-1), h3p, w_gate[0], w_up[0], batch * cap)
    y = expert_down(hmid, w_down[0], valc)
    out = expert_combine(slot, first, y, x2, norm_final, cap)
    return out.reshape(batch, seq, d)
```

```python
import functools

import jax
import jax.numpy as jnp
from jax import lax
from jax.experimental import pallas as pl
from jax.experimental.pallas import tpu as pltpu

F32 = jnp.float32
BF16 = jnp.bfloat16

HEAD_DIM = 128
A_HEADS = 16
A_KV_HEADS = 4
A_GROUP = A_HEADS // A_KV_HEADS
WINDOW = 128
A_BLOCK = 128
ROT_DIM = HEAD_DIM // 4
ROPE_THETA = 500000.0
B_HEADS = 16
GRID_W = 64
NA_KH_MAX = 8
NA_KW = 16
X_HEADS = 4
N_EXPERTS = 16
EC_CAPACITY = 2
EPS = 1e-6
NEG = -1e30
LOG2E = 1.4426950408889634
LANES = 128
MIB = 1024 * 1024

QA_W = A_HEADS * HEAD_DIM
KVA_W = A_KV_HEADS * HEAD_DIM
QB_W = B_HEADS * HEAD_DIM
X_W = X_HEADS * HEAD_DIM

_NT = (((1,), (1,)), ((), ()))
_TN = (((0,), (0,)), ((), ()))


def _params(semantics, vmem_mib):
    return pltpu.CompilerParams(dimension_semantics=semantics,
                                vmem_limit_bytes=vmem_mib * MIB)


def _rmsnorm_rows(x, g):
    ms = jnp.mean(x * x, axis=-1, keepdims=True)
    return x * lax.rsqrt(ms + EPS) * g


def _rmsnorm_body(x_ref, g_ref, o_ref):
    o_ref[...] = _rmsnorm_rows(x_ref[...], g_ref[...]).astype(o_ref.dtype)


def rmsnorm(x2d, g, out_dtype, bm=512):
    m, d = x2d.shape
    return pl.pallas_call(
        _rmsnorm_body,
        grid=(m // bm,),
        in_specs=[pl.BlockSpec((bm, d), lambda i: (i, 0)),
                  pl.BlockSpec((1, d), lambda i: (0, 0))],
        out_specs=pl.BlockSpec((bm, d), lambda i: (i, 0)),
        out_shape=jax.ShapeDtypeStruct((m, d), out_dtype),
        compiler_params=_params(("arbitrary",), 40),
        name="rmsnorm",
    )(x2d, g.reshape(1, d))


def _sigmoid(x):
    return 0.5 * jnp.tanh(0.5 * x) + 0.5


ROW_SPLIT = 4


def _row_parts(n_rows, split=ROW_SPLIT):
    step = n_rows // split
    return [slice(p * step, (p + 1) * step) for p in range(split)]


def _ep_store(acc, rows, o_ref):
    o_ref[rows, :] = acc.astype(o_ref.dtype)


def _ep_residual(acc, rows, o_ref, r_ref):
    o_ref[rows, :] = (r_ref[rows, :] + acc).astype(o_ref.dtype)


def _ep_sigmoid(acc, rows, o_ref, b_ref):
    o_ref[rows, :] = _sigmoid((acc + b_ref[...]).astype(o_ref.dtype))


def _ep_rotary(acc, rows, o_ref, c_ref, s1_ref, s2_ref):
    c, s1, s2 = c_ref[rows, :], s1_ref[rows, :], s2_ref[rows, :]
    half = ROT_DIM // 2
    for h in range(acc.shape[1] // HEAD_DIM):
        a = acc[:, h * HEAD_DIM:(h + 1) * HEAD_DIM]
        r = a * c + pltpu.roll(a, HEAD_DIM - half, 1) * s1 + pltpu.roll(a, half, 1) * s2
        o_ref[rows, h * HEAD_DIM:(h + 1) * HEAD_DIM] = r.astype(o_ref.dtype)


def _mm_rows_body(*refs, n_extra, epilogue, row_split):
    a_ref, w_ref = refs[0], refs[1]
    extra = refs[2:2 + n_extra]
    o_ref = refs[2 + n_extra]
    wb = w_ref[...].astype(BF16)
    for rows in _row_parts(a_ref.shape[0], row_split):
        acc = jnp.dot(a_ref[rows, :], wb, preferred_element_type=F32)
        epilogue(acc, rows, o_ref, *extra)


def matmul_rows(a, w, *, col_off, n_cols, bm, bn, out_dtype, epilogue=_ep_store,
                extras=(), extra_specs=(), row_split=ROW_SPLIT, vmem_mib=56, name="matmul_rows"):
    m, k = a.shape
    off = col_off // bn
    assert col_off % bn == 0 and n_cols % bn == 0 and m % bm == 0
    body = functools.partial(_mm_rows_body, n_extra=len(extras), epilogue=epilogue, row_split=row_split)
    return pl.pallas_call(
        body,
        grid=(m // bm, n_cols // bn),
        in_specs=[pl.BlockSpec((bm, k), lambda i, j: (i, 0)),
                  pl.BlockSpec((k, bn), lambda i, j: (0, j + off))] + list(extra_specs),
        out_specs=pl.BlockSpec((bm, bn), lambda i, j: (i, j)),
        out_shape=jax.ShapeDtypeStruct((m, n_cols), out_dtype),
        compiler_params=_params(("arbitrary", "arbitrary"), vmem_mib),
        name=name,
    )(a, w, *extras)


def _softmax_rows(s):
    e = jnp.exp(s - s.max(axis=1, keepdims=True))
    return e * (1.0 / e.sum(axis=1, keepdims=True))


WIN_UNROLL = 8


def _win_body(sink_ref, q_ref, k_ref, v_ref, o_ref, *, seq):
    kv = pl.program_id(1)
    nb = seq // A_BLOCK
    scale = HEAD_DIM ** -0.5 * LOG2E
    rows = A_GROUP * A_BLOCK
    qi = lax.broadcasted_iota(jnp.int32, (rows, A_BLOCK), 0) % A_BLOCK
    ci = lax.broadcasted_iota(jnp.int32, (rows, A_BLOCK), 1)
    sink_b = jnp.concatenate(
        [jnp.full((A_BLOCK, HEAD_DIM), sink_ref[kv * A_GROUP + g] * LOG2E, F32)
         for g in range(A_GROUP)], axis=0)

    def scores(n):
        r0 = pl.multiple_of(n * A_BLOCK, A_BLOCK)
        rp = pl.multiple_of(jnp.maximum(n - 1, 0) * A_BLOCK, A_BLOCK)
        rn = pl.multiple_of(jnp.minimum(n + 1, nb - 1) * A_BLOCK, A_BLOCK)
        off_p = jnp.where(n > 0, 0, 2 * A_BLOCK)
        off_n = jnp.where(n < nb - 1, 0, 2 * A_BLOCK)
        q = jnp.concatenate(
            [q_ref[pl.ds(r0, A_BLOCK), g * HEAD_DIM:(g + 1) * HEAD_DIM] for g in range(A_GROUP)],
            axis=0)
        sp = lax.dot_general(q, k_ref[pl.ds(rp, A_BLOCK), :], _NT, preferred_element_type=F32) * scale
        sc = lax.dot_general(q, k_ref[pl.ds(r0, A_BLOCK), :], _NT, preferred_element_type=F32) * scale
        sn = lax.dot_general(q, k_ref[pl.ds(rn, A_BLOCK), :], _NT, preferred_element_type=F32) * scale
        sp = jnp.where(ci >= qi + off_p, sp, NEG)
        sn = jnp.where(ci <= qi - off_n, sn, NEG)
        return (rp, r0, rn), (sp, sc, sn)

    def exps(parts):
        m = jnp.maximum(jnp.maximum(parts[0], parts[1]), parts[2]).max(axis=1, keepdims=True)
        m = jnp.maximum(jnp.broadcast_to(m, sink_b.shape), sink_b)
        es = [jnp.exp2(p - m) for p in parts]
        den = (es[0] + es[1] + es[2]).sum(axis=1, keepdims=True)
        den = jnp.broadcast_to(den, sink_b.shape) + jnp.exp2(sink_b - m)
        return [e.astype(BF16) for e in es], 1.0 / den

    def body(it, carry):
        blocks = [scores(it * WIN_UNROLL + u) for u in range(WIN_UNROLL)]
        probs = [exps(parts) for _, parts in blocks]
        for (rows_kv, _), (es, inv) in zip(blocks, probs):
            o = jnp.dot(es[0], v_ref[pl.ds(rows_kv[0], A_BLOCK), :], preferred_element_type=F32)
            o = o + jnp.dot(es[1], v_ref[pl.ds(rows_kv[1], A_BLOCK), :], preferred_element_type=F32)
            o = o + jnp.dot(es[2], v_ref[pl.ds(rows_kv[2], A_BLOCK), :], preferred_element_type=F32)
            o = o * inv
            for g in range(A_GROUP):
                o_ref[pl.ds(rows_kv[1], A_BLOCK), g * HEAD_DIM:(g + 1) * HEAD_DIM] = (
                    o[g * A_BLOCK:(g + 1) * A_BLOCK].astype(o_ref.dtype))
        return carry

    lax.fori_loop(0, nb // WIN_UNROLL, body, 0)


def window_attention(qk, vqkv, sink, batch, seq):
    gw = A_GROUP * HEAD_DIM
    k_blk0 = QA_W // HEAD_DIM
    return pl.pallas_call(
        functools.partial(_win_body, seq=seq),
        grid=(batch, A_KV_HEADS),
        in_specs=[pl.BlockSpec(memory_space=pltpu.SMEM),
                  pl.BlockSpec((seq, gw), lambda b, h: (b, h)),
                  pl.BlockSpec((seq, HEAD_DIM), lambda b, h: (b, k_blk0 + h)),
                  pl.BlockSpec((seq, HEAD_DIM), lambda b, h: (b, h))],
        out_specs=pl.BlockSpec((seq, gw), lambda b, h: (b, h)),
        out_shape=jax.ShapeDtypeStruct((batch * seq, QA_W), BF16),
        compiler_params=_params(("arbitrary", "arbitrary"), 32),
        name="window_attention",
    )(sink, qk, qk, vqkv)


NBR_HG = 4
NBR_ROWS = 16


def _nbr_body(q_ref, k_ref, v_ref, bias_ref, o_ref, *, seq):
    rows = seq // GRID_W
    kh = min(NA_KH_MAX, rows)
    strip = kh * GRID_W
    scale = HEAD_DIM ** -0.5 * LOG2E

    def body(it, carry):
        units = []
        for rr in range(NBR_ROWS):
            r = it * NBR_ROWS + rr
            rs = jnp.clip(r - kh // 2, 0, rows - kh)
            q0 = pl.multiple_of(r * GRID_W, GRID_W)
            k0 = pl.multiple_of(rs * GRID_W, GRID_W)
            for h in range(NBR_HG):
                units.append((q0, k0, r - rs, h, slice(h * HEAD_DIM, (h + 1) * HEAD_DIM)))
        ss = [lax.dot_general(q_ref[pl.ds(q0, GRID_W), cols], k_ref[pl.ds(k0, strip), cols], _NT,
                              preferred_element_type=F32) * scale + bias_ref[h, var]
              for q0, k0, var, h, cols in units]
        ps = []
        for s in ss:
            e = jnp.exp2(s - s.max(axis=1, keepdims=True))
            ps.append((e.astype(BF16), 1.0 / e.sum(axis=1, keepdims=True)))
        for (q0, k0, var, h, cols), (e, inv) in zip(units, ps):
            o = jnp.dot(e, v_ref[pl.ds(k0, strip), cols], preferred_element_type=F32) * inv
            o_ref[pl.ds(q0, GRID_W), cols] = o.astype(o_ref.dtype)
        return carry

    lax.fori_loop(0, rows // NBR_ROWS, body, 0)


def neighbourhood_attention(vqkv, bias_tbl, batch, seq):
    gw = NBR_HG * HEAD_DIM
    q0, k0, v0 = KVA_W // gw, (KVA_W + QB_W) // gw, (KVA_W + 2 * QB_W) // gw
    kh = bias_tbl.shape[1]
    return pl.pallas_call(
        functools.partial(_nbr_body, seq=seq),
        grid=(B_HEADS // NBR_HG, batch),
        in_specs=[pl.BlockSpec((seq, gw), lambda g, b: (b, q0 + g)),
                  pl.BlockSpec((seq, gw), lambda g, b: (b, k0 + g)),
                  pl.BlockSpec((seq, gw), lambda g, b: (b, v0 + g)),
                  pl.BlockSpec((NBR_HG, kh, GRID_W, kh * GRID_W), lambda g, b: (g, 0, 0, 0))],
        out_specs=pl.BlockSpec((seq, gw), lambda g, b: (b, g)),
        out_shape=jax.ShapeDtypeStruct((batch * seq, QB_W), BF16),
        compiler_params=_params(("arbitrary", "arbitrary"), 40),
        name="neighbourhood_attention",
    )(vqkv, vqkv, vqkv, bias_tbl)


def _bias_table_body(rpb_ref, o_ref, *, kh):
    h = pl.program_id(0)
    n_dr, n_dc = 2 * NA_KH_MAX - 1, 2 * NA_KW - 1
    c = lax.broadcasted_iota(jnp.int32, (GRID_W, LANES), 0)
    lane = lax.broadcasted_iota(jnp.int32, (GRID_W, LANES), 1)
    kc = lane % GRID_W
    diff = jnp.clip(kc - c + NA_KW - 1, 0, n_dc - 1)
    cs = jnp.clip(c - NA_KW // 2, 0, GRID_W - NA_KW)
    col_ok = (kc >= cs) & (kc < cs + NA_KW)
    slabs = []
    for dr in range(n_dr):
        acc = jnp.zeros((GRID_W, LANES), F32)
        for d in range(n_dc):
            acc = jnp.where(diff == d, rpb_ref[(h * n_dr + dr) * n_dc + d], acc)
        slabs.append(jnp.where(col_ok, acc * LOG2E, NEG))
    left = lane < GRID_W
    for var in range(kh):
        for jp in range(kh * GRID_W // LANES):
            dr0 = 2 * jp - var + NA_KH_MAX - 1
            o_ref[0, var, :, jp * LANES:(jp + 1) * LANES] = jnp.where(left, slabs[dr0], slabs[dr0 + 1])


def _nbr_bias_table(rpb, seq):
    rows = seq // GRID_W
    kh = min(NA_KH_MAX, rows)
    heads = rpb.shape[0]
    assert kh == NA_KH_MAX and 2 * GRID_W == LANES
    return pl.pallas_call(
        functools.partial(_bias_table_body, kh=kh),
        grid=(heads,),
        in_specs=[pl.BlockSpec(memory_space=pltpu.SMEM)],
        out_specs=pl.BlockSpec((1, kh, GRID_W, kh * GRID_W), lambda h: (h, 0, 0, 0)),
        out_shape=jax.ShapeDtypeStruct((heads, kh, GRID_W, kh * GRID_W), F32),
        compiler_params=_params(("arbitrary",), 16),
        name="nbr_bias_table",
    )(rpb.astype(F32).reshape(-1))


def _merge_body(oa_ref, ob_ref, wa_ref, wb_ref, g0_ref, g1_ref, o_ref):
    wa, wb = wa_ref[...].astype(BF16), wb_ref[...].astype(BF16)
    for rows in _row_parts(oa_ref.shape[0]):
        ya = jnp.dot(oa_ref[rows, :], wa, preferred_element_type=F32)
        yb = jnp.dot(ob_ref[rows, :], wb, preferred_element_type=F32)
        o_ref[rows, :] = (g0_ref[rows, :].astype(F32) * ya
                          + g1_ref[rows, :].astype(F32) * yb).astype(o_ref.dtype)


def branch_merge(oa, ob, wa, wb, gates, bm=2048, bn=256):
    m, k = oa.shape
    n = wa.shape[1]
    g1_off = n // bn
    return pl.pallas_call(
        _merge_body,
        grid=(m // bm, n // bn),
        in_specs=[pl.BlockSpec((bm, k), lambda i, j: (i, 0)),
                  pl.BlockSpec((bm, k), lambda i, j: (i, 0)),
                  pl.BlockSpec((k, bn), lambda i, j: (0, j)),
                  pl.BlockSpec((k, bn), lambda i, j: (0, j)),
                  pl.BlockSpec((bm, bn), lambda i, j: (i, j)),
                  pl.BlockSpec((bm, bn), lambda i, j: (i, j + g1_off))],
        out_specs=pl.BlockSpec((bm, bn), lambda i, j: (i, j)),
        out_shape=jax.ShapeDtypeStruct((m, n), BF16),
        compiler_params=_params(("arbitrary", "arbitrary"), 56),
        name="branch_merge",
    )(oa, ob, wa, wb, gates, gates)


def _cast_body(x_ref, o_ref):
    o_ref[...] = x_ref[...].astype(o_ref.dtype)


def cast_bf16(w):
    r, c = w.shape
    return pl.pallas_call(
        _cast_body,
        grid=(1,),
        in_specs=[pl.BlockSpec((r, c), lambda i: (0, 0))],
        out_specs=pl.BlockSpec((r, c), lambda i: (0, 0)),
        out_shape=jax.ShapeDtypeStruct((r, c), BF16),
        compiler_params=_params(("arbitrary",), 40),
        name="cast_bf16",
    )(w)


def _router_probs(hn, wr_ref):
    hi = hn.astype(BF16)
    lo = (hn - hi.astype(F32)).astype(BF16)
    l_hi = jnp.dot(hi, wr_ref[...], preferred_element_type=F32)
    l_lo = jnp.dot(lo, wr_ref[...], preferred_element_type=F32)
    logits = l_hi + pltpu.roll(l_hi, LANES - N_EXPERTS, 1) + l_lo
    lane = lax.broadcasted_iota(jnp.int32, logits.shape, 1)
    logits = jnp.where(lane < N_EXPERTS, logits, NEG)
    return _softmax_rows(logits)


def _xblock_body(x_ref, gc_ref, wq_ref, k_ref, v_ref, wo_ref, gf_ref, wr_ref,
                 x2_ref, h3_ref, aff_ref):
    scale = HEAD_DIM ** -0.5
    half = x_ref.shape[1] // 2
    parts = [slice(p * XB_ROWS, (p + 1) * XB_ROWS) for p in range(x_ref.shape[0] // XB_ROWS)]
    h2s = [_rmsnorm_rows(x_ref[r, :], gc_ref[...]).astype(BF16) for r in parts]
    qs = [jnp.dot(h2, wq_ref[...], preferred_element_type=F32).astype(BF16) for h2 in h2s]
    head_cols = [slice(h * HEAD_DIM, (h + 1) * HEAD_DIM) for h in range(X_HEADS)]
    ss = [[lax.dot_general(q[:, cols], k_ref[:, cols], _NT, preferred_element_type=F32) * scale
           for cols in head_cols] for q in qs]
    ps = [[_softmax_rows(s).astype(BF16) for s in row] for row in ss]
    os = [jnp.concatenate([jnp.dot(p, v_ref[:, cols], preferred_element_type=F32).astype(BF16)
                           for p, cols in zip(row, head_cols)], axis=1) for row in ps]
    for r, o in zip(parts, os):
        x2_ref[r, :] = x_ref[r, :] + jnp.dot(o, wo_ref[...], preferred_element_type=F32)
    for r in parts:
        hn = _rmsnorm_rows(x2_ref[r, :], gf_ref[...])
        packed = pltpu.pack_elementwise([hn[:, :half], hn[:, half:]], packed_dtype=BF16)
        h3_ref[r, :] = pltpu.bitcast(packed, jnp.uint32)
        aff = _router_probs(hn, wr_ref)
        aff_ref[0, :, r] = aff.T[:N_EXPERTS, :]


XB_ROWS = 256


def cross_attention_block(x1, g_cross, wq, kx, vx, wo, g_ffn, w_router, batch, seq, mem_len, bm=512):
    m, d = x1.shape
    nt = seq // bm
    w_hi = w_router.astype(BF16)
    w_lo = (w_router - w_hi.astype(F32)).astype(BF16)
    wr = jnp.concatenate([w_hi, w_lo, jnp.zeros((d, LANES - 2 * N_EXPERTS), BF16)], axis=1)
    const = lambda i: (0, 0)
    once = pl.Buffered(1)
    return pl.pallas_call(
        _xblock_body,
        grid=(m // bm,),
        in_specs=[pl.BlockSpec((bm, d), lambda i: (i, 0)),
                  pl.BlockSpec((1, d), const),
                  pl.BlockSpec((d, X_W), const, pipeline_mode=once),
                  pl.BlockSpec((mem_len, X_W), lambda i: (i // nt, 0)),
                  pl.BlockSpec((mem_len, X_W), lambda i: (i // nt, 0)),
                  pl.BlockSpec((X_W, d), const, pipeline_mode=once),
                  pl.BlockSpec((1, d), const),
                  pl.BlockSpec((d, LANES), const, pipeline_mode=once)],
        out_specs=[pl.BlockSpec((bm, d), lambda i: (i, 0)),
                   pl.BlockSpec((bm, d // 2), lambda i: (i, 0)),
                   pl.BlockSpec((1, N_EXPERTS, bm), lambda i: (i // nt, 0, i % nt))],
        out_shape=[jax.ShapeDtypeStruct((m, d), F32),
                   jax.ShapeDtypeStruct((m, d // 2), jnp.uint32),
                   jax.ShapeDtypeStruct((batch, N_EXPERTS, seq), F32)],
        compiler_params=_params(("arbitrary",), 58),
        name="cross_attention_block",
    )(x1, g_cross.reshape(1, d), cast_bf16(wq), kx, vx, cast_bf16(wo), g_ffn.reshape(1, d), wr)


CUM_CHUNK = 256


def _excl_cumsum_lanes(x01, tri):
    n = x01.shape[1]
    carry = jnp.zeros((x01.shape[0], 1), F32)
    out = []
    for c in range(n // CUM_CHUNK):
        xc = x01[:, c * CUM_CHUNK:(c + 1) * CUM_CHUNK]
        out.append(jnp.dot(xc.astype(BF16), tri, preferred_element_type=F32) + carry)
        carry = carry + xc.sum(axis=1, keepdims=True)
    return jnp.concatenate(out, axis=1)


COMBINE_TS = 256
COMBINE_W = 64


def _topk_body(aff_ref, slot_ref, first_ref, *, cap):
    a = aff_ref[0]
    n_exp, s = a.shape
    capf = jnp.float32(cap)
    lane_s = lax.broadcasted_iota(jnp.int32, (n_exp, s), 1)

    def count(mask):
        return jnp.where(mask, 1.0, 0.0).sum(axis=1, keepdims=True)

    bits = pltpu.bitcast(a, jnp.int32)

    def search(i, t):
        cand = t | jnp.left_shift(jnp.int32(1), 30 - i)
        return jnp.where(count(bits >= cand) >= capf, cand, t)

    t = lax.fori_loop(0, 31, search, jnp.zeros((n_exp, 1), jnp.int32))
    at = jnp.where(bits == t, lane_s, s).astype(F32).min(axis=1, keepdims=True).astype(jnp.int32)
    pivot = jnp.where(lane_s == at, a, 0.0).sum(axis=1, keepdims=True)

    def stats(p):
        return p, count(a > p), count(a >= p)

    def wrong(state):
        _, n_gt, n_ge = state
        return jnp.where(jnp.logical_or(n_gt >= capf, n_ge < capf), 1.0, 0.0).sum() > 0.0

    def step(state):
        p, n_gt, n_ge = state
        up = jnp.where(a > p, a, jnp.inf).min(axis=1, keepdims=True)
        down = jnp.where(a < p, a, -jnp.inf).max(axis=1, keepdims=True)
        return stats(jnp.where(n_gt >= capf, up, jnp.where(n_ge < capf, down, p)))

    pivot, n_gt, _ = lax.while_loop(wrong, step, stats(pivot))
    ri = lax.broadcasted_iota(jnp.int32, (CUM_CHUNK, CUM_CHUNK), 0)
    cj = lax.broadcasted_iota(jnp.int32, (CUM_CHUNK, CUM_CHUNK), 1)
    tri = jnp.where(ri < cj, 1.0, 0.0).astype(BF16)
    eq = jnp.where(a == pivot, 1.0, 0.0)
    sel = jnp.where(a > pivot, 1.0, 0.0) + jnp.where(_excl_cumsum_lanes(eq, tri) < capf - n_gt, eq, 0.0)
    pos = _excl_cumsum_lanes(sel, tri)
    slot_ref[0] = jnp.where(sel > 0.5, pos, -1.0).astype(jnp.int32)
    lane_k = lax.broadcasted_iota(jnp.int32, (n_exp, LANES), 1)
    first = jnp.zeros((n_exp, LANES), F32)
    for k in range(s // COMBINE_TS):
        at_k = jnp.where(lane_s == k * COMBINE_TS, pos, 0.0).sum(axis=1, keepdims=True)
        first = jnp.where(lane_k == k, at_k, first)
    first_ref[0] = first.astype(jnp.int32)


def expert_topk(aff_t, cap):
    b, e, s = aff_t.shape
    return pl.pallas_call(
        functools.partial(_topk_body, cap=cap),
        grid=(b,),
        in_specs=[pl.BlockSpec((1, e, s), lambda i: (i, 0, 0))],
        out_specs=[pl.BlockSpec((1, e, s), lambda i: (i, 0, 0)),
                   pl.BlockSpec((1, e, LANES), lambda i: (i, 0, 0))],
        out_shape=[jax.ShapeDtypeStruct((b, e, s), jnp.int32),
                   jax.ShapeDtypeStruct((b, e, LANES), jnp.int32)],
        compiler_params=_params(("arbitrary",), 32),
        name="expert_topk",
    )(aff_t)


def _slot_index_body(slot_ref, aff_ref, idx_ref, val_ref, *, cap):
    b = pl.program_id(0)
    n_exp, s = slot_ref.shape[1], slot_ref.shape[2]
    ci = lax.broadcasted_iota(jnp.int32, (cap, s), 0)
    tok = lax.broadcasted_iota(jnp.int32, (cap, s), 1).astype(F32)
    base = (b * s).astype(F32)
    for e in range(n_exp):
        hit = slot_ref[0, e:e + 1, :] == ci
        idx = jnp.where(hit, tok, 0.0).sum(axis=1, keepdims=True) + base
        idx_ref[e] = idx.astype(jnp.int32)
        val_ref[e] = jnp.where(hit, aff_ref[0, e:e + 1, :], 0.0).sum(axis=1, keepdims=True)


def slot_index(slot, aff_t, cap):
    b, e, s = slot.shape
    return pl.pallas_call(
        functools.partial(_slot_index_body, cap=cap),
        grid=(b,),
        in_specs=[pl.BlockSpec((1, e, s), lambda bi: (bi, 0, 0)),
                  pl.BlockSpec((1, e, s), lambda bi: (bi, 0, 0))],
        out_specs=[pl.BlockSpec((e, cap, 1), lambda bi: (0, bi, 0)),
                   pl.BlockSpec((e, cap, 1), lambda bi: (0, bi, 0))],
        out_shape=[jax.ShapeDtypeStruct((e, b * cap, 1), jnp.int32),
                   jax.ShapeDtypeStruct((e, b * cap, 1), F32)],
        compiler_params=_params(("arbitrary",), 32),
        name="slot_index",
    )(slot, aff_t)


UNPACK_ROWS = 128


def _row_copy(idx_ref, hp_ref, gbuf, sem, expert, rows, row):
    tok = idx_ref[expert * rows + row]
    return pltpu.make_async_copy(hp_ref.at[pl.ds(tok, 1)], gbuf.at[pl.ds(row, 1)], sem)


def _expert_up_body(idx_ref, hp_ref, wg_ref, wu_ref, o_ref, gbuf, xbf, sem, *, rows, per_step):
    e, f = pl.program_id(0), pl.program_id(1)
    n_e, n_f = pl.num_programs(0), pl.num_programs(1)
    half = gbuf.shape[1]

    def wait_all_rows():
        pltpu.make_async_copy(hp_ref.at[pl.ds(0, rows)], gbuf, sem).wait()

    @pl.when(jnp.logical_and(e == 0, f == 0))
    def _():
        def body(r, carry):
            _row_copy(idx_ref, hp_ref, gbuf, sem, 0, rows, r).start()
            return carry
        lax.fori_loop(0, rows, body, 0)

    @pl.when(f == 0)
    def _():
        wait_all_rows()

        def unpack(k, carry):
            r = pl.multiple_of(k * UNPACK_ROWS, UNPACK_ROWS)
            w = gbuf[pl.ds(r, UNPACK_ROWS), :]
            lo = pltpu.unpack_elementwise(w, index=0, packed_dtype=BF16, unpacked_dtype=F32)
            hi = pltpu.unpack_elementwise(w, index=1, packed_dtype=BF16, unpacked_dtype=F32)
            xbf[pl.ds(r, UNPACK_ROWS), :half] = lo.astype(BF16)
            xbf[pl.ds(r, UNPACK_ROWS), half:] = hi.astype(BF16)
            return carry
        lax.fori_loop(0, rows // UNPACK_ROWS, unpack, 0)

    nxt = jnp.minimum(e + 1, n_e - 1)
    for r in range(per_step):
        _row_copy(idx_ref, hp_ref, gbuf, sem, nxt, rows, f * per_step + r).start()

    wg, wu = wg_ref[0].astype(BF16), wu_ref[0].astype(BF16)
    for part in _row_parts(rows):
        a = jnp.dot(xbf[part, :], wg, preferred_element_type=F32)
        u = jnp.dot(xbf[part, :], wu, preferred_element_type=F32)
        o_ref[0, part, :] = (a * _sigmoid(a) * u).astype(o_ref.dtype)

    @pl.when(jnp.logical_and(e == n_e - 1, f == n_f - 1))
    def _():
        wait_all_rows()


def expert_up(idx, hp, w_gate, w_up, rows, tf=256):
    n_e, d, f = w_gate.shape
    assert hp.shape[1] * 2 == d and rows % (f // tf) == 0
    grid_spec = pltpu.PrefetchScalarGridSpec(
        num_scalar_prefetch=1,
        grid=(n_e, f // tf),
        in_specs=[pl.BlockSpec(memory_space=pl.ANY),
                  pl.BlockSpec((1, d, tf), lambda ei, fi, idx_ref: (ei, 0, fi)),
                  pl.BlockSpec((1, d, tf), lambda ei, fi, idx_ref: (ei, 0, fi))],
        out_specs=pl.BlockSpec((1, rows, tf), lambda ei, fi, idx_ref: (ei, 0, fi)),
        scratch_shapes=[pltpu.VMEM((rows, d // 2), jnp.uint32),
                        pltpu.VMEM((rows, d), BF16),
                        pltpu.SemaphoreType.DMA(())],
    )
    return pl.pallas_call(
        functools.partial(_expert_up_body, rows=rows, per_step=rows // (f // tf)),
        grid_spec=grid_spec,
        out_shape=jax.ShapeDtypeStruct((n_e, rows, f), BF16),
        compiler_params=_params(("arbitrary", "arbitrary"), 56),
        name="expert_up",
    )(idx, hp, w_gate, w_up)


def _expert_down_body(h_ref, wd_ref, val_ref, o_ref):
    wd = wd_ref[0].astype(BF16)
    for part in _row_parts(h_ref.shape[1]):
        y = jnp.dot(h_ref[0, part, :], wd, preferred_element_type=F32)
        o_ref[0, part, :] = (y * val_ref[0, part, :]).astype(o_ref.dtype)


def expert_down(hmid, w_down, valc, tn=1024):
    e, rows, f = hmid.shape
    d = w_down.shape[-1]
    return pl.pallas_call(
        _expert_down_body,
        grid=(e, d // tn),
        in_specs=[pl.BlockSpec((1, rows, f), lambda ei, ni: (ei, 0, 0)),
                  pl.BlockSpec((1, f, tn), lambda ei, ni: (ei, 0, ni)),
                  pl.BlockSpec((1, rows, 1), lambda ei, ni: (ei, 0, 0))],
        out_specs=pl.BlockSpec((1, rows, tn), lambda ei, ni: (ei, 0, ni)),
        out_shape=jax.ShapeDtypeStruct((e, rows, d), BF16),
        compiler_params=_params(("arbitrary", "arbitrary"), 48),
        name="expert_down",
    )(hmid, w_down, valc)


COMBINE_TN = 512
ROW_ALIGN_BF16 = 16


def _combine_windows(first, cap, n_rows):
    b, e, nt = first.shape
    row0 = jnp.arange(b, dtype=jnp.int32)[:, None, None] * cap
    lo = first + row0
    hi = jnp.concatenate([first[:, :, 1:], jnp.full((b, e, 1), cap, jnp.int32)], axis=2) + row0
    start = jnp.minimum(lo // ROW_ALIGN_BF16 * ROW_ALIGN_BF16, n_rows - COMBINE_W)
    fast = jnp.all(hi - start <= COMBINE_W, axis=1)
    return start.transpose(0, 2, 1).reshape(-1), fast.astype(jnp.int32).reshape(-1)


def _combine_body(win_ref, fast_ref, slot_ref, y_hbm, x_ref, g_ref, o_ref,
                  ywin, oht, ybuf, ohs, wsem, ssem, *, cap, n_tiles):
    i = pl.program_id(0)
    n_exp = slot_ref.shape[1]
    ts, d = o_ref.shape
    w = COMBINE_W
    row0 = (i // n_tiles) * cap
    buf = i % 2

    def window_copy(step, e, b):
        start = pl.multiple_of(win_ref[step * n_exp + e], ROW_ALIGN_BF16)
        return pltpu.make_async_copy(y_hbm.at[e, pl.ds(start, w)], ywin.at[b, pl.ds(e * w, w)],
                                     wsem.at[b])

    @pl.when(i == 0)
    def _():
        for e in range(n_exp):
            window_copy(0, e, 0).start()

    @pl.when(i + 1 < pl.num_programs(0))
    def _():
        for e in range(n_exp):
            window_copy(i + 1, e, 1 - buf).start()

    pltpu.make_async_copy(y_hbm.at[0, pl.ds(0, n_exp * w)], ywin.at[buf], wsem.at[buf]).wait()

    @pl.when(fast_ref[i] == 1)
    def _():
        row = lax.broadcasted_iota(jnp.int32, (2 * w, ts), 0)
        upper = row >= w
        j = row % w
        for p in range(n_exp // 2):
            rel_a = slot_ref[0, 2 * p:2 * p + 1, :] + (row0 - win_ref[i * n_exp + 2 * p])
            rel_b = slot_ref[0, 2 * p + 1:2 * p + 2, :] + (row0 - win_ref[i * n_exp + 2 * p + 1])
            hit = jnp.where(upper, rel_b, rel_a) == j
            oht[:, p * 2 * w:(p + 1) * 2 * w] = jnp.where(hit, 1.0, 0.0).T.astype(BF16)
        for c in range(d // COMBINE_TN):
            cols = slice(c * COMBINE_TN, (c + 1) * COMBINE_TN)
            o_ref[:, cols] = x_ref[:, cols] + jnp.dot(oht[...], ywin[buf, :, cols],
                                                      preferred_element_type=F32)

    @pl.when(fast_ref[i] == 0)
    def _():
        o_ref[...] = x_ref[...]
        ci = lax.broadcasted_iota(jnp.int32, (cap, ts), 0)

        def body(e, carry):
            cp = pltpu.make_async_copy(y_hbm.at[e, pl.ds(pl.multiple_of(row0, cap), cap)], ybuf, ssem)
            cp.start()
            cp.wait()
            srow = slot_ref[0, pl.ds(e, 1), :]
            ohs[...] = jnp.where(srow == ci, 1.0, 0.0).T.astype(BF16)
            for c in range(d // COMBINE_TN):
                cols = slice(c * COMBINE_TN, (c + 1) * COMBINE_TN)
                o_ref[:, cols] += jnp.dot(ohs[...], ybuf[:, cols], preferred_element_type=F32)
            return carry
        lax.fori_loop(0, n_exp, body, 0)

    o_ref[...] = _rmsnorm_rows(o_ref[...], g_ref[...])


def expert_combine(slot, first, y, x2d, g, cap):
    b, e, s = slot.shape
    d = x2d.shape[-1]
    ts, w = COMBINE_TS, COMBINE_W
    nt = s // ts
    assert e % 2 == 0 and y.shape[1] >= e * w and cap % ROW_ALIGN_BF16 == 0
    win, fast = _combine_windows(first[:, :, :nt], cap, y.shape[1])
    grid_spec = pltpu.PrefetchScalarGridSpec(
        num_scalar_prefetch=2,
        grid=(b * nt,),
        in_specs=[pl.BlockSpec((1, e, ts), lambda i, win_ref, fast_ref: (i // nt, 0, i % nt)),
                  pl.BlockSpec(memory_space=pl.ANY),
                  pl.BlockSpec((ts, d), lambda i, win_ref, fast_ref: (i, 0)),
                  pl.BlockSpec((1, d), lambda i, win_ref, fast_ref: (0, 0))],
        out_specs=pl.BlockSpec((ts, d), lambda i, win_ref, fast_ref: (i, 0)),
        scratch_shapes=[pltpu.VMEM((2, e * w, d), BF16),
                        pltpu.VMEM((ts, e * w), BF16),
                        pltpu.VMEM((cap, d), BF16),
                        pltpu.VMEM((ts, cap), BF16),
                        pltpu.SemaphoreType.DMA((2,)),
                        pltpu.SemaphoreType.DMA(())],
    )
    return pl.pallas_call(
        functools.partial(_combine_body, cap=cap, n_tiles=nt),
        grid_spec=grid_spec,
        out_shape=jax.ShapeDtypeStruct(x2d.shape, F32),
        compiler_params=_params(("arbitrary",), 48),
        name="expert_combine",
    )(win, fast, slot, y, x2d, g.reshape(1, d))


def _rotary_tables(seq):
    half = ROT_DIM // 2
    inv = ROPE_THETA ** (-jnp.arange(half, dtype=F32) * 2.0 / ROT_DIM)
    ang = jnp.arange(seq).astype(F32)[:, None] * inv[None, :]
    cos, sin = jnp.cos(ang), jnp.sin(ang)
    ones = jnp.ones((seq, HEAD_DIM - ROT_DIM), F32)
    zeros = jnp.zeros((seq, HEAD_DIM - ROT_DIM), F32)
    zh = jnp.zeros((seq, half), F32)
    c = jnp.concatenate([cos, cos, ones], axis=1)
    s1 = jnp.concatenate([-sin, zh, zeros], axis=1)
    s2 = jnp.concatenate([zh, sin, zeros], axis=1)
    return c, s1, s2


def kernel(x, mem, norm_mix, w_in, b_gate, sink, rpb, w_branch_a, w_branch_b, w_out,
           norm_cross, norm_mem, wq_x, wk_x, wv_x, wo_x, norm_ffn, w_router,
           w_gate, w_up, w_down, norm_final):
    batch, seq, d = x.shape
    mem_len = mem.shape[1]
    m = batch * seq
    assert norm_mix.shape[0] == 1, "final RMSNorm is fused into the single layer's last kernel"
    cap = EC_CAPACITY * seq // N_EXPERTS
    bm, bn = 2048, 256
    sb = seq // bm
    x0 = x.reshape(m, d)

    h = rmsnorm(x0, norm_mix[0], BF16)
    rot = _rotary_tables(seq)
    rot_specs = [pl.BlockSpec((bm, HEAD_DIM), lambda i, j: (i % sb, 0))] * 3
    qk = matmul_rows(h, w_in[0], col_off=0, n_cols=QA_W + KVA_W, bm=bm, bn=bn, out_dtype=BF16,
                     epilogue=_ep_rotary, extras=rot, extra_specs=rot_specs, row_split=2 * ROW_SPLIT,
                     name="in_proj_rotary")
    vqkv = matmul_rows(h, w_in[0], col_off=QA_W + KVA_W, n_cols=KVA_W + 3 * QB_W, bm=bm, bn=bn,
                       out_dtype=BF16, name="in_proj_plain")
    g_off = QA_W + 2 * KVA_W + 3 * QB_W
    gates = matmul_rows(h, w_in[0], col_off=g_off, n_cols=2 * d, bm=bm, bn=bn, out_dtype=BF16,
                        epilogue=_ep_sigmoid, extras=(b_gate[0].reshape(1, 2 * d),),
                        extra_specs=[pl.BlockSpec((1, bn), lambda i, j: (0, j))], name="in_proj_gates")
    oa = window_attention(qk, vqkv, sink[0], batch, seq)
    ob = neighbourhood_attention(vqkv, _nbr_bias_table(rpb[0], seq), batch, seq)
    merged = branch_merge(oa, ob, w_branch_a[0], w_branch_b[0], gates, bm=bm, bn=bn)
    res_spec = [pl.BlockSpec((bm, bn), lambda i, j: (i, j))]
    x1 = matmul_rows(merged, w_out[0], col_off=0, n_cols=d, bm=bm, bn=bn, out_dtype=F32,
                     epilogue=_ep_residual, extras=(x0,), extra_specs=res_spec, name="out_proj")

    mem_rows = batch * mem_len
    mn = rmsnorm(mem.reshape(mem_rows, d), norm_mem[0], BF16)
    kx = matmul_rows(mn, wk_x[0], col_off=0, n_cols=X_W, bm=mem_rows, bn=bn, out_dtype=BF16, name="xattn_k")
    vx = matmul_rows(mn, wv_x[0], col_off=0, n_cols=X_W, bm=mem_rows, bn=bn, out_dtype=BF16, name="xattn_v")
    x2, h3p, aff_t = cross_attention_block(
        x1, norm_cross[0], wq_x[0], kx, vx, wo_x[0], norm_ffn[0], w_router[0], batch, seq, mem_len)

    slot, first = expert_topk(aff_t, cap)
    idx, valc = slot_index(slot, aff_t, cap)
    hmid = expert_up(idx.reshape(-1), h3p, w_gate[0], w_up[0], batch * cap)
    y = expert_down(hmid, w_down[0], valc)
    out = expert_combine(slot, first, y, x2, norm_final, cap)
    return out.reshape(batch, seq, d)
```

```python
import functools

import jax
import jax.numpy as jnp
from jax import lax
from jax.experimental import pallas as pl
from jax.experimental.pallas import tpu as pltpu

F32 = jnp.float32
BF16 = jnp.bfloat16

HEAD_DIM = 128
A_HEADS = 16
A_KV_HEADS = 4
A_GROUP = A_HEADS // A_KV_HEADS
WINDOW = 128
A_BLOCK = 128
ROT_DIM = HEAD_DIM // 4
ROPE_THETA = 500000.0
B_HEADS = 16
GRID_W = 64
NA_KH_MAX = 8
NA_KW = 16
X_HEADS = 4
N_EXPERTS = 16
EC_CAPACITY = 2
EPS = 1e-6
NEG = -1e30
LOG2E = 1.4426950408889634
LANES = 128
MIB = 1024 * 1024

QA_W = A_HEADS * HEAD_DIM
KVA_W = A_KV_HEADS * HEAD_DIM
QB_W = B_HEADS * HEAD_DIM
X_W = X_HEADS * HEAD_DIM

_NT = (((1,), (1,)), ((), ()))
_TN = (((0,), (0,)), ((), ()))


def _params(semantics, vmem_mib):
    return pltpu.CompilerParams(dimension_semantics=semantics,
                                vmem_limit_bytes=vmem_mib * MIB)


def _rmsnorm_rows(x, g):
    ms = jnp.mean(x * x, axis=-1, keepdims=True)
    return x * lax.rsqrt(ms + EPS) * g


def _rmsnorm_body(x_ref, g_ref, o_ref):
    o_ref[...] = _rmsnorm_rows(x_ref[...], g_ref[...]).astype(o_ref.dtype)


def rmsnorm(x2d, g, out_dtype, bm=512):
    m, d = x2d.shape
    return pl.pallas_call(
        _rmsnorm_body,
        grid=(m // bm,),
        in_specs=[pl.BlockSpec((bm, d), lambda i: (i, 0)),
                  pl.BlockSpec((1, d), lambda i: (0, 0))],
        out_specs=pl.BlockSpec((bm, d), lambda i: (i, 0)),
        out_shape=jax.ShapeDtypeStruct((m, d), out_dtype),
        compiler_params=_params(("arbitrary",), 40),
        name="rmsnorm",
    )(x2d, g.reshape(1, d))


def _sigmoid(x):
    return 0.5 * jnp.tanh(0.5 * x) + 0.5


ROW_SPLIT = 4


def _row_parts(n_rows, split=ROW_SPLIT):
    step = n_rows // split
    return [slice(p * step, (p + 1) * step) for p in range(split)]


def _ep_store(acc, rows, o_ref):
    o_ref[rows, :] = acc.astype(o_ref.dtype)


def _ep_residual(acc, rows, o_ref, r_ref):
    o_ref[rows, :] = (r_ref[rows, :] + acc).astype(o_ref.dtype)


def _ep_sigmoid(acc, rows, o_ref, b_ref):
    o_ref[rows, :] = _sigmoid((acc + b_ref[...]).astype(o_ref.dtype))


def _ep_rotary(acc, rows, o_ref, c_ref, s1_ref, s2_ref):
    c, s1, s2 = c_ref[rows, :], s1_ref[rows, :], s2_ref[rows, :]
    half = ROT_DIM // 2
    for h in range(acc.shape[1] // HEAD_DIM):
        a = acc[:, h * HEAD_DIM:(h + 1) * HEAD_DIM]
        r = a * c + pltpu.roll(a, HEAD_DIM - half, 1) * s1 + pltpu.roll(a, half, 1) * s2
        o_ref[rows, h * HEAD_DIM:(h + 1) * HEAD_DIM] = r.astype(o_ref.dtype)


def _mm_rows_body(*refs, n_extra, epilogue, row_split):
    a_ref, w_ref = refs[0], refs[1]
    extra = refs[2:2 + n_extra]
    o_ref = refs[2 + n_extra]
    wb = w_ref[...].astype(BF16)
    for rows in _row_parts(a_ref.shape[0], row_split):
        acc = jnp.dot(a_ref[rows, :], wb, preferred_element_type=F32)
        epilogue(acc, rows, o_ref, *extra)


def matmul_rows(a, w, *, col_off, n_cols, bm, bn, out_dtype, epilogue=_ep_store,
                extras=(), extra_specs=(), row_split=ROW_SPLIT, vmem_mib=56, name="matmul_rows"):
    m, k = a.shape
    off = col_off // bn
    assert col_off % bn == 0 and n_cols % bn == 0 and m % bm == 0
    body = functools.partial(_mm_rows_body, n_extra=len(extras), epilogue=epilogue, row_split=row_split)
    return pl.pallas_call(
        body,
        grid=(m // bm, n_cols // bn),
        in_specs=[pl.BlockSpec((bm, k), lambda i, j: (i, 0)),
                  pl.BlockSpec((k, bn), lambda i, j: (0, j + off))] + list(extra_specs),
        out_specs=pl.BlockSpec((bm, bn), lambda i, j: (i, j)),
        out_shape=jax.ShapeDtypeStruct((m, n_cols), out_dtype),
        compiler_params=_params(("arbitrary", "arbitrary"), vmem_mib),
        name=name,
    )(a, w, *extras)


def _softmax_rows(s):
    e = jnp.exp(s - s.max(axis=1, keepdims=True))
    return e * (1.0 / e.sum(axis=1, keepdims=True))


WIN_UNROLL = 8


def _win_body(sink_ref, q_ref, k_ref, v_ref, o_ref, *, seq):
    kv = pl.program_id(1)
    nb = seq // A_BLOCK
    scale = HEAD_DIM ** -0.5 * LOG2E
    rows = A_GROUP * A_BLOCK
    qi = lax.broadcasted_iota(jnp.int32, (rows, A_BLOCK), 0) % A_BLOCK
    ci = lax.broadcasted_iota(jnp.int32, (rows, A_BLOCK), 1)
    sink_b = jnp.concatenate(
        [jnp.full((A_BLOCK, HEAD_DIM), sink_ref[kv * A_GROUP + g] * LOG2E, F32)
         for g in range(A_GROUP)], axis=0)

    def scores(n):
        r0 = pl.multiple_of(n * A_BLOCK, A_BLOCK)
        rp = pl.multiple_of(jnp.maximum(n - 1, 0) * A_BLOCK, A_BLOCK)
        rn = pl.multiple_of(jnp.minimum(n + 1, nb - 1) * A_BLOCK, A_BLOCK)
        off_p = jnp.where(n > 0, 0, 2 * A_BLOCK)
        off_n = jnp.where(n < nb - 1, 0, 2 * A_BLOCK)
        q = jnp.concatenate(
            [q_ref[pl.ds(r0, A_BLOCK), g * HEAD_DIM:(g + 1) * HEAD_DIM] for g in range(A_GROUP)],
            axis=0)
        sp = lax.dot_general(q, k_ref[pl.ds(rp, A_BLOCK), :], _NT, preferred_element_type=F32) * scale
        sc = lax.dot_general(q, k_ref[pl.ds(r0, A_BLOCK), :], _NT, preferred_element_type=F32) * scale
        sn = lax.dot_general(q, k_ref[pl.ds(rn, A_BLOCK), :], _NT, preferred_element_type=F32) * scale
        sp = jnp.where(ci >= qi + off_p, sp, NEG)
        sn = jnp.where(ci <= qi - off_n, sn, NEG)
        return (rp, r0, rn), (sp, sc, sn)

    def exps(parts):
        m = jnp.maximum(jnp.maximum(parts[0], parts[1]), parts[2]).max(axis=1, keepdims=True)
        m = jnp.maximum(jnp.broadcast_to(m, sink_b.shape), sink_b)
        es = [jnp.exp2(p - m) for p in parts]
        den = (es[0] + es[1] + es[2]).sum(axis=1, keepdims=True)
        den = jnp.broadcast_to(den, sink_b.shape) + jnp.exp2(sink_b - m)
        return [e.astype(BF16) for e in es], 1.0 / den

    def body(it, carry):
        blocks = [scores(it * WIN_UNROLL + u) for u in range(WIN_UNROLL)]
        probs = [exps(parts) for _, parts in blocks]
        for (rows_kv, _), (es, inv) in zip(blocks, probs):
            o = jnp.dot(es[0], v_ref[pl.ds(rows_kv[0], A_BLOCK), :], preferred_element_type=F32)
            o = o + jnp.dot(es[1], v_ref[pl.ds(rows_kv[1], A_BLOCK), :], preferred_element_type=F32)
            o = o + jnp.dot(es[2], v_ref[pl.ds(rows_kv[2], A_BLOCK), :], preferred_element_type=F32)
            o = o * inv
            for g in range(A_GROUP):
                o_ref[pl.ds(rows_kv[1], A_BLOCK), g * HEAD_DIM:(g + 1) * HEAD_DIM] = (
                    o[g * A_BLOCK:(g + 1) * A_BLOCK].astype(o_ref.dtype))
        return carry

    lax.fori_loop(0, nb // WIN_UNROLL, body, 0)


def window_attention(qk, vqkv, sink, batch, seq):
    gw = A_GROUP * HEAD_DIM
    k_blk0 = QA_W // HEAD_DIM
    return pl.pallas_call(
        functools.partial(_win_body, seq=seq),
        grid=(batch, A_KV_HEADS),
        in_specs=[pl.BlockSpec(memory_space=pltpu.SMEM),
                  pl.BlockSpec((seq, gw), lambda b, h: (b, h)),
                  pl.BlockSpec((seq, HEAD_DIM), lambda b, h: (b, k_blk0 + h)),
                  pl.BlockSpec((seq, HEAD_DIM), lambda b, h: (b, h))],
        out_specs=pl.BlockSpec((seq, gw), lambda b, h: (b, h)),
        out_shape=jax.ShapeDtypeStruct((batch * seq, QA_W), BF16),
        compiler_params=_params(("arbitrary", "arbitrary"), 32),
        name="window_attention",
    )(sink, qk, qk, vqkv)


NBR_HG = 4
NBR_ROWS = 16


def _nbr_body(q_ref, k_ref, v_ref, bias_ref, o_ref, *, seq):
    rows = seq // GRID_W
    kh = min(NA_KH_MAX, rows)
    strip = kh * GRID_W
    scale = HEAD_DIM ** -0.5 * LOG2E

    def body(it, carry):
        units = []
        for rr in range(NBR_ROWS):
            r = it * NBR_ROWS + rr
            rs = jnp.clip(r - kh // 2, 0, rows - kh)
            q0 = pl.multiple_of(r * GRID_W, GRID_W)
            k0 = pl.multiple_of(rs * GRID_W, GRID_W)
            for h in range(NBR_HG):
                units.append((q0, k0, r - rs, h, slice(h * HEAD_DIM, (h + 1) * HEAD_DIM)))
        ss = [lax.dot_general(q_ref[pl.ds(q0, GRID_W), cols], k_ref[pl.ds(k0, strip), cols], _NT,
                              preferred_element_type=F32) * scale + bias_ref[h, var]
              for q0, k0, var, h, cols in units]
        ps = []
        for s in ss:
            e = jnp.exp2(s - s.max(axis=1, keepdims=True))
            ps.append((e.astype(BF16), 1.0 / e.sum(axis=1, keepdims=True)))
        for (q0, k0, var, h, cols), (e, inv) in zip(units, ps):
            o = jnp.dot(e, v_ref[pl.ds(k0, strip), cols], preferred_element_type=F32) * inv
            o_ref[pl.ds(q0, GRID_W), cols] = o.astype(o_ref.dtype)
        return carry

    lax.fori_loop(0, rows // NBR_ROWS, body, 0)


def neighbourhood_attention(vqkv, bias_tbl, batch, seq):
    gw = NBR_HG * HEAD_DIM
    q0, k0, v0 = KVA_W // gw, (KVA_W + QB_W) // gw, (KVA_W + 2 * QB_W) // gw
    kh = bias_tbl.shape[1]
    return pl.pallas_call(
        functools.partial(_nbr_body, seq=seq),
        grid=(B_HEADS // NBR_HG, batch),
        in_specs=[pl.BlockSpec((seq, gw), lambda g, b: (b, q0 + g)),
                  pl.BlockSpec((seq, gw), lambda g, b: (b, k0 + g)),
                  pl.BlockSpec((seq, gw), lambda g, b: (b, v0 + g)),
                  pl.BlockSpec((NBR_HG, kh, GRID_W, kh * GRID_W), lambda g, b: (g, 0, 0, 0))],
        out_specs=pl.BlockSpec((seq, gw), lambda g, b: (b, g)),
        out_shape=jax.ShapeDtypeStruct((batch * seq, QB_W), BF16),
        compiler_params=_params(("arbitrary", "arbitrary"), 40),
        name="neighbourhood_attention",
    )(vqkv, vqkv, vqkv, bias_tbl)


def _bias_table_body(rpb_ref, o_ref, *, kh):
    h = pl.program_id(0)
    n_dr, n_dc = 2 * NA_KH_MAX - 1, 2 * NA_KW - 1
    c = lax.broadcasted_iota(jnp.int32, (GRID_W, LANES), 0)
    lane = lax.broadcasted_iota(jnp.int32, (GRID_W, LANES), 1)
    kc = lane % GRID_W
    diff = jnp.clip(kc - c + NA_KW - 1, 0, n_dc - 1)
    cs = jnp.clip(c - NA_KW // 2, 0, GRID_W - NA_KW)
    col_ok = (kc >= cs) & (kc < cs + NA_KW)
    slabs = []
    for dr in range(n_dr):
        acc = jnp.zeros((GRID_W, LANES), F32)
        for d in range(n_dc):
            acc = jnp.where(diff == d, rpb_ref[(h * n_dr + dr) * n_dc + d], acc)
        slabs.append(jnp.where(col_ok, acc * LOG2E, NEG))
    left = lane < GRID_W
    for var in range(kh):
        for jp in range(kh * GRID_W // LANES):
            dr0 = 2 * jp - var + NA_KH_MAX - 1
            o_ref[0, var, :, jp * LANES:(jp + 1) * LANES] = jnp.where(left, slabs[dr0], slabs[dr0 + 1])


def _nbr_bias_table(rpb, seq):
    rows = seq // GRID_W
    kh = min(NA_KH_MAX, rows)
    heads = rpb.shape[0]
    assert kh == NA_KH_MAX and 2 * GRID_W == LANES
    return pl.pallas_call(
        functools.partial(_bias_table_body, kh=kh),
        grid=(heads,),
        in_specs=[pl.BlockSpec(memory_space=pltpu.SMEM)],
        out_specs=pl.BlockSpec((1, kh, GRID_W, kh * GRID_W), lambda h: (h, 0, 0, 0)),
        out_shape=jax.ShapeDtypeStruct((heads, kh, GRID_W, kh * GRID_W), F32),
        compiler_params=_params(("arbitrary",), 16),
        name="nbr_bias_table",
    )(rpb.astype(F32).reshape(-1))


def _merge_body(oa_ref, ob_ref, wa_ref, wb_ref, g0_ref, g1_ref, o_ref):
    wa, wb = wa_ref[...].astype(BF16), wb_ref[...].astype(BF16)
    for rows in _row_parts(oa_ref.shape[0]):
        ya = jnp.dot(oa_ref[rows, :], wa, preferred_element_type=F32)
        yb = jnp.dot(ob_ref[rows, :], wb, preferred_element_type=F32)
        o_ref[rows, :] = (g0_ref[rows, :].astype(F32) * ya
                          + g1_ref[rows, :].astype(F32) * yb).astype(o_ref.dtype)


def branch_merge(oa, ob, wa, wb, gates, bm=2048, bn=256):
    m, k = oa.shape
    n = wa.shape[1]
    g1_off = n // bn
    return pl.pallas_call(
        _merge_body,
        grid=(m // bm, n // bn),
        in_specs=[pl.BlockSpec((bm, k), lambda i, j: (i, 0)),
                  pl.BlockSpec((bm, k), lambda i, j: (i, 0)),
                  pl.BlockSpec((k, bn), lambda i, j: (0, j)),
                  pl.BlockSpec((k, bn), lambda i, j: (0, j)),
                  pl.BlockSpec((bm, bn), lambda i, j: (i, j)),
                  pl.BlockSpec((bm, bn), lambda i, j: (i, j + g1_off))],
        out_specs=pl.BlockSpec((bm, bn), lambda i, j: (i, j)),
        out_shape=jax.ShapeDtypeStruct((m, n), BF16),
        compiler_params=_params(("arbitrary", "arbitrary"), 56),
        name="branch_merge",
    )(oa, ob, wa, wb, gates, gates)


def _cast_body(x_ref, o_ref):
    o_ref[...] = x_ref[...].astype(o_ref.dtype)


def cast_bf16(w):
    r, c = w.shape
    return pl.pallas_call(
        _cast_body,
        grid=(1,),
        in_specs=[pl.BlockSpec((r, c), lambda i: (0, 0))],
        out_specs=pl.BlockSpec((r, c), lambda i: (0, 0)),
        out_shape=jax.ShapeDtypeStruct((r, c), BF16),
        compiler_params=_params(("arbitrary",), 40),
        name="cast_bf16",
    )(w)


def _router_probs(hn, wr_ref):
    hi = hn.astype(BF16)
    lo = (hn - hi.astype(F32)).astype(BF16)
    l_hi = jnp.dot(hi, wr_ref[...], preferred_element_type=F32)
    l_lo = jnp.dot(lo, wr_ref[...], preferred_element_type=F32)
    logits = l_hi + pltpu.roll(l_hi, LANES - N_EXPERTS, 1) + l_lo
    lane = lax.broadcasted_iota(jnp.int32, logits.shape, 1)
    logits = jnp.where(lane < N_EXPERTS, logits, NEG)
    return _softmax_rows(logits)


def _xblock_body(x_ref, gc_ref, wq_ref, k_ref, v_ref, wo_ref, gf_ref, wr_ref,
                 x2_ref, h3_ref, aff_ref):
    scale = HEAD_DIM ** -0.5
    half = x_ref.shape[1] // 2
    parts = [slice(p * XB_ROWS, (p + 1) * XB_ROWS) for p in range(x_ref.shape[0] // XB_ROWS)]
    h2s = [_rmsnorm_rows(x_ref[r, :], gc_ref[...]).astype(BF16) for r in parts]
    qs = [jnp.dot(h2, wq_ref[...], preferred_element_type=F32).astype(BF16) for h2 in h2s]
    head_cols = [slice(h * HEAD_DIM, (h + 1) * HEAD_DIM) for h in range(X_HEADS)]
    ss = [[lax.dot_general(q[:, cols], k_ref[:, cols], _NT, preferred_element_type=F32) * scale
           for cols in head_cols] for q in qs]
    ps = [[_softmax_rows(s).astype(BF16) for s in row] for row in ss]
    os = [jnp.concatenate([jnp.dot(p, v_ref[:, cols], preferred_element_type=F32).astype(BF16)
                           for p, cols in zip(row, head_cols)], axis=1) for row in ps]
    for r, o in zip(parts, os):
        x2_ref[r, :] = x_ref[r, :] + jnp.dot(o, wo_ref[...], preferred_element_type=F32)
    for r in parts:
        hn = _rmsnorm_rows(x2_ref[r, :], gf_ref[...])
        packed = pltpu.pack_elementwise([hn[:, :half], hn[:, half:]], packed_dtype=BF16)
        h3_ref[r, :] = pltpu.bitcast(packed, jnp.uint32)
        aff = _router_probs(hn, wr_ref)
        aff_ref[0, :, r] = aff.T[:N_EXPERTS, :]


XB_ROWS = 256


def cross_attention_block(x1, g_cross, wq, kx, vx, wo, g_ffn, w_router, batch, seq, mem_len, bm=512):
    m, d = x1.shape
    nt = seq // bm
    w_hi = w_router.astype(BF16)
    w_lo = (w_router - w_hi.astype(F32)).astype(BF16)
    wr = jnp.concatenate([w_hi, w_lo, jnp.zeros((d, LANES - 2 * N_EXPERTS), BF16)], axis=1)
    const = lambda i: (0, 0)
    once = pl.Buffered(1)
    return pl.pallas_call(
        _xblock_body,
        grid=(m // bm,),
        in_specs=[pl.BlockSpec((bm, d), lambda i: (i, 0)),
                  pl.BlockSpec((1, d), const),
                  pl.BlockSpec((d, X_W), const, pipeline_mode=once),
                  pl.BlockSpec((mem_len, X_W), lambda i: (i // nt, 0)),
                  pl.BlockSpec((mem_len, X_W), lambda i: (i // nt, 0)),
                  pl.BlockSpec((X_W, d), const, pipeline_mode=once),
                  pl.BlockSpec((1, d), const),
                  pl.BlockSpec((d, LANES), const, pipeline_mode=once)],
        out_specs=[pl.BlockSpec((bm, d), lambda i: (i, 0)),
                   pl.BlockSpec((bm, d // 2), lambda i: (i, 0)),
                   pl.BlockSpec((1, N_EXPERTS, bm), lambda i: (i // nt, 0, i % nt))],
        out_shape=[jax.ShapeDtypeStruct((m, d), F32),
                   jax.ShapeDtypeStruct((m, d // 2), jnp.uint32),
                   jax.ShapeDtypeStruct((batch, N_EXPERTS, seq), F32)],
        compiler_params=_params(("arbitrary",), 58),
        name="cross_attention_block",
    )(x1, g_cross.reshape(1, d), cast_bf16(wq), kx, vx, cast_bf16(wo), g_ffn.reshape(1, d), wr)


CUM_CHUNK = 256


def _excl_cumsum_lanes(x01, tri):
    n = x01.shape[1]
    carry = jnp.zeros((x01.shape[0], 1), F32)
    out = []
    for c in range(n // CUM_CHUNK):
        xc = x01[:, c * CUM_CHUNK:(c + 1) * CUM_CHUNK]
        out.append(jnp.dot(xc.astype(BF16), tri, preferred_element_type=F32) + carry)
        carry = carry + xc.sum(axis=1, keepdims=True)
    return jnp.concatenate(out, axis=1)


COMBINE_TS = 256
COMBINE_W = 64


def _topk_body(aff_ref, slot_ref, first_ref, *, cap):
    a = aff_ref[0]
    n_exp, s = a.shape
    capf = jnp.float32(cap)
    lane_s = lax.broadcasted_iota(jnp.int32, (n_exp, s), 1)

    def count(mask):
        return jnp.where(mask, 1.0, 0.0).sum(axis=1, keepdims=True)

    bits = pltpu.bitcast(a, jnp.int32)

    def search(i, t):
        cand = t | jnp.left_shift(jnp.int32(1), 30 - i)
        return jnp.where(count(bits >= cand) >= capf, cand, t)

    t = lax.fori_loop(0, 31, search, jnp.zeros((n_exp, 1), jnp.int32))
    at = jnp.where(bits == t, lane_s, s).astype(F32).min(axis=1, keepdims=True).astype(jnp.int32)
    pivot = jnp.where(lane_s == at, a, 0.0).sum(axis=1, keepdims=True)

    def stats(p):
        return p, count(a > p), count(a >= p)

    def wrong(state):
        _, n_gt, n_ge = state
        return jnp.where(jnp.logical_or(n_gt >= capf, n_ge < capf), 1.0, 0.0).sum() > 0.0

    def step(state):
        p, n_gt, n_ge = state
        up = jnp.where(a > p, a, jnp.inf).min(axis=1, keepdims=True)
        down = jnp.where(a < p, a, -jnp.inf).max(axis=1, keepdims=True)
        return stats(jnp.where(n_gt >= capf, up, jnp.where(n_ge < capf, down, p)))

    pivot, n_gt, _ = lax.while_loop(wrong, step, stats(pivot))
    ri = lax.broadcasted_iota(jnp.int32, (CUM_CHUNK, CUM_CHUNK), 0)
    cj = lax.broadcasted_iota(jnp.int32, (CUM_CHUNK, CUM_CHUNK), 1)
    tri = jnp.where(ri < cj, 1.0, 0.0).astype(BF16)
    eq = jnp.where(a == pivot, 1.0, 0.0)
    sel = jnp.where(a > pivot, 1.0, 0.0) + jnp.where(_excl_cumsum_lanes(eq, tri) < capf - n_gt, eq, 0.0)
    pos = _excl_cumsum_lanes(sel, tri)
    slot_ref[0] = jnp.where(sel > 0.5, pos, -1.0).astype(jnp.int32)
    lane_k = lax.broadcasted_iota(jnp.int32, (n_exp, LANES), 1)
    first = jnp.zeros((n_exp, LANES), F32)
    for k in range(s // COMBINE_TS):
        at_k = jnp.where(lane_s == k * COMBINE_TS, pos, 0.0).sum(axis=1, keepdims=True)
        first = jnp.where(lane_k == k, at_k, first)
    first_ref[0] = first.astype(jnp.int32)


def expert_topk(aff_t, cap):
    b, e, s = aff_t.shape
    return pl.pallas_call(
        functools.partial(_topk_body, cap=cap),
        grid=(b,),
        in_specs=[pl.BlockSpec((1, e, s), lambda i: (i, 0, 0))],
        out_specs=[pl.BlockSpec((1, e, s), lambda i: (i, 0, 0)),
                   pl.BlockSpec((1, e, LANES), lambda i: (i, 0, 0))],
        out_shape=[jax.ShapeDtypeStruct((b, e, s), jnp.int32),
                   jax.ShapeDtypeStruct((b, e, LANES), jnp.int32)],
        compiler_params=_params(("arbitrary",), 32),
        name="expert_topk",
    )(aff_t)


TOK_RADIX = 64


def _slot_index_body(slot_ref, aff_ref, idx_ref, val_ref, *, cap):
    b = pl.program_id(0)
    n_exp, s = slot_ref.shape[1], slot_ref.shape[2]
    ci = lax.broadcasted_iota(jnp.int32, (cap, s), 0)
    tok = lax.broadcasted_iota(jnp.int32, (1, s), 1)
    digits = [(tok // TOK_RADIX).astype(F32).astype(BF16), (tok % TOK_RADIX).astype(F32).astype(BF16)]
    pad = [jnp.zeros((1, s), BF16)] * 3
    base = (b * s).astype(F32)
    for e in range(n_exp):
        onehot = jnp.where(slot_ref[0, e:e + 1, :] == ci, 1.0, 0.0).astype(BF16)
        a = aff_ref[0, e:e + 1, :]
        a1 = a.astype(BF16)
        r1 = a - a1.astype(F32)
        a2 = r1.astype(BF16)
        a3 = (r1 - a2.astype(F32)).astype(BF16)
        picked = lax.dot_general(onehot, jnp.concatenate(digits + [a1, a2, a3] + pad, axis=0), _NT,
                                 preferred_element_type=F32)
        idx = picked[:, 0:1] * TOK_RADIX + picked[:, 1:2] + base
        idx_ref[e] = idx.astype(jnp.int32)
        val_ref[e] = picked[:, 2:3] + picked[:, 3:4] + picked[:, 4:5]


def slot_index(slot, aff_t, cap):
    b, e, s = slot.shape
    return pl.pallas_call(
        functools.partial(_slot_index_body, cap=cap),
        grid=(b,),
        in_specs=[pl.BlockSpec((1, e, s), lambda bi: (bi, 0, 0)),
                  pl.BlockSpec((1, e, s), lambda bi: (bi, 0, 0))],
        out_specs=[pl.BlockSpec((e, cap, 1), lambda bi: (0, bi, 0)),
                   pl.BlockSpec((e, cap, 1), lambda bi: (0, bi, 0))],
        out_shape=[jax.ShapeDtypeStruct((e, b * cap, 1), jnp.int32),
                   jax.ShapeDtypeStruct((e, b * cap, 1), F32)],
        compiler_params=_params(("arbitrary",), 32),
        name="slot_index",
    )(slot, aff_t)


UNPACK_ROWS = 128


def _row_copy(idx_ref, hp_ref, gbuf, sem, expert, rows, row):
    tok = idx_ref[expert * rows + row]
    return pltpu.make_async_copy(hp_ref.at[pl.ds(tok, 1)], gbuf.at[pl.ds(row, 1)], sem)


def _expert_up_body(idx_ref, hp_ref, wg_ref, wu_ref, o_ref, gbuf, xbf, sem, *, rows, per_step):
    e, f = pl.program_id(0), pl.program_id(1)
    n_e, n_f = pl.num_programs(0), pl.num_programs(1)
    half = gbuf.shape[1]

    def wait_all_rows():
        pltpu.make_async_copy(hp_ref.at[pl.ds(0, rows)], gbuf, sem).wait()

    @pl.when(jnp.logical_and(e == 0, f == 0))
    def _():
        def body(r, carry):
            _row_copy(idx_ref, hp_ref, gbuf, sem, 0, rows, r).start()
            return carry
        lax.fori_loop(0, rows, body, 0)

    @pl.when(f == 0)
    def _():
        wait_all_rows()

        def unpack(k, carry):
            r = pl.multiple_of(k * UNPACK_ROWS, UNPACK_ROWS)
            w = gbuf[pl.ds(r, UNPACK_ROWS), :]
            lo = pltpu.unpack_elementwise(w, index=0, packed_dtype=BF16, unpacked_dtype=F32)
            hi = pltpu.unpack_elementwise(w, index=1, packed_dtype=BF16, unpacked_dtype=F32)
            xbf[pl.ds(r, UNPACK_ROWS), :half] = lo.astype(BF16)
            xbf[pl.ds(r, UNPACK_ROWS), half:] = hi.astype(BF16)
            return carry
        lax.fori_loop(0, rows // UNPACK_ROWS, unpack, 0)

    nxt = jnp.minimum(e + 1, n_e - 1)
    for r in range(per_step):
        _row_copy(idx_ref, hp_ref, gbuf, sem, nxt, rows, f * per_step + r).start()

    wg, wu = wg_ref[0].astype(BF16), wu_ref[0].astype(BF16)
    for part in _row_parts(rows):
        a = jnp.dot(xbf[part, :], wg, preferred_element_type=F32)
        u = jnp.dot(xbf[part, :], wu, preferred_element_type=F32)
        o_ref[0, part, :] = (a * _sigmoid(a) * u).astype(o_ref.dtype)

    @pl.when(jnp.logical_and(e == n_e - 1, f == n_f - 1))
    def _():
        wait_all_rows()


def expert_up(idx, hp, w_gate, w_up, rows, tf=256):
    n_e, d, f = w_gate.shape
    assert hp.shape[1] * 2 == d and rows % (f // tf) == 0
    grid_spec = pltpu.PrefetchScalarGridSpec(
        num_scalar_prefetch=1,
        grid=(n_e, f // tf),
        in_specs=[pl.BlockSpec(memory_space=pl.ANY),
                  pl.BlockSpec((1, d, tf), lambda ei, fi, idx_ref: (ei, 0, fi)),
                  pl.BlockSpec((1, d, tf), lambda ei, fi, idx_ref: (ei, 0, fi))],
        out_specs=pl.BlockSpec((1, rows, tf), lambda ei, fi, idx_ref: (ei, 0, fi)),
        scratch_shapes=[pltpu.VMEM((rows, d // 2), jnp.uint32),
                        pltpu.VMEM((rows, d), BF16),
                        pltpu.SemaphoreType.DMA(())],
    )
    return pl.pallas_call(
        functools.partial(_expert_up_body, rows=rows, per_step=rows // (f // tf)),
        grid_spec=grid_spec,
        out_shape=jax.ShapeDtypeStruct((n_e, rows, f), BF16),
        compiler_params=_params(("arbitrary", "arbitrary"), 56),
        name="expert_up",
    )(idx, hp, w_gate, w_up)


def _expert_down_body(h_ref, wd_ref, val_ref, o_ref):
    wd = wd_ref[0].astype(BF16)
    for part in _row_parts(h_ref.shape[1]):
        y = jnp.dot(h_ref[0, part, :], wd, preferred_element_type=F32)
        o_ref[0, part, :] = (y * val_ref[0, part, :]).astype(o_ref.dtype)


def expert_down(hmid, w_down, valc, tn=1024):
    e, rows, f = hmid.shape
    d = w_down.shape[-1]
    return pl.pallas_call(
        _expert_down_body,
        grid=(e, d // tn),
        in_specs=[pl.BlockSpec((1, rows, f), lambda ei, ni: (ei, 0, 0)),
                  pl.BlockSpec((1, f, tn), lambda ei, ni: (ei, 0, ni)),
                  pl.BlockSpec((1, rows, 1), lambda ei, ni: (ei, 0, 0))],
        out_specs=pl.BlockSpec((1, rows, tn), lambda ei, ni: (ei, 0, ni)),
        out_shape=jax.ShapeDtypeStruct((e, rows, d), BF16),
        compiler_params=_params(("arbitrary", "arbitrary"), 48),
        name="expert_down",
    )(hmid, w_down, valc)


COMBINE_TN = 512
ROW_ALIGN_BF16 = 16


def _combine_windows(first, cap, n_rows):
    b, e, nt = first.shape
    row0 = jnp.arange(b, dtype=jnp.int32)[:, None, None] * cap
    lo = first + row0
    hi = jnp.concatenate([first[:, :, 1:], jnp.full((b, e, 1), cap, jnp.int32)], axis=2) + row0
    start = jnp.minimum(lo // ROW_ALIGN_BF16 * ROW_ALIGN_BF16, n_rows - COMBINE_W)
    fast = jnp.all(hi - start <= COMBINE_W, axis=1)
    return start.transpose(0, 2, 1).reshape(-1), fast.astype(jnp.int32).reshape(-1)


def _combine_body(win_ref, fast_ref, slot_ref, y_hbm, x_ref, g_ref, o_ref,
                  ywin, oht, ybuf, ohs, wsem, ssem, *, cap, n_tiles):
    i = pl.program_id(0)
    n_exp = slot_ref.shape[1]
    ts, d = o_ref.shape
    w = COMBINE_W
    row0 = (i // n_tiles) * cap
    buf = i % 2

    def window_copy(step, e, b):
        start = pl.multiple_of(win_ref[step * n_exp + e], ROW_ALIGN_BF16)
        return pltpu.make_async_copy(y_hbm.at[e, pl.ds(start, w)], ywin.at[b, pl.ds(e * w, w)],
                                     wsem.at[b])

    @pl.when(i == 0)
    def _():
        for e in range(n_exp):
            window_copy(0, e, 0).start()

    @pl.when(i + 1 < pl.num_programs(0))
    def _():
        for e in range(n_exp):
            window_copy(i + 1, e, 1 - buf).start()

    pltpu.make_async_copy(y_hbm.at[0, pl.ds(0, n_exp * w)], ywin.at[buf], wsem.at[buf]).wait()

    @pl.when(fast_ref[i] == 1)
    def _():
        row = lax.broadcasted_iota(jnp.int32, (2 * w, ts), 0)
        upper = row >= w
        j = row % w
        for p in range(n_exp // 2):
            rel_a = slot_ref[0, 2 * p:2 * p + 1, :] + (row0 - win_ref[i * n_exp + 2 * p])
            rel_b = slot_ref[0, 2 * p + 1:2 * p + 2, :] + (row0 - win_ref[i * n_exp + 2 * p + 1])
            hit = jnp.where(upper, rel_b, rel_a) == j
            oht[:, p * 2 * w:(p + 1) * 2 * w] = jnp.where(hit, 1.0, 0.0).T.astype(BF16)
        for c in range(d // COMBINE_TN):
            cols = slice(c * COMBINE_TN, (c + 1) * COMBINE_TN)
            o_ref[:, cols] = x_ref[:, cols] + jnp.dot(oht[...], ywin[buf, :, cols],
                                                      preferred_element_type=F32)

    @pl.when(fast_ref[i] == 0)
    def _():
        o_ref[...] = x_ref[...]
        ci = lax.broadcasted_iota(jnp.int32, (cap, ts), 0)

        def body(e, carry):
            cp = pltpu.make_async_copy(y_hbm.at[e, pl.ds(pl.multiple_of(row0, cap), cap)], ybuf, ssem)
            cp.start()
            cp.wait()
            srow = slot_ref[0, pl.ds(e, 1), :]
            ohs[...] = jnp.where(srow == ci, 1.0, 0.0).T.astype(BF16)
            for c in range(d // COMBINE_TN):
                cols = slice(c * COMBINE_TN, (c + 1) * COMBINE_TN)
                o_ref[:, cols] += jnp.dot(ohs[...], ybuf[:, cols], preferred_element_type=F32)
            return carry
        lax.fori_loop(0, n_exp, body, 0)

    o_ref[...] = _rmsnorm_rows(o_ref[...], g_ref[...])


def expert_combine(slot, first, y, x2d, g, cap):
    b, e, s = slot.shape
    d = x2d.shape[-1]
    ts, w = COMBINE_TS, COMBINE_W
    nt = s // ts
    assert e % 2 == 0 and y.shape[1] >= e * w and cap % ROW_ALIGN_BF16 == 0
    win, fast = _combine_windows(first[:, :, :nt], cap, y.shape[1])
    grid_spec = pltpu.PrefetchScalarGridSpec(
        num_scalar_prefetch=2,
        grid=(b * nt,),
        in_specs=[pl.BlockSpec((1, e, ts), lambda i, win_ref, fast_ref: (i // nt, 0, i % nt)),
                  pl.BlockSpec(memory_space=pl.ANY),
                  pl.BlockSpec((ts, d), lambda i, win_ref, fast_ref: (i, 0)),
                  pl.BlockSpec((1, d), lambda i, win_ref, fast_ref: (0, 0))],
        out_specs=pl.BlockSpec((ts, d), lambda i, win_ref, fast_ref: (i, 0)),
        scratch_shapes=[pltpu.VMEM((2, e * w, d), BF16),
                        pltpu.VMEM((ts, e * w), BF16),
                        pltpu.VMEM((cap, d), BF16),
                        pltpu.VMEM((ts, cap), BF16),
                        pltpu.SemaphoreType.DMA((2,)),
                        pltpu.SemaphoreType.DMA(())],
    )
    return pl.pallas_call(
        functools.partial(_combine_body, cap=cap, n_tiles=nt),
        grid_spec=grid_spec,
        out_shape=jax.ShapeDtypeStruct(x2d.shape, F32),
        compiler_params=_params(("arbitrary",), 48),
        name="expert_combine",
    )(win, fast, slot, y, x2d, g.reshape(1, d))


def _rotary_tables(seq):
    half = ROT_DIM // 2
    inv = ROPE_THETA ** (-jnp.arange(half, dtype=F32) * 2.0 / ROT_DIM)
    ang = jnp.arange(seq).astype(F32)[:, None] * inv[None, :]
    cos, sin = jnp.cos(ang), jnp.sin(ang)
    ones = jnp.ones((seq, HEAD_DIM - ROT_DIM), F32)
    zeros = jnp.zeros((seq, HEAD_DIM - ROT_DIM), F32)
    zh = jnp.zeros((seq, half), F32)
    c = jnp.concatenate([cos, cos, ones], axis=1)
    s1 = jnp.concatenate([-sin, zh, zeros], axis=1)
    s2 = jnp.concatenate([zh, sin, zeros], axis=1)
    return c, s1, s2


def kernel(x, mem, norm_mix, w_in, b_gate, sink, rpb, w_branch_a, w_branch_b, w_out,
           norm_cross, norm_mem, wq_x, wk_x, wv_x, wo_x, norm_ffn, w_router,
           w_gate, w_up, w_down, norm_final):
    batch, seq, d = x.shape
    mem_len = mem.shape[1]
    m = batch * seq
    assert norm_mix.shape[0] == 1, "final RMSNorm is fused into the single layer's last kernel"
    cap = EC_CAPACITY * seq // N_EXPERTS
    bm, bn = 2048, 256
    sb = seq // bm
    x0 = x.reshape(m, d)

    h = rmsnorm(x0, norm_mix[0], BF16)
    rot = _rotary_tables(seq)
    rot_specs = [pl.BlockSpec((bm, HEAD_DIM), lambda i, j: (i % sb, 0))] * 3
    qk = matmul_rows(h, w_in[0], col_off=0, n_cols=QA_W + KVA_W, bm=bm, bn=bn, out_dtype=BF16,
                     epilogue=_ep_rotary, extras=rot, extra_specs=rot_specs, row_split=2 * ROW_SPLIT,
                     name="in_proj_rotary")
    vqkv = matmul_rows(h, w_in[0], col_off=QA_W + KVA_W, n_cols=KVA_W + 3 * QB_W, bm=bm, bn=bn,
                       out_dtype=BF16, name="in_proj_plain")
    g_off = QA_W + 2 * KVA_W + 3 * QB_W
    gates = matmul_rows(h, w_in[0], col_off=g_off, n_cols=2 * d, bm=bm, bn=bn, out_dtype=BF16,
                        epilogue=_ep_sigmoid, extras=(b_gate[0].reshape(1, 2 * d),),
                        extra_specs=[pl.BlockSpec((1, bn), lambda i, j: (0, j))], name="in_proj_gates")
    oa = window_attention(qk, vqkv, sink[0], batch, seq)
    ob = neighbourhood_attention(vqkv, _nbr_bias_table(rpb[0], seq), batch, seq)
    merged = branch_merge(oa, ob, w_branch_a[0], w_branch_b[0], gates, bm=bm, bn=bn)
    res_spec = [pl.BlockSpec((bm, bn), lambda i, j: (i, j))]
    x1 = matmul_rows(merged, w_out[0], col_off=0, n_cols=d, bm=bm, bn=bn, out_dtype=F32,
                     epilogue=_ep_residual, extras=(x0,), extra_specs=res_spec, name="out_proj")

    mem_rows = batch * mem_len
    mn = rmsnorm(mem.reshape(mem_rows, d), norm_mem[0], BF16)
    kx = matmul_rows(mn, wk_x[0], col_off=0, n_cols=X_W, bm=mem_rows, bn=bn, out_dtype=BF16, name="xattn_k")
    vx = matmul_rows(mn, wv_x[0], col_off=0, n_cols=X_W, bm=mem_rows, bn=bn, out_dtype=BF16, name="xattn_v")
    x2, h3p, aff_t = cross_attention_block(
        x1, norm_cross[0], wq_x[0], kx, vx, wo_x[0], norm_ffn[0], w_router[0], batch, seq, mem_len)

    slot, first = expert_topk(aff_t, cap)
    idx, valc = slot_index(slot, aff_t, cap)
    hmid = expert_up(idx.reshape(-1), h3p, w_gate[0], w_up[0], batch * cap)
    y = expert_down(hmid, w_down[0], valc)
    out = expert_combine(slot, first, y, x2, norm_final, cap)
    return out.reshape(batch, seq, d)
```
